```python
import math
import jax, jax.numpy as jnp
from jax import lax
import numpy as np

D_MODEL = 1024
BATCH = 32
SEQ = 2048
DEPTH = 1
DEC_BATCH = 16
DEC_SEQ = 2048
PAST_LEN = 128

PLE_DIM = 256
EPS = 1e-6
CONV_K = 4
SSD_HEADS = 16
SSD_HEAD_DIM = 64
SSD_INNER = SSD_HEADS * SSD_HEAD_DIM
SSD_GROUPS = 4
SSD_STATE = 128
SSD_XBC = SSD_INNER + 2 * SSD_GROUPS * SSD_STATE
SSD_CHUNK = 128
GDN_HEADS = 8
GDN_HEAD_DIM = 128
GDN_INNER = GDN_HEADS * GDN_HEAD_DIM
GDN_CHUNK = 64
PEER_HEADS = 8
PEER_KEYS = 128
PEER_EXPERTS = PEER_KEYS * PEER_KEYS
PEER_QDIM = 256
PEER_TOPK = 16
PEER_BLOCK = 128
IN_SIZES = (SSD_INNER, SSD_XBC, 2 * SSD_HEADS, 3 * GDN_INNER, GDN_INNER, 2 * GDN_HEADS, 2 * GDN_HEADS, D_MODEL, D_MODEL)
IN_COLS = sum(IN_SIZES)
IN_SPLITS = tuple(int(s) for s in np.cumsum(IN_SIZES)[:-1])

kernel_name = 'hybrid_bissd_bigdn_peer_encoder'


def rmsnorm(x, g):
    xf = x.astype(jnp.float32)
    y = xf * lax.rsqrt(jnp.mean(xf * xf, axis=-1, keepdims=True) + EPS)
    return (y * g.astype(jnp.float32)).astype(x.dtype)


def centred_depthwise_conv(x, w):
    kk = w.shape[0]
    left = kk // 2
    l = x.shape[1]
    xp = jnp.pad(x, ((0, 0), (left, kk - 1 - left), (0, 0)))
    y = xp[:, 0:l] * w[0]
    for j in range(1, kk):
        y = y + xp[:, j:j + l] * w[j]
    return y


def l2norm(t):
    return t * lax.rsqrt(jnp.sum(t * t, axis=-1, keepdims=True) + EPS)


def ssd_chunked_scan(x, dt, a, bm, cm):
    b, l, h, p = x.shape
    g, n = bm.shape[2], bm.shape[3]
    r = h // g
    q = SSD_CHUNK
    c = l // q
    xd = (x * dt[..., None]).reshape(b, c, q, g, r, p)
    a_cum = jnp.cumsum((dt * a).reshape(b, c, q, g, r), axis=2)
    bc = bm.reshape(b, c, q, g, n)
    cc = cm.reshape(b, c, q, g, n)
    lower = jnp.tril(jnp.ones((q, q), bool))[:, :, None, None]
    seg = a_cum[:, :, :, None] - a_cum[:, :, None, :]
    decay_ls = jnp.exp(jnp.where(lower, seg, -jnp.inf))
    scores = jnp.einsum('bclgn,bcsgn->bclsg', cc, bc)
    mix = scores[..., None] * decay_ls
    y_diag = jnp.einsum('bclsgr,bcsgrp->bclgrp', mix, xd)
    decay_to_end = jnp.exp(a_cum[:, :, -1:] - a_cum)
    chunk_states = jnp.einsum('bcsgn,bcsgrp->bcgrpn', bc, xd * decay_to_end[..., None])
    chunk_decay = jnp.exp(a_cum[:, :, -1])

    def step(state, inp):
        st, dec = inp
        return state * dec[..., None, None] + st, state

    init = jnp.zeros((b, g, r, p, n), x.dtype)
    _, prev = lax.scan(step, init, (jnp.moveaxis(chunk_states, 1, 0), jnp.moveaxis(chunk_decay, 1, 0)))
    prev = jnp.moveaxis(prev, 0, 1)
    y_off = jnp.einsum('bclgn,bcgrpn->bclgrp', cc, prev) * jnp.exp(a_cum)[..., None]
    return (y_diag + y_off).reshape(b, l, h, p)


def ssd_mixer(z, xbc, dt_raw, conv_w, conv_b, a_log, dt_bias, d_skip, norm_w):
    f32 = jnp.float32
    b, l, _ = z.shape
    xbc = jax.nn.silu((centred_depthwise_conv(xbc, conv_w) + conv_b).astype(f32))
    xs, bm, cm = jnp.split(xbc, [SSD_INNER, SSD_INNER + SSD_GROUPS * SSD_STATE], axis=-1)
    xs = xs.reshape(b, l, SSD_HEADS, SSD_HEAD_DIM)
    bm = bm.reshape(b, l, SSD_GROUPS, SSD_STATE)
    cm = cm.reshape(b, l, SSD_GROUPS, SSD_STATE)
    dt_f, dt_b = jnp.split(dt_raw.astype(f32), 2, axis=-1)
    dt_bias = dt_bias.astype(f32)
    dt_f = jax.nn.softplus(dt_f + dt_bias[0])
    dt_b = jax.nn.softplus(dt_b + dt_bias[1])
    a = -jnp.exp(a_log.astype(f32))
    fl = lambda t: jnp.flip(t, axis=1)
    y = ssd_chunked_scan(xs, dt_f, a[0], bm, cm) + fl(ssd_chunked_scan(fl(xs), fl(dt_b), a[1], fl(bm), fl(cm)))
    y = y + xs * d_skip.astype(f32)[:, None]
    y = y.reshape(b, l, SSD_INNER) * jax.nn.silu(z.astype(f32))
    yg = y.reshape(b, l, SSD_GROUPS, SSD_INNER // SSD_GROUPS)
    yg = yg * lax.rsqrt(jnp.mean(yg * yg, axis=-1, keepdims=True) + EPS)
    return (yg.reshape(b, l, SSD_INNER) * norm_w.astype(f32)).astype(z.dtype)


def gated_delta_chunked(q, k, v, g, beta):
    b, h, l, dk = q.shape
    dv = v.shape[-1]
    c = GDN_CHUNK
    n = l // c
    q = q * dk ** -0.5
    q, k, v = (t.reshape(b, h, n, c, t.shape[-1]) for t in (q, k, v))
    g, beta = (t.reshape(b, h, n, c) for t in (g, beta))
    kb = k * beta[..., None]
    vb = v * beta[..., None]
    dec = jnp.cumsum(g, axis=-1)
    incl = jnp.tril(jnp.ones((c, c), bool))
    strict = jnp.tril(jnp.ones((c, c), bool), -1)
    gam = jnp.exp(jnp.where(incl, dec[..., :, None] - dec[..., None, :], -jnp.inf))
    a_low = jnp.where(strict, jnp.einsum('bhnid,bhnjd->bhnij', kb, k) * gam, 0.0)
    eye = jnp.eye(c, dtype=q.dtype)
    t_inv = lax.linalg.triangular_solve(a_low + eye, jnp.broadcast_to(eye, a_low.shape),
                                        left_side=True, lower=True, unit_diagonal=True)
    u = t_inv @ vb
    w = t_inv @ (kb * jnp.exp(dec)[..., None])
    qk = jnp.einsum('bhnid,bhnjd->bhnij', q, k) * gam
    q_dec = q * jnp.exp(dec)[..., None]
    k_dec = k * jnp.exp(dec[..., -1:] - dec)[..., None]
    last = jnp.exp(dec[..., -1])

    def step(s, inp):
        u_c, w_c, qk_c, qd_c, kd_c, last_c = inp
        v_new = u_c - jnp.einsum('bhck,bhkv->bhcv', w_c, s)
        o = jnp.einsum('bhck,bhkv->bhcv', qd_c, s) + jnp.einsum('bhcs,bhsv->bhcv', qk_c, v_new)
        s = s * last_c[..., None, None] + jnp.einsum('bhck,bhcv->bhkv', kd_c, v_new)
        return s, o

    xs = tuple(jnp.moveaxis(t, 2, 0) for t in (u, w, qk, q_dec, k_dec, last))
    _, o = lax.scan(step, jnp.zeros((b, h, dk, dv), q.dtype), xs)
    return jnp.moveaxis(o, 0, 2).reshape(b, h, l, dv)


def gdn_mixer(qkv, z, a_raw, beta_raw, conv_w, a_log, dt_bias, norm_w):
    f32 = jnp.float32
    b, l, _ = z.shape
    qkv = jax.nn.silu(centred_depthwise_conv(qkv, conv_w).astype(f32))
    q, k, v = jnp.split(qkv, 3, axis=-1)
    heads = lambda t: t.reshape(b, l, GDN_HEADS, GDN_HEAD_DIM).transpose(0, 2, 1, 3)
    q, k, v = l2norm(heads(q)), l2norm(heads(k)), heads(v)
    a_f, a_b = jnp.split(a_raw.astype(f32), 2, axis=-1)
    beta_f, beta_b = jnp.split(jax.nn.sigmoid(beta_raw.astype(f32)), 2, axis=-1)
    a_coef = jnp.exp(a_log.astype(f32))
    dt_bias = dt_bias.astype(f32)
    g_f = -a_coef[0] * jax.nn.softplus(a_f + dt_bias[0])
    g_b = -a_coef[1] * jax.nn.softplus(a_b + dt_bias[1])
    tr = lambda t: t.transpose(0, 2, 1)
    fl = lambda t: jnp.flip(t, axis=2)
    o = gated_delta_chunked(q, k, v, tr(g_f), tr(beta_f)) + fl(
        gated_delta_chunked(fl(q), fl(k), fl(v), fl(tr(g_b)), fl(tr(beta_b))))
    o = o.transpose(0, 2, 1, 3)
    o = o * lax.rsqrt(jnp.mean(o * o, axis=-1, keepdims=True) + EPS) * norm_w.astype(f32)
    o = o.reshape(b, l, GDN_INNER) * jax.nn.silu(z.astype(f32))
    return o.astype(z.dtype)


def peer_ffn(x, w_query, sub_keys, expert_u, expert_v):
    f32 = jnp.float32
    b, l, d = x.shape
    t = x.reshape(b * l, d)
    nt = t.shape[0]
    qry = (t @ w_query).reshape(nt, PEER_HEADS, 2, PEER_QDIM // 2)
    s = jnp.einsum('thzd,hzkd->thzk', qry.astype(f32), sub_keys.astype(f32))
    top_s, top_i = lax.top_k(s, PEER_TOPK)
    kk = PEER_TOPK * PEER_TOPK
    cand_s = (top_s[:, :, 0, :, None] + top_s[:, :, 1, None, :]).reshape(nt, PEER_HEADS, kk)
    cand_i = (top_i[:, :, 0, :, None] * PEER_KEYS + top_i[:, :, 1, None, :]).reshape(nt, PEER_HEADS, kk)
    best_s, best_pos = lax.top_k(cand_s, PEER_TOPK)
    idx = jnp.take_along_axis(cand_i, best_pos, axis=-1)
    gate = jax.nn.softmax(best_s, axis=-1)
    nb = nt // PEER_BLOCK

    def block(args):
        tb, ib, gb = args
        act = jax.nn.gelu(jnp.einsum('td,thkd->thk', tb, expert_u[ib]).astype(f32), approximate=False)
        wts = (act * gb).astype(tb.dtype)
        return jnp.einsum('thk,thkd->td', wts, expert_v[ib])

    out = lax.map(block, (t.reshape(nb, PEER_BLOCK, d),
                          idx.reshape(nb, PEER_BLOCK, PEER_HEADS, PEER_TOPK),
                          gate.reshape(nb, PEER_BLOCK, PEER_HEADS, PEER_TOPK)))
    return out.reshape(b, l, d).astype(x.dtype)


def encoder_layer(h, ple, W, i):
    f32 = jnp.float32
    n1 = rmsnorm(h, W['norm_mix'][i])
    proj = n1 @ W['w_in'][i]
    ssd_z, ssd_xbc, ssd_dt, gdn_qkv, gdn_z, gdn_a, gdn_beta, gate_a, gate_b = jnp.split(proj, IN_SPLITS, axis=-1)
    y_a = ssd_mixer(ssd_z, ssd_xbc, ssd_dt, W['ssd_conv_w'][i], W['ssd_conv_b'][i], W['ssd_a_log'][i],
                    W['ssd_dt_bias'][i], W['ssd_d'][i], W['ssd_norm'][i]) @ W['w_ssd_out'][i]
    y_b = gdn_mixer(gdn_qkv, gdn_z, gdn_a, gdn_beta, W['gdn_conv_w'][i], W['gdn_a_log'][i],
                    W['gdn_dt_bias'][i], W['gdn_norm'][i]) @ W['w_gdn_out'][i]
    merged = (jax.nn.sigmoid(gate_a.astype(f32)) * y_a.astype(f32)
              + jax.nn.sigmoid(gate_b.astype(f32)) * y_b.astype(f32))
    h = h + (merged.astype(h.dtype) @ W['w_out'][i]).astype(h.dtype)
    h = h + peer_ffn(rmsnorm(h, W['norm_ffn'][i]), W['peer_query'][i], W['peer_keys'][i],
                     W['expert_u'][i], W['expert_v'][i]).astype(h.dtype)
    ple_gate = jax.nn.sigmoid((rmsnorm(h, W['norm_ple'][i]) @ W['w_ple_gate'][i]).astype(f32))
    h = h + (ple_gate * (ple @ W['w_ple_proj'][i]).astype(f32)).astype(h.dtype)
    return h


def encoder_trunk(x, ple, W, norm_final):
    h = x
    for i in range(DEPTH):
        h = encoder_layer(h, ple[i], W, i)
    return rmsnorm(h, norm_final)


def setup_inputs(seed: int = 0) -> dict:
    key = jax.random.key(seed)
    ks = iter(jax.random.split(key, 40))
    f32 = jnp.float32
    L = DEPTH

    def nrm(shape, scale):
        return jax.random.normal(next(ks), shape, f32) * scale

    def gain(shape):
        return 1.0 + nrm(shape, 0.02)

    def log_uniform(shape, lo, hi):
        return jnp.log(jax.random.uniform(next(ks), shape, f32, lo, hi))

    def dt_bias_init(shape):
        dt = jnp.exp(jax.random.uniform(next(ks), shape, f32, math.log(1e-3), math.log(1e-1)))
        return dt + jnp.log(-jnp.expm1(-dt))

    return {
        'x_prompt': nrm((BATCH, SEQ, D_MODEL), 1.0),
        'x_sample': nrm((DEC_BATCH, DEC_SEQ, D_MODEL), 1.0),
        'p_prompt': nrm((DEPTH, BATCH, SEQ, PLE_DIM), 1.0),
        'p_sample': nrm((DEPTH, DEC_BATCH, DEC_SEQ, PLE_DIM), 1.0),
        'norm_mix': gain((L, D_MODEL)),
        'w_in': nrm((L, D_MODEL, IN_COLS), D_MODEL ** -0.5),
        'ssd_conv_w': nrm((L, CONV_K, SSD_XBC), CONV_K ** -0.5),
        'ssd_conv_b': nrm((L, SSD_XBC), 0.02),
        'ssd_a_log': log_uniform((L, 2, SSD_HEADS), 1.0, 16.0),
        'ssd_dt_bias': dt_bias_init((L, 2, SSD_HEADS)),
        'ssd_d': 1.0 + nrm((L, SSD_HEADS), 0.1),
        'ssd_norm': gain((L, SSD_INNER)),
        'w_ssd_out': nrm((L, SSD_INNER, D_MODEL), SSD_INNER ** -0.5),
        'gdn_conv_w': nrm((L, CONV_K, 3 * GDN_INNER), CONV_K ** -0.5),
        'gdn_a_log': log_uniform((L, 2, GDN_HEADS), 1.0, 16.0),
        'gdn_dt_bias': dt_bias_init((L, 2, GDN_HEADS)),
        'gdn_norm': gain((L, GDN_HEAD_DIM)),
        'w_gdn_out': nrm((L, GDN_INNER, D_MODEL), GDN_INNER ** -0.5),
        'w_out': nrm((L, D_MODEL, D_MODEL), D_MODEL ** -0.5),
        'norm_ffn': gain((L, D_MODEL)),
        'peer_query': nrm((L, D_MODEL, PEER_HEADS * PEER_QDIM), D_MODEL ** -0.5),
        'peer_keys': nrm((L, PEER_HEADS, 2, PEER_KEYS, PEER_QDIM // 2), (PEER_QDIM // 2) ** -0.5),
        'expert_u': nrm((L, PEER_EXPERTS, D_MODEL), D_MODEL ** -0.5),
        'expert_v': nrm((L, PEER_EXPERTS, D_MODEL), PEER_HEADS ** -0.5),
        'norm_ple': gain((L, D_MODEL)),
        'w_ple_gate': nrm((L, D_MODEL, D_MODEL), D_MODEL ** -0.5),
        'w_ple_proj': nrm((L, PLE_DIM, D_MODEL), PLE_DIM ** -0.5),
        'norm_final': gain((D_MODEL,)),
    }


def reference(x_prompt, x_sample, p_prompt, p_sample, norm_mix, w_in, ssd_conv_w, ssd_conv_b, ssd_a_log,
              ssd_dt_bias, ssd_d, ssd_norm, w_ssd_out, gdn_conv_w, gdn_a_log, gdn_dt_bias, gdn_norm, w_gdn_out,
              w_out, norm_ffn, peer_query, peer_keys, expert_u, expert_v, norm_ple, w_ple_gate, w_ple_proj,
              norm_final):
    W = dict(norm_mix=norm_mix, w_in=w_in, ssd_conv_w=ssd_conv_w, ssd_conv_b=ssd_conv_b, ssd_a_log=ssd_a_log,
             ssd_dt_bias=ssd_dt_bias, ssd_d=ssd_d, ssd_norm=ssd_norm, w_ssd_out=w_ssd_out, gdn_conv_w=gdn_conv_w,
             gdn_a_log=gdn_a_log, gdn_dt_bias=gdn_dt_bias, gdn_norm=gdn_norm, w_gdn_out=w_gdn_out, w_out=w_out,
             norm_ffn=norm_ffn, peer_query=peer_query, peer_keys=peer_keys, expert_u=expert_u, expert_v=expert_v,
             norm_ple=norm_ple, w_ple_gate=w_ple_gate, w_ple_proj=w_ple_proj)
    y_prompt = encoder_trunk(x_prompt, p_prompt, W, norm_final)
    y_sample = encoder_trunk(x_sample, p_sample, W, norm_final)
    return (y_prompt, y_sample)
```

```python
import functools

import jax
import jax.numpy as jnp
from jax import lax
from jax.experimental import pallas as pl
from jax.experimental.pallas import tpu as pltpu

F32 = jnp.float32
BF16 = jnp.bfloat16
I32 = jnp.int32

EPS = 1e-6
CONV_K = 4
SSD_HEADS = 16
SSD_HEAD_DIM = 64
SSD_GROUPS = 4
SSD_STATE = 128
SSD_INNER = SSD_HEADS * SSD_HEAD_DIM
GDN_HEADS = 8
GDN_HEAD_DIM = 128
GDN_INNER = GDN_HEADS * GDN_HEAD_DIM
PEER_HEADS = 8
PEER_KEYS = 128
PEER_TOPK = 16
PEER_HALF = 128

LANES = 128
SUBLANES = 8
SSD_CHUNK = 128
GDN_CHUNK = 128
VMEM_LIMIT = 56 * 1024 * 1024

TM_PROJ = 1024
TN_PROJ = 1024
TM_MERGE = 256
TM_ROUTE = 256
TM_EXP = 512
EXP_ROWS = 2
G_PITCH = TM_EXP + 8
TM_PLE = 512

NEG = -1e30


def _dot(a, b):
    return jnp.dot(a, b, preferred_element_type=F32)


def _dot_nt(a, b):
    return lax.dot_general(a, b, (((1,), (1,)), ((), ())), preferred_element_type=F32)


def _dot_exact_lhs(lb, m):
    hi = m.astype(BF16)
    r1 = m - hi.astype(F32)
    mid = r1.astype(BF16)
    lo = (r1 - mid.astype(F32)).astype(BF16)
    return _dot(lb, hi) + _dot(lb, mid) + _dot(lb, lo)


def _softplus(x):
    return jnp.maximum(x, 0.0) + jnp.log1p(jnp.exp(-jnp.abs(x)))


def _silu(x):
    return x * jax.nn.sigmoid(x)


def _rms(x, g):
    return x * lax.rsqrt(jnp.mean(x * x, axis=-1, keepdims=True) + EPS) * g


def _cparams(sem):
    return pltpu.CompilerParams(dimension_semantics=sem, vmem_limit_bytes=VMEM_LIMIT)


def _in_proj_kernel(x_ref, g_ref, wm_ref, ws_ref, main_ref, small_ref, n_scr):
    @pl.when(pl.program_id(1) == 0)
    def _():
        n = _rms(x_ref[...], g_ref[...]).astype(BF16)
        n_scr[...] = n
        small_ref[...] = _dot(n, ws_ref[...])

    main_ref[...] = _dot(n_scr[...], wm_ref[...])


def _in_proj(x2d, g, w_main, w_small):
    t, d = x2d.shape
    n_main = w_main.shape[1]
    tm = min(TM_PROJ, t)
    return pl.pallas_call(
        _in_proj_kernel,
        grid=(t // tm, n_main // TN_PROJ),
        in_specs=[
            pl.BlockSpec((tm, d), lambda i, j: (i, 0)),
            pl.BlockSpec((1, d), lambda i, j: (0, 0)),
            pl.BlockSpec((d, TN_PROJ), lambda i, j: (0, j)),
            pl.BlockSpec((d, LANES), lambda i, j: (0, 0)),
        ],
        out_specs=[
            pl.BlockSpec((tm, TN_PROJ), lambda i, j: (i, j)),
            pl.BlockSpec((tm, LANES), lambda i, j: (i, 0)),
        ],
        out_shape=[jax.ShapeDtypeStruct((t, n_main), F32), jax.ShapeDtypeStruct((t, LANES), F32)],
        scratch_shapes=[pltpu.VMEM((tm, d), BF16)],
        compiler_params=_cparams(("parallel", "arbitrary")),
        name="in_proj",
    )(x2d, g, w_main, w_small)


def _conv_silu_chunk(src_ref, w_ref, b_ref, c, nch, rows):
    length = src_ref.shape[1]
    base = pl.multiple_of(c * rows, rows)
    main = src_ref[0, pl.ds(base, rows), :]
    prev = src_ref[0, pl.ds(pl.multiple_of(jnp.maximum(base - SUBLANES, 0), SUBLANES), SUBLANES), :]
    nxt = src_ref[0, pl.ds(pl.multiple_of(jnp.minimum(base + rows, length - SUBLANES), SUBLANES), SUBLANES), :]
    prev = jnp.where(c > 0, prev, 0.0)
    nxt = jnp.where(c < nch - 1, nxt, 0.0)
    ext = jnp.concatenate([prev, main, nxt], axis=0)
    w = w_ref[...]
    lo = SUBLANES - CONV_K // 2
    y = ext[lo:lo + rows] * w[0:1]
    for j in range(1, CONV_K):
        y = y + ext[lo + j:lo + j + rows] * w[j:j + 1]
    if b_ref is not None:
        y = y + b_ref[...]
    return _silu(y)


def _expand_heads(t, lane0, width):
    r = t.shape[0]
    hid = lax.broadcasted_iota(I32, (r, 4 * width), 1) // width
    out = jnp.broadcast_to(t[:, lane0 + 3:lane0 + 4], (r, 4 * width))
    for j in (2, 1, 0):
        out = jnp.where(hid == j, t[:, lane0 + j:lane0 + j + 1], out)
    return out


def _ssd_kernel(z_ref, xs_ref, bm_ref, cm_ref, sm_ref, cwx_ref, cwb_ref, cwc_ref, cbx_ref, cbb_ref, cbc_ref,
                hp_ref, nw_ref, out_ref, xc_scr, bc_scr, cc_scr, y_scr, st_scr):
    ch = SSD_CHUNK
    length = z_ref.shape[1]
    nch = length // ch
    gwid = 4 * SSD_HEAD_DIM
    grp = pl.program_id(1)
    shift = lax.rem(LANES - 4 * grp, LANES)
    hp = pltpu.roll(hp_ref[...], shift, 1)
    bias_row = hp[0:1, :]
    a_row = -jnp.exp(hp[1:2, :])
    dsk_x = _expand_heads(hp[2:3, :], 0, SSD_HEAD_DIM)

    row = lax.broadcasted_iota(I32, (ch, ch), 0)
    col = lax.broadcasted_iota(I32, (ch, ch), 1)
    hid = lax.broadcasted_iota(I32, (ch, gwid), 1) // SSD_HEAD_DIM

    def conv_body(c, carry):
        base = pl.multiple_of(c * ch, ch)
        xc_scr[pl.ds(base, ch), :] = _conv_silu_chunk(xs_ref, cwx_ref, cbx_ref, c, nch, ch)
        bc_scr[pl.ds(base, ch), :] = _conv_silu_chunk(bm_ref, cwb_ref, cbb_ref, c, nch, ch)
        cc_scr[pl.ds(base, ch), :] = _conv_silu_chunk(cm_ref, cwc_ref, cbc_ref, c, nch, ch)
        return carry

    lax.fori_loop(0, nch, conv_body, 0)

    def scan_pass(forward):
        mask = (row >= col) if forward else (row <= col)
        lmat = jnp.where(mask, 1.0, 0.0).astype(BF16)
        lane0 = 0 if forward else SSD_HEADS
        tot_row = ch - 1 if forward else 0
        st_scr[...] = jnp.zeros_like(st_scr)

        def body(ci, carry):
            c = ci if forward else nch - 1 - ci
            base = pl.multiple_of(c * ch, ch)
            x = xc_scr[pl.ds(base, ch), :]
            bb = bc_scr[pl.ds(base, ch), :]
            cb = cc_scr[pl.ds(base, ch), :].astype(BF16)
            sm = pltpu.roll(sm_ref[0, pl.ds(base, ch), :], shift, 1)
            dt_all = _softplus(sm + bias_row)
            acum = _dot_exact_lhs(lmat, dt_all * a_row)
            acum_t = acum.T
            tot = acum[tot_row:tot_row + 1, :]
            ea_x = _expand_heads(jnp.exp(acum), lane0, SSD_HEAD_DIM)
            dte_x = _expand_heads(jnp.exp(tot - acum), lane0, SSD_HEAD_DIM)
            etot_x = _expand_heads(jnp.exp(tot), lane0, SSD_HEAD_DIM)
            xd = x * _expand_heads(dt_all, lane0, SSD_HEAD_DIM)
            sc = jnp.where(mask, _dot_nt(cb, bb.astype(BF16)), 0.0)
            mixes = []
            rhs = []
            for j in range(4):
                d = acum[:, lane0 + j:lane0 + j + 1] - acum_t[lane0 + j:lane0 + j + 1, :]
                mixes.append((sc * jnp.exp(jnp.where(mask, d, NEG))).astype(BF16))
                rhs.append(jnp.where(hid == j, xd, 0.0).astype(BF16))
            y = _dot(jnp.concatenate(mixes, axis=1), jnp.concatenate(rhs, axis=0))
            st = st_scr[...]
            y = y + _dot(cb, st.astype(BF16)) * ea_x
            if forward:
                y_scr[pl.ds(base, ch), :] = y
            else:
                y_scr[pl.ds(base, ch), :] = y_scr[pl.ds(base, ch), :] + y
            st_scr[...] = st * etot_x + _dot(bb.T.astype(BF16), (xd * dte_x).astype(BF16))
            return carry

        lax.fori_loop(0, nch, body, 0)

    scan_pass(True)
    scan_pass(False)

    def fin_body(c, carry):
        base = pl.multiple_of(c * ch, ch)
        y = y_scr[pl.ds(base, ch), :] + xc_scr[pl.ds(base, ch), :] * dsk_x
        y = y * _silu(z_ref[0, pl.ds(base, ch), :])
        out_ref[0, pl.ds(base, ch), :] = _rms(y, nw_ref[...])
        return carry

    lax.fori_loop(0, nch, fin_body, 0)


def _ssd_mixer(proj3, small3, conv_w, conv_b, hp, norm_w):
    b, length, _ = proj3.shape
    gw = 4 * SSD_HEAD_DIM
    xs0 = SSD_INNER // gw
    b0 = 2 * SSD_INNER // SSD_STATE
    c0 = b0 + SSD_GROUPS
    nb_x = SSD_INNER // SSD_STATE
    seq = lambda w, off: pl.BlockSpec((1, length, w), lambda i, g: (i, 0, off + g))
    par = lambda r, w, off: pl.BlockSpec((r, w), lambda i, g: (0, off + g))
    return pl.pallas_call(
        _ssd_kernel,
        grid=(b, SSD_GROUPS),
        in_specs=[
            seq(gw, 0), seq(gw, xs0), seq(SSD_STATE, b0), seq(SSD_STATE, c0),
            pl.BlockSpec((1, length, LANES), lambda i, g: (i, 0, 0)),
            par(CONV_K, gw, 0), par(CONV_K, SSD_STATE, nb_x), par(CONV_K, SSD_STATE, nb_x + SSD_GROUPS),
            par(1, gw, 0), par(1, SSD_STATE, nb_x), par(1, SSD_STATE, nb_x + SSD_GROUPS),
            pl.BlockSpec((SUBLANES, LANES), lambda i, g: (0, 0)),
            par(1, gw, 0),
        ],
        out_specs=pl.BlockSpec((1, length, gw), lambda i, g: (i, 0, g)),
        out_shape=jax.ShapeDtypeStruct((b, length, SSD_INNER), F32),
        scratch_shapes=[
            pltpu.VMEM((length, gw), F32), pltpu.VMEM((length, SSD_STATE), F32),
            pltpu.VMEM((length, SSD_STATE), F32), pltpu.VMEM((length, gw), F32),
            pltpu.VMEM((SSD_STATE, gw), F32),
        ],
        compiler_params=_cparams(("parallel", "arbitrary")),
        name="ssd_mixer",
    )(proj3, proj3, proj3, proj3, small3, conv_w, conv_w, conv_w, conv_b, conv_b, conv_b, hp, norm_w)


def _gdn_kernel(q_ref, k_ref, v_ref, z_ref, sm_ref, cwq_ref, cwk_ref, cwv_ref, hp_ref, nw_ref, out_ref,
                qn_scr, kn_scr, vn_scr, of_scr, ob_scr):
    ch = GDN_CHUNK
    length = z_ref.shape[1]
    nch = length // ch
    dk = GDN_HEAD_DIM
    head = pl.program_id(1)
    shift = lax.rem(LANES - head, LANES)
    lane_a = 2 * SSD_HEADS
    lane_beta = lane_a + 2 * GDN_HEADS
    hp = pltpu.roll(hp_ref[...], shift, 1)
    bias_row = hp[0:1, :]
    acoef_row = jnp.exp(hp[1:2, :])

    row = lax.broadcasted_iota(I32, (ch, ch), 0)
    col = lax.broadcasted_iota(I32, (ch, ch), 1)
    eye = jnp.where(row == col, 1.0, 0.0)
    n_double = (ch - 1).bit_length() - 1

    def conv_body(c, carry):
        base = pl.multiple_of(c * ch, ch)
        q = _conv_silu_chunk(q_ref, cwq_ref, None, c, nch, ch)
        k = _conv_silu_chunk(k_ref, cwk_ref, None, c, nch, ch)
        qn_scr[pl.ds(base, ch), :] = q * (lax.rsqrt(jnp.sum(q * q, axis=-1, keepdims=True) + EPS) * dk ** -0.5)
        kn_scr[pl.ds(base, ch), :] = k * lax.rsqrt(jnp.sum(k * k, axis=-1, keepdims=True) + EPS)
        vn_scr[pl.ds(base, ch), :] = _conv_silu_chunk(v_ref, cwv_ref, None, c, nch, ch)
        return carry

    lax.fori_loop(0, nch, conv_body, 0)

    def chunk(c, s, forward):
        incl = (row >= col) if forward else (row <= col)
        strict = (row > col) if forward else (row < col)
        lmat = jnp.where(incl, 1.0, 0.0).astype(BF16)
        umat = jnp.where(strict, 1.0, 0.0)
        jref = 0 if forward else ch - 1
        tot_row = ch - 1 if forward else 0
        dir_off = 0 if forward else GDN_HEADS

        base = pl.multiple_of(c * ch, ch)
        q = qn_scr[pl.ds(base, ch), :]
        k = kn_scr[pl.ds(base, ch), :]
        v = vn_scr[pl.ds(base, ch), :]
        sm = pltpu.roll(sm_ref[0, pl.ds(base, ch), :], shift, 1)
        g_all = -acoef_row * _softplus(sm + bias_row)
        beta_all = jax.nn.sigmoid(sm)
        g = g_all[:, lane_a + dir_off:lane_a + dir_off + 1]
        beta = beta_all[:, lane_beta + dir_off:lane_beta + dir_off + 1]

        dmat = _dot_exact_lhs(lmat, g * umat)
        dec = dmat[:, jref:jref + 1] + g[jref:jref + 1, :]
        tot = dec[tot_row:tot_row + 1, :]
        gam = jnp.exp(jnp.where(incl, dmat, NEG))
        edec = jnp.exp(dec)
        kb = k * beta
        kbf = k.astype(BF16)
        a = jnp.where(strict, _dot_nt(kb.astype(BF16), kbf) * gam, 0.0)
        p = eye - a
        ak = a
        for _ in range(n_double):
            akb = ak.astype(BF16)
            ak = _dot(akb, akb)
            p = p + _dot(p.astype(BF16), ak.astype(BF16))
        rhs = jnp.concatenate([v * beta, kb * edec], axis=1).astype(BF16)
        uw = _dot(p.astype(BF16), rhs)
        u = uw[:, :dk]
        w = uw[:, dk:]
        qk = (_dot_nt(q.astype(BF16), kbf) * gam).astype(BF16)
        lhs = jnp.concatenate([w, q * edec], axis=0).astype(BF16)
        ws = _dot(lhs, s.astype(BF16))
        v_new = (u - ws[:ch]).astype(BF16)
        o = ws[ch:] + _dot(qk, v_new)
        kd = k * jnp.exp(tot - dec)
        s_new = s * jnp.exp(tot) + _dot(kd.T.astype(BF16), v_new)
        return o, s_new

    def body(ci, carry):
        sf, sb = carry
        cb = nch - 1 - ci
        o_f, sf = chunk(ci, sf, True)
        o_b, sb = chunk(cb, sb, False)
        of_scr[pl.ds(pl.multiple_of(ci * ch, ch), ch), :] = o_f
        ob_scr[pl.ds(pl.multiple_of(cb * ch, ch), ch), :] = o_b
        return sf, sb

    zero = jnp.zeros((dk, dk), F32)
    lax.fori_loop(0, nch, body, (zero, zero))

    def fin_body(c, carry):
        base = pl.multiple_of(c * ch, ch)
        o = of_scr[pl.ds(base, ch), :] + ob_scr[pl.ds(base, ch), :]
        out_ref[0, pl.ds(base, ch), :] = _rms(o, nw_ref[...]) * _silu(z_ref[0, pl.ds(base, ch), :])
        return carry

    lax.fori_loop(0, nch, fin_body, 0)


def _gdn_mixer(proj3, small3, conv_w, hp, norm_w):
    b, length, _ = proj3.shape
    dk = GDN_HEAD_DIM
    q0 = (2 * SSD_INNER + 2 * SSD_GROUPS * SSD_STATE) // dk
    seq = lambda off: pl.BlockSpec((1, length, dk), lambda i, h: (i, 0, off + h))
    par = lambda off: pl.BlockSpec((CONV_K, dk), lambda i, h: (0, off + h))
    return pl.pallas_call(
        _gdn_kernel,
        grid=(b, GDN_HEADS),
        in_specs=[
            seq(q0), seq(q0 + GDN_HEADS), seq(q0 + 2 * GDN_HEADS), seq(q0 + 3 * GDN_HEADS),
            pl.BlockSpec((1, length, LANES), lambda i, h: (i, 0, 0)),
            par(0), par(GDN_HEADS), par(2 * GDN_HEADS),
            pl.BlockSpec((SUBLANES, LANES), lambda i, h: (0, 0)),
            pl.BlockSpec((1, dk), lambda i, h: (0, 0)),
        ],
        out_specs=pl.BlockSpec((1, length, dk), lambda i, h: (i, 0, h)),
        out_shape=jax.ShapeDtypeStruct((b, length, GDN_INNER), F32),
        scratch_shapes=[pltpu.VMEM((length, dk), F32) for _ in range(5)],
        compiler_params=_cparams(("parallel", "arbitrary")),
        name="gdn_mixer",
    )(proj3, proj3, proj3, proj3, small3, conv_w, conv_w, conv_w, hp, norm_w)


def _merge_kernel(x_ref, ya_ref, yb_ref, ga_ref, gb_ref, wa_ref, wb_ref, wo_ref, gf_ref, h_ref, n_ref):
    ya = _dot(ya_ref[...].astype(BF16), wa_ref[...])
    yb = _dot(yb_ref[...].astype(BF16), wb_ref[...])
    merged = jax.nn.sigmoid(ga_ref[...]) * ya + jax.nn.sigmoid(gb_ref[...]) * yb
    h = x_ref[...] + _dot(merged.astype(BF16), wo_ref[...])
    h_ref[...] = h
    n_ref[...] = _rms(h, gf_ref[...]).astype(BF16)


def _merge(x2d, ya, yb, proj2d, wa, wb, wo, gf):
    t, d = x2d.shape
    tm = min(TM_MERGE, t)
    ga0 = (proj2d.shape[1] - 2 * d) // d
    tok = lambda: pl.BlockSpec((tm, d), lambda i: (i, 0))
    wsp = lambda: pl.BlockSpec((d, d), lambda i: (0, 0))
    return pl.pallas_call(
        _merge_kernel,
        grid=(t // tm,),
        in_specs=[tok(), tok(), tok(),
                  pl.BlockSpec((tm, d), lambda i: (i, ga0)), pl.BlockSpec((tm, d), lambda i: (i, ga0 + 1)),
                  wsp(), wsp(), wsp(), pl.BlockSpec((1, d), lambda i: (0, 0))],
        out_specs=[tok(), tok()],
        out_shape=[jax.ShapeDtypeStruct((t, d), F32), jax.ShapeDtypeStruct((t, d), BF16)],
        compiler_params=_cparams(("parallel",)),
        name="merge",
    )(x2d, ya, yb, proj2d, proj2d, wa, wb, wo, gf)


def _topk_rows(s, k):
    r = s.shape[0]
    iota = lax.broadcasted_iota(I32, s.shape, 0)
    vals, idxs = [], []
    for _ in range(k):
        m = jnp.max(s, axis=0, keepdims=True)
        ix = jnp.min(jnp.where(s == m, iota, r), axis=0, keepdims=True)
        vals.append(m)
        idxs.append(ix)
        s = jnp.where(iota == ix, -jnp.inf, s)
    return vals, idxs


def _route_kernel(n_ref, wq_ref, keys_ref, ii_ref, jj_ref, gate_ref):
    tm = n_ref.shape[0]
    kk = PEER_TOPK
    qry = _dot(n_ref[...], wq_ref[...])
    ii_rows, jj_rows, gate_rows = [], [], []
    for h in range(PEER_HEADS):
        tops = []
        for z in range(2):
            hz = 2 * h + z
            qh = qry[:, hz * PEER_HALF:(hz + 1) * PEER_HALF].astype(BF16)
            st = _dot_nt(keys_ref[hz], qh)
            vals, idxs = _topk_rows(st, kk)
            tops.append((jnp.concatenate(vals, axis=0), jnp.concatenate(idxs, axis=0)))
        (s0, i0), (s1, i1) = tops
        cand_s = jnp.concatenate([s0[i:i + 1, :] + s1 for i in range(kk)], axis=0)
        cand_i = jnp.concatenate([jnp.broadcast_to(i0[i:i + 1, :], (kk, tm)) for i in range(kk)], axis=0)
        cand_j = jnp.concatenate([i1] * kk, axis=0)
        best, pos = _topk_rows(cand_s, kk)
        iota = lax.broadcasted_iota(I32, cand_s.shape, 0)
        for r in range(kk):
            hit = iota == pos[r]
            ii_rows.append(jnp.max(jnp.where(hit, cand_i, -1), axis=0, keepdims=True))
            jj_rows.append(jnp.max(jnp.where(hit, cand_j, -1), axis=0, keepdims=True))
        e = [jnp.exp(b - best[0]) for b in best]
        den = e[0]
        for t in e[1:]:
            den = den + t
        gate_rows.extend([t / den for t in e])
    ii_ref[...] = jnp.concatenate(ii_rows, axis=0).T
    jj_ref[...] = jnp.concatenate(jj_rows, axis=0).T
    gate_ref[...] = jnp.concatenate(gate_rows, axis=0).T


def _route(n2, wq, keys):
    t, d = n2.shape
    tm = min(TM_ROUTE, t)
    nq = wq.shape[1]
    hk = PEER_HEADS * PEER_TOPK
    out = lambda: pl.BlockSpec((tm, hk), lambda i: (i, 0))
    return pl.pallas_call(
        _route_kernel,
        grid=(t // tm,),
        in_specs=[pl.BlockSpec((tm, d), lambda i: (i, 0)),
                  pl.BlockSpec((d, nq), lambda i: (0, 0)),
                  pl.BlockSpec(keys.shape, lambda i: (0, 0, 0))],
        out_specs=[out(), out(), out()],
        out_shape=[jax.ShapeDtypeStruct((t, hk), I32), jax.ShapeDtypeStruct((t, hk), I32),
                   jax.ShapeDtypeStruct((t, hk), F32)],
        compiler_params=_cparams(("parallel",)),
        name="peer_route",
    )(n2, wq, keys)


def _expert_kernel(h_ref, n_ref, ii_ref, jj_ref, gate_ref, u_ref, v_ref, out_ref, g_scr):
    tm = n_ref.shape[0]
    nk = PEER_KEYS
    step = pl.program_id(1)

    @pl.when(step == 0)
    def _():
        out_ref[...] = h_ref[...]
        sub = lax.broadcasted_iota(I32, (nk, LANES), 0)

        def tok_body(t, carry):
            ri = ii_ref[pl.ds(t, 1), :]
            rj = jj_ref[pl.ds(t, 1), :]
            rg = gate_ref[pl.ds(t, 1), :]
            ghi = rg.astype(BF16).astype(F32)
            hit_i = sub == ri
            at = jnp.concatenate([jnp.where(hit_i, ghi, 0.0).astype(BF16),
                                  jnp.where(hit_i, rg - ghi, 0.0).astype(BF16)], axis=1)
            bt1 = jnp.where(sub == rj, 1.0, 0.0).astype(BF16)
            bt = jnp.concatenate([bt1, bt1], axis=1)
            g_scr[pl.ds(t, nk, stride=G_PITCH), :] = _dot_nt(at, bt)
            return carry

        lax.fori_loop(0, tm, tok_body, 0)

    hid = _dot_nt(n_ref[...], u_ref[...])
    act = 0.5 * hid * (1.0 + lax.erf(hid * (0.5 ** 0.5)))
    gates = [g_scr[pl.ds(pl.multiple_of((step * EXP_ROWS + r) * G_PITCH, SUBLANES), tm), :]
             for r in range(EXP_ROWS)]
    wts = (act * jnp.concatenate(gates, axis=1)).astype(BF16)
    out_ref[...] = out_ref[...] + _dot(wts, v_ref[...])


def _experts(h1, n2, ii, jj, gate, eu, ev):
    t, d = h1.shape
    tm = min(TM_EXP, t)
    assert tm == TM_EXP, "gate slab pitch is derived from TM_EXP"
    hk = ii.shape[1]
    er = EXP_ROWS * PEER_KEYS
    nsteps = eu.shape[0] // er
    return pl.pallas_call(
        _expert_kernel,
        grid=(t // tm, nsteps),
        in_specs=[pl.BlockSpec((tm, d), lambda i, e: (i, 0)),
                  pl.BlockSpec((tm, d), lambda i, e: (i, 0)),
                  pl.BlockSpec((tm, hk), lambda i, e: (i, 0)),
                  pl.BlockSpec((tm, hk), lambda i, e: (i, 0)),
                  pl.BlockSpec((tm, hk), lambda i, e: (i, 0)),
                  pl.BlockSpec((er, d), lambda i, e: (e, 0)),
                  pl.BlockSpec((er, d), lambda i, e: (e, 0))],
        out_specs=pl.BlockSpec((tm, d), lambda i, e: (i, 0)),
        out_shape=jax.ShapeDtypeStruct((t, d), F32),
        scratch_shapes=[pltpu.VMEM((PEER_KEYS * G_PITCH, LANES), F32)],
        compiler_params=_cparams(("parallel", "arbitrary")),
        name="peer_experts",
    )(h1, n2, ii, jj, gate, eu, ev)


def _ple_kernel(h_ref, p_ref, gp_ref, wg_ref, wp_ref, gf_ref, y_ref):
    h = h_ref[...]
    gate = jax.nn.sigmoid(_dot(_rms(h, gp_ref[...]).astype(BF16), wg_ref[...]))
    h = h + gate * _dot(p_ref[...].astype(BF16), wp_ref[...])
    y_ref[...] = _rms(h, gf_ref[...])


def _ple(h2, p2d, gp, wg, wp, gfin):
    t, d = h2.shape
    pd = p2d.shape[1]
    tm = min(TM_PLE, t)
    vec = lambda: pl.BlockSpec((1, d), lambda i: (0, 0))
    return pl.pallas_call(
        _ple_kernel,
        grid=(t // tm,),
        in_specs=[pl.BlockSpec((tm, d), lambda i: (i, 0)), pl.BlockSpec((tm, pd), lambda i: (i, 0)),
                  vec(), pl.BlockSpec((d, d), lambda i: (0, 0)), pl.BlockSpec((pd, d), lambda i: (0, 0)), vec()],
        out_specs=pl.BlockSpec((tm, d), lambda i: (i, 0)),
        out_shape=jax.ShapeDtypeStruct((t, d), F32),
        compiler_params=_cparams(("parallel",)),
        name="ple_final",
    )(h2, p2d, gp, wg, wp, gfin)


def _pad_lanes(v):
    return jnp.pad(v, (0, LANES - v.shape[0]))


def _prepare(norm_mix, w_in, ssd_conv_w, ssd_conv_b, ssd_a_log, ssd_dt_bias, ssd_d, ssd_norm, w_ssd_out,
             gdn_conv_w, gdn_a_log, gdn_dt_bias, gdn_norm, w_gdn_out, w_out, norm_ffn, peer_query, peer_keys,
             expert_u, expert_v, norm_ple, w_ple_gate, w_ple_proj, norm_final):
    w = w_in[0]
    d = w.shape[0]
    xbc = SSD_INNER + 2 * SSD_GROUPS * SSD_STATE
    c_dt = SSD_INNER + xbc
    c_qkv = c_dt + 2 * SSD_HEADS
    c_a = c_qkv + 4 * GDN_INNER
    c_ga = c_a + 4 * GDN_HEADS
    w_main = jnp.concatenate([w[:, :c_dt], w[:, c_qkv:c_a], w[:, c_ga:]], axis=1).astype(BF16)
    w_small = jnp.concatenate([w[:, c_dt:c_qkv], w[:, c_a:c_ga]], axis=1)
    w_small = jnp.pad(w_small, ((0, 0), (0, LANES - w_small.shape[1]))).astype(BF16)
    hp = jnp.stack([
        _pad_lanes(jnp.concatenate([ssd_dt_bias[0].reshape(-1), gdn_dt_bias[0].reshape(-1)])),
        _pad_lanes(jnp.concatenate([ssd_a_log[0].reshape(-1), gdn_a_log[0].reshape(-1)])),
        _pad_lanes(ssd_d[0]),
    ])
    hp = jnp.pad(hp, ((0, SUBLANES - hp.shape[0]), (0, 0)))
    row = lambda v: v.reshape(1, -1)
    return dict(
        g_mix=row(norm_mix[0]), w_main=w_main, w_small=w_small, hp=hp,
        ssd_conv_w=ssd_conv_w[0], ssd_conv_b=row(ssd_conv_b[0]), ssd_norm=row(ssd_norm[0]),
        gdn_conv_w=gdn_conv_w[0], gdn_norm=row(gdn_norm[0]),
        wa=w_ssd_out[0].astype(BF16), wb=w_gdn_out[0].astype(BF16), wo=w_out[0].astype(BF16),
        g_ffn=row(norm_ffn[0]), wq=peer_query[0].astype(BF16),
        keys=peer_keys[0].reshape(2 * PEER_HEADS, PEER_KEYS, PEER_HALF).astype(BF16),
        eu=expert_u[0].astype(BF16), ev=expert_v[0].astype(BF16),
        g_ple=row(norm_ple[0]), wg=w_ple_gate[0].astype(BF16), wp=w_ple_proj[0].astype(BF16),
        g_fin=row(norm_final), d=d,
    )


def _trunk(x, ple, p):
    b, length, d = x.shape
    t = b * length
    x2d = x.reshape(t, d)
    proj, small = _in_proj(x2d, p["g_mix"], p["w_main"], p["w_small"])
    proj3 = proj.reshape(b, length, -1)
    small3 = small.reshape(b, length, LANES)
    ya = _ssd_mixer(proj3, small3, p["ssd_conv_w"], p["ssd_conv_b"], p["hp"], p["ssd_norm"])
    yb = _gdn_mixer(proj3, small3, p["gdn_conv_w"], p["hp"], p["gdn_norm"])
    h1, n2 = _merge(x2d, ya.reshape(t, -1), yb.reshape(t, -1), proj, p["wa"], p["wb"], p["wo"], p["g_ffn"])
    ii, jj, gate = _route(n2, p["wq"], p["keys"])
    h2 = _experts(h1, n2, ii, jj, gate, p["eu"], p["ev"])
    y = _ple(h2, ple.reshape(t, -1), p["g_ple"], p["wg"], p["wp"], p["g_fin"])
    return y.reshape(b, length, d)


def kernel(x_prompt, x_sample, p_prompt, p_sample, norm_mix, w_in, ssd_conv_w, ssd_conv_b, ssd_a_log, ssd_dt_bias, ssd_d, ssd_norm, w_ssd_out, gdn_conv_w, gdn_a_log, gdn_dt_bias, gdn_norm, w_gdn_out, w_out, norm_ffn, peer_query, peer_keys, expert_u, expert_v, norm_ple, w_ple_gate, w_ple_proj, norm_final):
    assert w_in.shape[0] == 1, "single-layer trunk"
    p = _prepare(norm_mix, w_in, ssd_conv_w, ssd_conv_b, ssd_a_log, ssd_dt_bias, ssd_d, ssd_norm, w_ssd_out,
                 gdn_conv_w, gdn_a_log, gdn_dt_bias, gdn_norm, w_gdn_out, w_out, norm_ffn, peer_query, peer_keys,
                 expert_u, expert_v, norm_ple, w_ple_gate, w_ple_proj, norm_final)
    return (_trunk(x_prompt, p_prompt[0], p), _trunk(x_sample, p_sample[0], p))
```

```python
import jax
import jax.numpy as jnp
from jax import lax
from jax.experimental import pallas as pl
from jax.experimental.pallas import tpu as pltpu

F32 = jnp.float32
BF16 = jnp.bfloat16
I32 = jnp.int32

EPS = 1e-6
CONV_K = 4
SSD_HEADS = 16
SSD_HEAD_DIM = 64
SSD_GROUPS = 4
SSD_STATE = 128
SSD_INNER = SSD_HEADS * SSD_HEAD_DIM
GDN_HEADS = 8
GDN_HEAD_DIM = 128
GDN_INNER = GDN_HEADS * GDN_HEAD_DIM
PEER_HEADS = 8
PEER_KEYS = 128
PEER_TOPK = 16
PEER_HALF = 128

LANES = 128
SUBLANES = 8
SSD_CHUNK = 128
GDN_CHUNK = 128
GDN_PREP_CHUNKS = 4
VMEM_LIMIT = 56 * 1024 * 1024

TM_PROJ = 1024
TN_PROJ = 1024
TM_MERGE = 256
TM_ROUTE = 256
TM_EXP = 512
EXP_ROWS = 4
G_PITCH = TM_EXP + 8
TOKEN_UNROLL = 8
TM_PLE = 512

NEG = -1e30


def _dot(a, b):
    return jnp.dot(a, b, preferred_element_type=F32)


def _dot_nt(a, b):
    return lax.dot_general(a, b, (((1,), (1,)), ((), ())), preferred_element_type=F32)


def _dot_exact_lhs(lb, m):
    n = m.shape[1]
    hi = m.astype(BF16)
    r1 = m - hi.astype(F32)
    mid = r1.astype(BF16)
    lo = (r1 - mid.astype(F32)).astype(BF16)
    r = _dot(lb, jnp.concatenate([hi, mid, lo], axis=1))
    return r[:, :n] + r[:, n:2 * n] + r[:, 2 * n:]


def _softplus(x):
    return jnp.maximum(x, 0.0) + jnp.log1p(jnp.exp(-jnp.abs(x)))


def _silu(x):
    return x * jax.nn.sigmoid(x)


def _rms(x, g):
    return x * lax.rsqrt(jnp.mean(x * x, axis=-1, keepdims=True) + EPS) * g


def _cparams(sem):
    return pltpu.CompilerParams(dimension_semantics=sem, vmem_limit_bytes=VMEM_LIMIT)


def _in_proj_kernel(x_ref, g_ref, wm_ref, ws_ref, main_ref, small_ref, n_scr):
    @pl.when(pl.program_id(1) == 0)
    def _():
        n = _rms(x_ref[...], g_ref[...]).astype(BF16)
        n_scr[...] = n
        small_ref[...] = _dot(n, ws_ref[...])

    main_ref[...] = _dot(n_scr[...], wm_ref[...])


def _in_proj(x2d, g, w_main, w_small):
    t, d = x2d.shape
    n_main = w_main.shape[1]
    tm = min(TM_PROJ, t)
    return pl.pallas_call(
        _in_proj_kernel,
        grid=(t // tm, n_main // TN_PROJ),
        in_specs=[
            pl.BlockSpec((tm, d), lambda i, j: (i, 0)),
            pl.BlockSpec((1, d), lambda i, j: (0, 0)),
            pl.BlockSpec((d, TN_PROJ), lambda i, j: (0, j)),
            pl.BlockSpec((d, LANES), lambda i, j: (0, 0)),
        ],
        out_specs=[
            pl.BlockSpec((tm, TN_PROJ), lambda i, j: (i, j)),
            pl.BlockSpec((tm, LANES), lambda i, j: (i, 0)),
        ],
        out_shape=[jax.ShapeDtypeStruct((t, n_main), F32), jax.ShapeDtypeStruct((t, LANES), F32)],
        scratch_shapes=[pltpu.VMEM((tm, d), BF16)],
        compiler_params=_cparams(("parallel", "arbitrary")),
        name="in_proj",
    )(x2d, g, w_main, w_small)


def _conv_silu_chunk(src_ref, w_ref, b_ref, c, nch, rows):
    length = src_ref.shape[1]
    base = pl.multiple_of(c * rows, rows)
    main = src_ref[0, pl.ds(base, rows), :]
    prev = src_ref[0, pl.ds(pl.multiple_of(jnp.maximum(base - SUBLANES, 0), SUBLANES), SUBLANES), :]
    nxt = src_ref[0, pl.ds(pl.multiple_of(jnp.minimum(base + rows, length - SUBLANES), SUBLANES), SUBLANES), :]
    prev = jnp.where(c > 0, prev, 0.0)
    nxt = jnp.where(c < nch - 1, nxt, 0.0)
    ext = jnp.concatenate([prev, main, nxt], axis=0)
    w = w_ref[...]
    lo = SUBLANES - CONV_K // 2
    y = ext[lo:lo + rows] * w[0:1]
    for j in range(1, CONV_K):
        y = y + ext[lo + j:lo + j + rows] * w[j:j + 1]
    if b_ref is not None:
        y = y + b_ref[...]
    return _silu(y)


def _expand_heads(t, lane0, width):
    r = t.shape[0]
    hid = lax.broadcasted_iota(I32, (r, 4 * width), 1) // width
    out = jnp.broadcast_to(t[:, lane0 + 3:lane0 + 4], (r, 4 * width))
    for j in (2, 1, 0):
        out = jnp.where(hid == j, t[:, lane0 + j:lane0 + j + 1], out)
    return out


def _ssd_kernel(z_ref, xs_ref, bm_ref, cm_ref, sm_ref, cwx_ref, cwb_ref, cwc_ref, cbx_ref, cbb_ref, cbc_ref,
                hp_ref, nw_ref, out_ref, xc_scr, bc_scr, cc_scr, yf_scr, yb_scr, stf_scr, stb_scr):
    ch = SSD_CHUNK
    length = z_ref.shape[1]
    nch = length // ch
    gwid = 4 * SSD_HEAD_DIM
    grp = pl.program_id(1)
    shift = lax.rem(LANES - 4 * grp, LANES)
    hp = pltpu.roll(hp_ref[...], shift, 1)
    bias_row = hp[0:1, :]
    a_row = -jnp.exp(hp[1:2, :])
    dsk_x = _expand_heads(hp[2:3, :], 0, SSD_HEAD_DIM)

    row = lax.broadcasted_iota(I32, (ch, ch), 0)
    col = lax.broadcasted_iota(I32, (ch, ch), 1)
    hid = lax.broadcasted_iota(I32, (ch, gwid), 1) // SSD_HEAD_DIM

    def conv_body(c, carry):
        base = pl.multiple_of(c * ch, ch)
        xc_scr[pl.ds(base, ch), :] = _conv_silu_chunk(xs_ref, cwx_ref, cbx_ref, c, nch, ch)
        bc_scr[pl.ds(base, ch), :] = _conv_silu_chunk(bm_ref, cwb_ref, cbb_ref, c, nch, ch)
        cc_scr[pl.ds(base, ch), :] = _conv_silu_chunk(cm_ref, cwc_ref, cbc_ref, c, nch, ch)
        return carry

    lax.fori_loop(0, nch, conv_body, 0)

    masks = ((row >= col), (row <= col))
    lmats = tuple(jnp.where(m, 1.0, 0.0).astype(BF16) for m in masks)
    lanes0 = (0, SSD_HEADS)
    tot_rows = (ch - 1, 0)
    st_refs = (stf_scr, stb_scr)
    y_refs = (yf_scr, yb_scr)
    dirs = (0, 1)

    stf_scr[...] = jnp.zeros_like(stf_scr)
    stb_scr[...] = jnp.zeros_like(stb_scr)

    def scan_body(ci, carry):
        bases = (pl.multiple_of(ci * ch, ch), pl.multiple_of((nch - 1 - ci) * ch, ch))
        x = [xc_scr[pl.ds(b, ch), :] for b in bases]
        bb = [bc_scr[pl.ds(b, ch), :] for b in bases]
        cb = [cc_scr[pl.ds(b, ch), :].astype(BF16) for b in bases]
        dt_all = [_softplus(pltpu.roll(sm_ref[0, pl.ds(b, ch), :], shift, 1) + bias_row) for b in bases]
        acum = [_dot_exact_lhs(lmats[d], dt_all[d] * a_row) for d in dirs]
        sc = [jnp.where(masks[d], _dot_nt(cb[d], bb[d].astype(BF16)), 0.0) for d in dirs]
        st = [st_refs[d][...] for d in dirs]
        y_off = [_dot(cb[d], st[d].astype(BF16)) for d in dirs]
        bt = [bb[d].T.astype(BF16) for d in dirs]
        acum_t = [a.T for a in acum]
        tot = [acum[d][tot_rows[d]:tot_rows[d] + 1, :] for d in dirs]
        xd = [x[d] * _expand_heads(dt_all[d], lanes0[d], SSD_HEAD_DIM) for d in dirs]
        lhs, rhs = [], []
        for d in dirs:
            mixes, parts = [], []
            for j in range(4):
                ln = lanes0[d] + j
                dd = acum[d][:, ln:ln + 1] - acum_t[d][ln:ln + 1, :]
                mixes.append((sc[d] * jnp.exp(jnp.where(masks[d], dd, NEG))).astype(BF16))
                parts.append(jnp.where(hid == j, xd[d], 0.0).astype(BF16))
            lhs.append(jnp.concatenate(mixes, axis=1))
            rhs.append(jnp.concatenate(parts, axis=0))
        y = [_dot(lhs[d], rhs[d]) for d in dirs]
        dte_x = [_expand_heads(jnp.exp(tot[d] - acum[d]), lanes0[d], SSD_HEAD_DIM) for d in dirs]
        upd = [_dot(bt[d], (xd[d] * dte_x[d]).astype(BF16)) for d in dirs]
        for d in dirs:
            ea_x = _expand_heads(jnp.exp(acum[d]), lanes0[d], SSD_HEAD_DIM)
            etot_x = _expand_heads(jnp.exp(tot[d]), lanes0[d], SSD_HEAD_DIM)
            y_refs[d][pl.ds(bases[d], ch), :] = y[d] + y_off[d] * ea_x
            st_refs[d][...] = st[d] * etot_x + upd[d]
        return carry

    lax.fori_loop(0, nch, scan_body, 0)

    def fin_body(c, carry):
        base = pl.multiple_of(c * ch, ch)
        y = yf_scr[pl.ds(base, ch), :] + yb_scr[pl.ds(base, ch), :] + xc_scr[pl.ds(base, ch), :] * dsk_x
        y = y * _silu(z_ref[0, pl.ds(base, ch), :])
        out_ref[0, pl.ds(base, ch), :] = _rms(y, nw_ref[...])
        return carry

    lax.fori_loop(0, nch, fin_body, 0)


def _ssd_mixer(proj3, small3, conv_w, conv_b, hp, norm_w):
    b, length, _ = proj3.shape
    gw = 4 * SSD_HEAD_DIM
    xs0 = SSD_INNER // gw
    b0 = 2 * SSD_INNER // SSD_STATE
    c0 = b0 + SSD_GROUPS
    nb_x = SSD_INNER // SSD_STATE
    seq = lambda w, off: pl.BlockSpec((1, length, w), lambda i, g: (i, 0, off + g))
    par = lambda r, w, off: pl.BlockSpec((r, w), lambda i, g: (0, off + g))
    return pl.pallas_call(
        _ssd_kernel,
        grid=(b, SSD_GROUPS),
        in_specs=[
            seq(gw, 0), seq(gw, xs0), seq(SSD_STATE, b0), seq(SSD_STATE, c0),
            pl.BlockSpec((1, length, LANES), lambda i, g: (i, 0, 0)),
            par(CONV_K, gw, 0), par(CONV_K, SSD_STATE, nb_x), par(CONV_K, SSD_STATE, nb_x + SSD_GROUPS),
            par(1, gw, 0), par(1, SSD_STATE, nb_x), par(1, SSD_STATE, nb_x + SSD_GROUPS),
            pl.BlockSpec((SUBLANES, LANES), lambda i, g: (0, 0)),
            par(1, gw, 0),
        ],
        out_specs=pl.BlockSpec((1, length, gw), lambda i, g: (i, 0, g)),
        out_shape=jax.ShapeDtypeStruct((b, length, SSD_INNER), F32),
        scratch_shapes=[
            pltpu.VMEM((length, gw), F32), pltpu.VMEM((length, SSD_STATE), F32),
            pltpu.VMEM((length, SSD_STATE), F32), pltpu.VMEM((length, gw), F32), pltpu.VMEM((length, gw), F32),
            pltpu.VMEM((SSD_STATE, gw), F32), pltpu.VMEM((SSD_STATE, gw), F32),
        ],
        compiler_params=_cparams(("parallel", "arbitrary")),
        name="ssd_mixer",
    )(proj3, proj3, proj3, proj3, small3, conv_w, conv_w, conv_w, conv_b, conv_b, conv_b, hp, norm_w)


def _gdn_kernel(q_ref, k_ref, v_ref, z_ref, sm_ref, cwq_ref, cwk_ref, cwv_ref, hp_ref, nw_ref, out_ref,
                qn_scr, kn_scr, vn_scr, u_scr, wq_scr, qk_scr, kd_scr, et_scr, o_scr):
    ch = GDN_CHUNK
    length = z_ref.shape[1]
    nch = length // ch
    dk = GDN_HEAD_DIM
    head = pl.program_id(1)
    shift = lax.rem(LANES - head, LANES)
    lane_a = 2 * SSD_HEADS
    lane_beta = lane_a + 2 * GDN_HEADS
    hp = pltpu.roll(hp_ref[...], shift, 1)
    bias_row = hp[0:1, :]
    acoef_row = jnp.exp(hp[1:2, :])

    row = lax.broadcasted_iota(I32, (ch, ch), 0)
    col = lax.broadcasted_iota(I32, (ch, ch), 1)
    eye = jnp.where(row == col, 1.0, 0.0)
    n_double = (ch - 1).bit_length() - 1

    def conv_body(c, carry):
        base = pl.multiple_of(c * ch, ch)
        q = _conv_silu_chunk(q_ref, cwq_ref, None, c, nch, ch)
        k = _conv_silu_chunk(k_ref, cwk_ref, None, c, nch, ch)
        qn_scr[pl.ds(base, ch), :] = q * (lax.rsqrt(jnp.sum(q * q, axis=-1, keepdims=True) + EPS) * dk ** -0.5)
        kn_scr[pl.ds(base, ch), :] = k * lax.rsqrt(jnp.sum(k * k, axis=-1, keepdims=True) + EPS)
        vn_scr[pl.ds(base, ch), :] = _conv_silu_chunk(v_ref, cwv_ref, None, c, nch, ch)
        return carry

    lax.fori_loop(0, nch, conv_body, 0)

    incl = ((row >= col), (row <= col))
    strict = ((row > col), (row < col))
    lmat = tuple(jnp.where(m, 1.0, 0.0).astype(BF16) for m in incl)
    umat = tuple(jnp.where(m, 1.0, 0.0) for m in strict)
    jref = (0, ch - 1)
    tot_row = (ch - 1, 0)
    group = min(GDN_PREP_CHUNKS, nch)

    def prep_body(gi, carry):
        chains = []
        for ci in range(group):
            c = gi * group + ci
            base = pl.multiple_of(c * ch, ch)
            q = qn_scr[pl.ds(base, ch), :]
            k = kn_scr[pl.ds(base, ch), :]
            v = vn_scr[pl.ds(base, ch), :]
            sm = pltpu.roll(sm_ref[0, pl.ds(base, ch), :], shift, 1)
            g_all = -acoef_row * _softplus(sm + bias_row)
            beta_all = jax.nn.sigmoid(sm)
            kbf = k.astype(BF16)
            for di in (0, 1):
                ln = lane_a + di * GDN_HEADS
                lb = lane_beta + di * GDN_HEADS
                chains.append(dict(c=c, di=di, q=q, k=k, v=v, kbf=kbf, g=g_all[:, ln:ln + 1],
                                   beta=beta_all[:, lb:lb + 1]))
        for t in chains:
            t["dmat"] = _dot_exact_lhs(lmat[t["di"]], t["g"] * umat[t["di"]])
        for t in chains:
            t["kb"] = t["k"] * t["beta"]
            t["kq"] = _dot_nt(jnp.concatenate([t["kb"], t["q"]], axis=0).astype(BF16), t["kbf"])
        for t in chains:
            di = t["di"]
            dec = t["dmat"][:, jref[di]:jref[di] + 1] + t["g"][jref[di]:jref[di] + 1, :]
            t["dec"] = dec
            t["tot"] = dec[tot_row[di]:tot_row[di] + 1, :]
            gam = jnp.exp(jnp.where(incl[di], t["dmat"], NEG))
            t["qk"] = (t["kq"][ch:] * gam).astype(BF16)
            a = jnp.where(strict[di], t["kq"][:ch] * gam, 0.0)
            t["p"] = eye - a
            t["ab"] = a.astype(BF16)
        for t in chains:
            t["x"] = _dot(t["ab"], t["ab"])
        for i in range(n_double):
            last = i == n_double - 1
            for t in chains:
                xb = t["x"].astype(BF16)
                pb = t["p"].astype(BF16)
                t["r"] = _dot(xb, pb) if last else _dot(xb, jnp.concatenate([pb, xb], axis=1))
            for t in chains:
                t["p"] = t["p"] + t["r"][:, :ch]
                if not last:
                    t["x"] = t["r"][:, ch:]
        for t in chains:
            t["edec"] = jnp.exp(t["dec"])
            rhs = jnp.concatenate([t["v"] * t["beta"], t["kb"] * t["edec"]], axis=1).astype(BF16)
            t["uw"] = _dot(t["p"].astype(BF16), rhs)
        for t in chains:
            c, di = t["c"], t["di"]
            base = pl.multiple_of(c * ch, ch)
            base2 = pl.multiple_of(c * 2 * ch, 2 * ch)
            u_scr[di, pl.ds(base, ch), :] = t["uw"][:, :dk]
            wq_scr[di, pl.ds(base2, 2 * ch), :] = jnp.concatenate(
                [t["uw"][:, dk:], t["q"] * t["edec"]], axis=0).astype(BF16)
            qk_scr[di, pl.ds(base, ch), :] = t["qk"]
            kd_scr[di, pl.ds(base, dk), :] = (t["k"] * jnp.exp(t["tot"] - t["dec"])).T.astype(BF16)
            et_scr[di, pl.ds(pl.multiple_of(c * SUBLANES, SUBLANES), SUBLANES), :] = jnp.broadcast_to(
                jnp.exp(t["tot"]), (SUBLANES, LANES))
        return carry

    lax.fori_loop(0, nch // group, prep_body, 0)

    def rec_body(ci, carry):
        cs = (ci, nch - 1 - ci)
        bases = [pl.multiple_of(c * ch, ch) for c in cs]
        ws = [_dot(wq_scr[di, pl.ds(pl.multiple_of(cs[di] * 2 * ch, 2 * ch), 2 * ch), :], carry[di].astype(BF16))
              for di in (0, 1)]
        v_new = [(u_scr[di, pl.ds(bases[di], ch), :] - ws[di][:ch]).astype(BF16) for di in (0, 1)]
        upd = [_dot(kd_scr[di, pl.ds(bases[di], dk), :], v_new[di]) for di in (0, 1)]
        intra = [_dot(qk_scr[di, pl.ds(bases[di], ch), :], v_new[di]) for di in (0, 1)]
        s_new = []
        for di in (0, 1):
            et = et_scr[di, pl.ds(pl.multiple_of(cs[di] * SUBLANES, SUBLANES), 1), :]
            s_new.append(carry[di] * et + upd[di])
            o_scr[di, pl.ds(bases[di], ch), :] = ws[di][ch:] + intra[di]
        return tuple(s_new)

    zero = jnp.zeros((dk, dk), F32)
    lax.fori_loop(0, nch, rec_body, (zero, zero))

    def fin_body(c, carry):
        base = pl.multiple_of(c * ch, ch)
        o = o_scr[0, pl.ds(base, ch), :] + o_scr[1, pl.ds(base, ch), :]
        out_ref[0, pl.ds(base, ch), :] = _rms(o, nw_ref[...]) * _silu(z_ref[0, pl.ds(base, ch), :])
        return carry

    lax.fori_loop(0, nch, fin_body, 0)


def _gdn_mixer(proj3, small3, conv_w, hp, norm_w):
    b, length, _ = proj3.shape
    dk = GDN_HEAD_DIM
    nch = length // GDN_CHUNK
    q0 = (2 * SSD_INNER + 2 * SSD_GROUPS * SSD_STATE) // dk
    seq = lambda off: pl.BlockSpec((1, length, dk), lambda i, h: (i, 0, off + h))
    par = lambda off: pl.BlockSpec((CONV_K, dk), lambda i, h: (0, off + h))
    return pl.pallas_call(
        _gdn_kernel,
        grid=(b, GDN_HEADS),
        in_specs=[
            seq(q0), seq(q0 + GDN_HEADS), seq(q0 + 2 * GDN_HEADS), seq(q0 + 3 * GDN_HEADS),
            pl.BlockSpec((1, length, LANES), lambda i, h: (i, 0, 0)),
            par(0), par(GDN_HEADS), par(2 * GDN_HEADS),
            pl.BlockSpec((SUBLANES, LANES), lambda i, h: (0, 0)),
            pl.BlockSpec((1, dk), lambda i, h: (0, 0)),
        ],
        out_specs=pl.BlockSpec((1, length, dk), lambda i, h: (i, 0, h)),
        out_shape=jax.ShapeDtypeStruct((b, length, GDN_INNER), F32),
        scratch_shapes=[
            pltpu.VMEM((length, dk), F32), pltpu.VMEM((length, dk), F32), pltpu.VMEM((length, dk), F32),
            pltpu.VMEM((2, length, dk), F32),
            pltpu.VMEM((2, 2 * length, dk), BF16),
            pltpu.VMEM((2, length, GDN_CHUNK), BF16),
            pltpu.VMEM((2, nch * dk, GDN_CHUNK), BF16),
            pltpu.VMEM((2, nch * SUBLANES, LANES), F32),
            pltpu.VMEM((2, length, dk), F32),
        ],
        compiler_params=_cparams(("parallel", "arbitrary")),
        name="gdn_mixer",
    )(proj3, proj3, proj3, proj3, small3, conv_w, conv_w, conv_w, hp, norm_w)


def _merge_kernel(x_ref, ya_ref, yb_ref, ga_ref, gb_ref, wa_ref, wb_ref, wo_ref, gf_ref, h_ref, n_ref):
    ya = _dot(ya_ref[...].astype(BF16), wa_ref[...])
    yb = _dot(yb_ref[...].astype(BF16), wb_ref[...])
    merged = jax.nn.sigmoid(ga_ref[...]) * ya + jax.nn.sigmoid(gb_ref[...]) * yb
    h = x_ref[...] + _dot(merged.astype(BF16), wo_ref[...])
    h_ref[...] = h
    n_ref[...] = _rms(h, gf_ref[...]).astype(BF16)


def _merge(x2d, ya, yb, proj2d, wa, wb, wo, gf):
    t, d = x2d.shape
    tm = min(TM_MERGE, t)
    ga0 = (proj2d.shape[1] - 2 * d) // d
    tok = lambda: pl.BlockSpec((tm, d), lambda i: (i, 0))
    wsp = lambda: pl.BlockSpec((d, d), lambda i: (0, 0))
    return pl.pallas_call(
        _merge_kernel,
        grid=(t // tm,),
        in_specs=[tok(), tok(), tok(),
                  pl.BlockSpec((tm, d), lambda i: (i, ga0)), pl.BlockSpec((tm, d), lambda i: (i, ga0 + 1)),
                  wsp(), wsp(), wsp(), pl.BlockSpec((1, d), lambda i: (0, 0))],
        out_specs=[tok(), tok()],
        out_shape=[jax.ShapeDtypeStruct((t, d), F32), jax.ShapeDtypeStruct((t, d), BF16)],
        compiler_params=_cparams(("parallel",)),
        name="merge",
    )(x2d, ya, yb, proj2d, proj2d, wa, wb, wo, gf)


def _topk_rows(s, k, payload=None):
    r = s.shape[0]
    iota = lax.broadcasted_iota(I32, s.shape, 0).astype(F32)
    vals, outs = [], []
    for _ in range(k):
        m = jnp.max(s, axis=0, keepdims=True)
        cand = jnp.where(s == m, iota, float(r))
        ix = jnp.min(cand, axis=0, keepdims=True)
        hit = cand == ix
        vals.append(m)
        outs.append(ix if payload is None else jnp.max(jnp.where(hit, payload, -1.0), axis=0, keepdims=True))
        s = jnp.where(hit, -jnp.inf, s)
    return vals, outs


def _route_kernel(n_ref, wq_ref, keys_ref, e_ref, gate_ref, q_scr, e_scr, g_scr):
    kk = PEER_TOPK
    half = kk // 2
    q_scr[...] = _dot(n_ref[...], wq_ref[...])

    def head_body(h, carry):
        tops = []
        for z in range(2):
            off = pl.multiple_of((2 * h + z) * PEER_HALF, PEER_HALF)
            st = _dot_nt(keys_ref[2 * h + z], q_scr[:, pl.ds(off, PEER_HALF)].astype(BF16))
            vals, idxs = _topk_rows(st, kk)
            tops.append((jnp.concatenate(vals, axis=0), jnp.concatenate(idxs, axis=0)))
        (s0, i0), (s1, i1) = tops
        e0 = i0 * float(PEER_KEYS)
        cs = [s0[0:1] + s1] + [s0[i:i + 1] + s1[0:half] for i in range(1, half)] + [s0[half:] + s1[0:1]]
        ce = [e0[0:1] + i1] + [e0[i:i + 1] + i1[0:half] for i in range(1, half)] + [e0[half:] + i1[0:1]]
        best, experts = _topk_rows(jnp.concatenate(cs, axis=0), kk, jnp.concatenate(ce, axis=0))
        ex = [jnp.exp(b - best[0]) for b in best]
        den = ex[0]
        for t in ex[1:]:
            den = den + t
        rows = pl.ds(pl.multiple_of(h * kk, kk), kk)
        e_scr[rows, :] = jnp.concatenate(experts, axis=0)
        g_scr[rows, :] = jnp.concatenate(ex, axis=0) / den
        return carry

    lax.fori_loop(0, PEER_HEADS, head_body, 0)
    e_ref[...] = e_scr[...].T.astype(I32)
    gate_ref[...] = g_scr[...].T


def _route(n2, wq, keys):
    t, d = n2.shape
    tm = min(TM_ROUTE, t)
    nq = wq.shape[1]
    hk = PEER_HEADS * PEER_TOPK
    out = lambda: pl.BlockSpec((tm, hk), lambda i: (i, 0))
    return pl.pallas_call(
        _route_kernel,
        grid=(t // tm,),
        in_specs=[pl.BlockSpec((tm, d), lambda i: (i, 0)),
                  pl.BlockSpec((d, nq), lambda i: (0, 0)),
                  pl.BlockSpec(keys.shape, lambda i: (0, 0, 0))],
        out_specs=[out(), out()],
        out_shape=[jax.ShapeDtypeStruct((t, hk), I32), jax.ShapeDtypeStruct((t, hk), F32)],
        scratch_shapes=[pltpu.VMEM((tm, nq), F32), pltpu.VMEM((hk, tm), F32), pltpu.VMEM((hk, tm), F32)],
        compiler_params=_cparams(("parallel",)),
        name="peer_route",
    )(n2, wq, keys)


def _expert_kernel(n_ref, e_ref, gate_ref, u_ref, v_ref, out_ref, g_scr):
    tm = n_ref.shape[0]
    nk = PEER_KEYS
    step = pl.program_id(1)

    @pl.when(step == 0)
    def _():
        out_ref[...] = jnp.zeros_like(out_ref)
        sub = lax.broadcasted_iota(I32, (nk, LANES), 0)

        def tok_body(t, carry):
            re = e_ref[pl.ds(t, 1), :]
            rg = gate_ref[pl.ds(t, 1), :]
            ghi = rg.astype(BF16).astype(F32)
            hit_i = sub == (re >> 7)
            at = jnp.concatenate([jnp.where(hit_i, ghi, 0.0).astype(BF16),
                                  jnp.where(hit_i, rg - ghi, 0.0).astype(BF16)], axis=1)
            bt1 = jnp.where(sub == (re & (nk - 1)), 1.0, 0.0).astype(BF16)
            bt = jnp.concatenate([bt1, bt1], axis=1)
            g_scr[pl.ds(t, nk, stride=G_PITCH), :] = _dot_nt(at, bt)
            return carry

        lax.fori_loop(0, tm, tok_body, 0, unroll=TOKEN_UNROLL)

    hid = _dot_nt(n_ref[...], u_ref[...])
    act = 0.5 * hid * (1.0 + lax.erf(hid * (0.5 ** 0.5)))
    gates = [g_scr[pl.ds(pl.multiple_of((step * EXP_ROWS + r) * G_PITCH, SUBLANES), tm), :]
             for r in range(EXP_ROWS)]
    wts = (act * jnp.concatenate(gates, axis=1)).astype(BF16)
    out_ref[...] = out_ref[...] + _dot(wts, v_ref[...])


def _experts(n2, e, gate, eu, ev):
    t, d = n2.shape
    tm = min(TM_EXP, t)
    assert tm == TM_EXP, "gate slab pitch is derived from TM_EXP"
    assert PEER_KEYS == LANES, "expert ids are decoded with a 7-bit shift"
    hk = e.shape[1]
    er = EXP_ROWS * PEER_KEYS
    nsteps = eu.shape[0] // er
    return pl.pallas_call(
        _expert_kernel,
        grid=(t // tm, nsteps),
        in_specs=[pl.BlockSpec((tm, d), lambda i, s: (i, 0)),
                  pl.BlockSpec((tm, hk), lambda i, s: (i, 0)),
                  pl.BlockSpec((tm, hk), lambda i, s: (i, 0)),
                  pl.BlockSpec((er, d), lambda i, s: (s, 0)),
                  pl.BlockSpec((er, d), lambda i, s: (s, 0))],
        out_specs=pl.BlockSpec((tm, d), lambda i, s: (i, 0)),
        out_shape=jax.ShapeDtypeStruct((t, d), F32),
        scratch_shapes=[pltpu.VMEM((PEER_KEYS * G_PITCH, LANES), F32)],
        compiler_params=_cparams(("parallel", "arbitrary")),
        name="peer_experts",
    )(n2, e, gate, eu, ev)


def _ple_kernel(h_ref, f_ref, p_ref, gp_ref, wg_ref, wp_ref, gf_ref, y_ref):
    h = h_ref[...] + f_ref[...]
    gate = jax.nn.sigmoid(_dot(_rms(h, gp_ref[...]).astype(BF16), wg_ref[...]))
    h = h + gate * _dot(p_ref[...].astype(BF16), wp_ref[...])
    y_ref[...] = _rms(h, gf_ref[...])


def _ple(h1, ffn, p2d, gp, wg, wp, gfin):
    t, d = h1.shape
    pd = p2d.shape[1]
    tm = min(TM_PLE, t)
    vec = lambda: pl.BlockSpec((1, d), lambda i: (0, 0))
    tok = lambda: pl.BlockSpec((tm, d), lambda i: (i, 0))
    return pl.pallas_call(
        _ple_kernel,
        grid=(t // tm,),
        in_specs=[tok(), tok(), pl.BlockSpec((tm, pd), lambda i: (i, 0)),
                  vec(), pl.BlockSpec((d, d), lambda i: (0, 0)), pl.BlockSpec((pd, d), lambda i: (0, 0)), vec()],
        out_specs=tok(),
        out_shape=jax.ShapeDtypeStruct((t, d), F32),
        compiler_params=_cparams(("parallel",)),
        name="ple_final",
    )(h1, ffn, p2d, gp, wg, wp, gfin)


def _pad_lanes(v):
    return jnp.pad(v, (0, LANES - v.shape[0]))


def _prepare(norm_mix, w_in, ssd_conv_w, ssd_conv_b, ssd_a_log, ssd_dt_bias, ssd_d, ssd_norm, w_ssd_out,
             gdn_conv_w, gdn_a_log, gdn_dt_bias, gdn_norm, w_gdn_out, w_out, norm_ffn, peer_query, peer_keys,
             expert_u, expert_v, norm_ple, w_ple_gate, w_ple_proj, norm_final):
    w = w_in[0]
    xbc = SSD_INNER + 2 * SSD_GROUPS * SSD_STATE
    c_dt = SSD_INNER + xbc
    c_qkv = c_dt + 2 * SSD_HEADS
    c_a = c_qkv + 4 * GDN_INNER
    c_ga = c_a + 4 * GDN_HEADS
    w_main = jnp.concatenate([w[:, :c_dt], w[:, c_qkv:c_a], w[:, c_ga:]], axis=1).astype(BF16)
    w_small = jnp.concatenate([w[:, c_dt:c_qkv], w[:, c_a:c_ga]], axis=1)
    w_small = jnp.pad(w_small, ((0, 0), (0, LANES - w_small.shape[1]))).astype(BF16)
    hp = jnp.stack([
        _pad_lanes(jnp.concatenate([ssd_dt_bias[0].reshape(-1), gdn_dt_bias[0].reshape(-1)])),
        _pad_lanes(jnp.concatenate([ssd_a_log[0].reshape(-1), gdn_a_log[0].reshape(-1)])),
        _pad_lanes(ssd_d[0]),
    ])
    hp = jnp.pad(hp, ((0, SUBLANES - hp.shape[0]), (0, 0)))
    row = lambda v: v.reshape(1, -1)
    return dict(
        g_mix=row(norm_mix[0]), w_main=w_main, w_small=w_small, hp=hp,
        ssd_conv_w=ssd_conv_w[0], ssd_conv_b=row(ssd_conv_b[0]), ssd_norm=row(ssd_norm[0]),
        gdn_conv_w=gdn_conv_w[0], gdn_norm=row(gdn_norm[0]),
        wa=w_ssd_out[0].astype(BF16), wb=w_gdn_out[0].astype(BF16), wo=w_out[0].astype(BF16),
        g_ffn=row(norm_ffn[0]), wq=peer_query[0].astype(BF16),
        keys=peer_keys[0].reshape(2 * PEER_HEADS, PEER_KEYS, PEER_HALF).astype(BF16),
        eu=expert_u[0].astype(BF16), ev=expert_v[0].astype(BF16),
        g_ple=row(norm_ple[0]), wg=w_ple_gate[0].astype(BF16), wp=w_ple_proj[0].astype(BF16),
        g_fin=row(norm_final),
    )


def _trunk(x, ple, p):
    b, length, d = x.shape
    t = b * length
    x2d = x.reshape(t, d)
    proj, small = _in_proj(x2d, p["g_mix"], p["w_main"], p["w_small"])
    proj3 = proj.reshape(b, length, -1)
    small3 = small.reshape(b, length, LANES)
    ya = _ssd_mixer(proj3, small3, p["ssd_conv_w"], p["ssd_conv_b"], p["hp"], p["ssd_norm"])
    yb = _gdn_mixer(proj3, small3, p["gdn_conv_w"], p["hp"], p["gdn_norm"])
    h1, n2 = _merge(x2d, ya.reshape(t, -1), yb.reshape(t, -1), proj, p["wa"], p["wb"], p["wo"], p["g_ffn"])
    e, gate = _route(n2, p["wq"], p["keys"])
    ffn = _experts(n2, e, gate, p["eu"], p["ev"])
    y = _ple(h1, ffn, ple.reshape(t, -1), p["g_ple"], p["wg"], p["wp"], p["g_fin"])
    return y.reshape(b, length, d)


def kernel(x_prompt, x_sample, p_prompt, p_sample, norm_mix, w_in, ssd_conv_w, ssd_conv_b, ssd_a_log, ssd_dt_bias, ssd_d, ssd_norm, w_ssd_out, gdn_conv_w, gdn_a_log, gdn_dt_bias, gdn_norm, w_gdn_out, w_out, norm_ffn, peer_query, peer_keys, expert_u, expert_v, norm_ple, w_ple_gate, w_ple_proj, norm_final):
    assert w_in.shape[0] == 1, "single-layer trunk"
    p = _prepare(norm_mix, w_in, ssd_conv_w, ssd_conv_b, ssd_a_log, ssd_dt_bias, ssd_d, ssd_norm, w_ssd_out,
                 gdn_conv_w, gdn_a_log, gdn_dt_bias, gdn_norm, w_gdn_out, w_out, norm_ffn, peer_query, peer_keys,
                 expert_u, expert_v, norm_ple, w_ple_gate, w_ple_proj, norm_final)
    return (_trunk(x_prompt, p_prompt[0], p), _trunk(x_sample, p_sample[0], p))
```

```python
import jax
import jax.numpy as jnp
from jax import lax
from jax.experimental import pallas as pl
from jax.experimental.pallas import tpu as pltpu

F32 = jnp.float32
BF16 = jnp.bfloat16
I32 = jnp.int32

EPS = 1e-6
CONV_K = 4
SSD_HEADS = 16
SSD_HEAD_DIM = 64
SSD_GROUPS = 4
SSD_STATE = 128
SSD_INNER = SSD_HEADS * SSD_HEAD_DIM
GDN_HEADS = 8
GDN_HEAD_DIM = 128
GDN_INNER = GDN_HEADS * GDN_HEAD_DIM
PEER_HEADS = 8
PEER_KEYS = 128
PEER_TOPK = 16
PEER_HALF = 128

LANES = 128
SUBLANES = 8
SSD_CHUNK = 128
GDN_CHUNK = 128
GDN_PREP_CHUNKS = 4
VMEM_LIMIT = 56 * 1024 * 1024

TM_PROJ = 1024
TN_PROJ = 1024
TM_MERGE = 256
TM_ROUTE = 256
TM_EXP = 512
EXP_ROWS = 4
G_PITCH = TM_EXP + 8
TOKEN_UNROLL = 16
TM_PLE = 512

NEG = -1e30


def _dot(a, b):
    return jnp.dot(a, b, preferred_element_type=F32)


def _dot_nt(a, b):
    return lax.dot_general(a, b, (((1,), (1,)), ((), ())), preferred_element_type=F32)


def _dot_tn(a, b):
    return lax.dot_general(a, b, (((0,), (0,)), ((), ())), preferred_element_type=F32)


def _bf16_terms(m, terms):
    out = []
    for _ in range(terms - 1):
        t = m.astype(BF16)
        out.append(t)
        m = m - t.astype(F32)
    out.append(m.astype(BF16))
    return out


def _dot_exact_lhs(lb, m, terms=3):
    n = m.shape[1]
    r = _dot(lb, jnp.concatenate(_bf16_terms(m, terms), axis=1))
    return sum(r[:, i * n:(i + 1) * n] for i in range(1, terms)) + r[:, :n]


def _dot_exact_rhs(m, eb, terms=2):
    return _dot(jnp.concatenate(_bf16_terms(m, terms), axis=1), jnp.concatenate([eb] * terms, axis=0))


def _rows_transposed(sel, m, terms=3):
    n = m.shape[0]
    r = _dot_nt(sel, jnp.concatenate(_bf16_terms(m, terms), axis=0))
    return sum(r[:, i * n:(i + 1) * n] for i in range(1, terms)) + r[:, :n]


def _softplus(x):
    return jnp.maximum(x, 0.0) + jnp.log1p(jnp.exp(-jnp.abs(x)))


def _silu(x):
    return x * jax.nn.sigmoid(x)


def _rms(x, g):
    return x * lax.rsqrt(jnp.mean(x * x, axis=-1, keepdims=True) + EPS) * g


def _cparams(sem):
    return pltpu.CompilerParams(dimension_semantics=sem, vmem_limit_bytes=VMEM_LIMIT)


def _in_proj_kernel(x_ref, g_ref, wm_ref, ws_ref, main_ref, small_ref, n_scr):
    @pl.when(pl.program_id(1) == 0)
    def _():
        n = _rms(x_ref[...], g_ref[...]).astype(BF16)
        n_scr[...] = n
        small_ref[...] = _dot(n, ws_ref[...])

    main_ref[...] = _dot(n_scr[...], wm_ref[...])


def _in_proj(x2d, g, w_main, w_small):
    t, d = x2d.shape
    n_main = w_main.shape[1]
    tm = min(TM_PROJ, t)
    return pl.pallas_call(
        _in_proj_kernel,
        grid=(t // tm, n_main // TN_PROJ),
        in_specs=[
            pl.BlockSpec((tm, d), lambda i, j: (i, 0)),
            pl.BlockSpec((1, d), lambda i, j: (0, 0)),
            pl.BlockSpec((d, TN_PROJ), lambda i, j: (0, j)),
            pl.BlockSpec((d, LANES), lambda i, j: (0, 0)),
        ],
        out_specs=[
            pl.BlockSpec((tm, TN_PROJ), lambda i, j: (i, j)),
            pl.BlockSpec((tm, LANES), lambda i, j: (i, 0)),
        ],
        out_shape=[jax.ShapeDtypeStruct((t, n_main), F32), jax.ShapeDtypeStruct((t, LANES), F32)],
        scratch_shapes=[pltpu.VMEM((tm, d), BF16)],
        compiler_params=_cparams(("parallel", "arbitrary")),
        name="in_proj",
    )(x2d, g, w_main, w_small)


def _conv_silu_chunk(src_ref, w_ref, b_ref, c, nch, rows):
    length = src_ref.shape[1]
    base = pl.multiple_of(c * rows, rows)
    main = src_ref[0, pl.ds(base, rows), :]
    prev = src_ref[0, pl.ds(pl.multiple_of(jnp.maximum(base - SUBLANES, 0), SUBLANES), SUBLANES), :]
    nxt = src_ref[0, pl.ds(pl.multiple_of(jnp.minimum(base + rows, length - SUBLANES), SUBLANES), SUBLANES), :]
    prev = jnp.where(c > 0, prev, 0.0)
    nxt = jnp.where(c < nch - 1, nxt, 0.0)
    ext = jnp.concatenate([prev, main, nxt], axis=0)
    w = w_ref[...]
    lo = SUBLANES - CONV_K // 2
    y = ext[lo:lo + rows] * w[0:1]
    for j in range(1, CONV_K):
        y = y + ext[lo + j:lo + j + rows] * w[j:j + 1]
    if b_ref is not None:
        y = y + b_ref[...]
    return _silu(y)


def _expand_heads(t, lane0, width):
    r = t.shape[0]
    hid = lax.broadcasted_iota(I32, (r, 4 * width), 1) // width
    out = jnp.broadcast_to(t[:, lane0 + 3:lane0 + 4], (r, 4 * width))
    for j in (2, 1, 0):
        out = jnp.where(hid == j, t[:, lane0 + j:lane0 + j + 1], out)
    return out


def _ssd_kernel(z_ref, xs_ref, bm_ref, cm_ref, sm_ref, cwx_ref, cwb_ref, cwc_ref, cbx_ref, cbb_ref, cbc_ref,
                hp_ref, nw_ref, out_ref, xc_scr, bc_scr, cc_scr, yf_scr, yb_scr, stf_scr, stb_scr):
    ch = SSD_CHUNK
    length = z_ref.shape[1]
    nch = length // ch
    gwid = 4 * SSD_HEAD_DIM
    grp = pl.program_id(1)
    shift = lax.rem(LANES - 4 * grp, LANES)
    hp = pltpu.roll(hp_ref[...], shift, 1)
    bias_row = hp[0:1, :]
    a_row = -jnp.exp(hp[1:2, :])
    dsk_x = _expand_heads(hp[2:3, :], 0, SSD_HEAD_DIM)

    row = lax.broadcasted_iota(I32, (ch, ch), 0)
    col = lax.broadcasted_iota(I32, (ch, ch), 1)
    hid = lax.broadcasted_iota(I32, (ch, gwid), 1) // SSD_HEAD_DIM

    def conv_body(c, carry):
        base = pl.multiple_of(c * ch, ch)
        xc_scr[pl.ds(base, ch), :] = _conv_silu_chunk(xs_ref, cwx_ref, cbx_ref, c, nch, ch)
        bc_scr[pl.ds(base, ch), :] = _conv_silu_chunk(bm_ref, cwb_ref, cbb_ref, c, nch, ch)
        cc_scr[pl.ds(base, ch), :] = _conv_silu_chunk(cm_ref, cwc_ref, cbc_ref, c, nch, ch)
        return carry

    lax.fori_loop(0, nch, conv_body, 0)

    masks = ((row >= col), (row <= col))
    lmats = tuple(jnp.where(m, 1.0, 0.0).astype(BF16) for m in masks)
    lanes0 = (0, SSD_HEADS)
    tot_rows = (ch - 1, 0)
    erow = lax.broadcasted_iota(I32, (LANES, gwid), 0)
    ecol = lax.broadcasted_iota(I32, (LANES, gwid), 1) // SSD_HEAD_DIM
    emats = tuple(jnp.where(erow == ecol + l0, 1.0, 0.0).astype(BF16) for l0 in lanes0)
    srow = lax.broadcasted_iota(I32, (SUBLANES, LANES), 0)
    scol = lax.broadcasted_iota(I32, (SUBLANES, LANES), 1)
    sels = tuple(jnp.where(scol == srow + l0, 1.0, 0.0).astype(BF16) for l0 in lanes0)
    st_refs = (stf_scr, stb_scr)
    y_refs = (yf_scr, yb_scr)
    dirs = (0, 1)

    stf_scr[...] = jnp.zeros_like(stf_scr)
    stb_scr[...] = jnp.zeros_like(stb_scr)

    def scan_body(ci, carry):
        bases = (pl.multiple_of(ci * ch, ch), pl.multiple_of((nch - 1 - ci) * ch, ch))
        x = [xc_scr[pl.ds(b, ch), :] for b in bases]
        bb = [bc_scr[pl.ds(b, ch), :] for b in bases]
        cb = [cc_scr[pl.ds(b, ch), :].astype(BF16) for b in bases]
        dt_all = [_softplus(pltpu.roll(sm_ref[0, pl.ds(b, ch), :], shift, 1) + bias_row) for b in bases]
        acum = [_dot_exact_lhs(lmats[d], dt_all[d] * a_row) for d in dirs]
        sc = [jnp.where(masks[d], _dot_nt(cb[d], bb[d].astype(BF16)), 0.0) for d in dirs]
        st = [st_refs[d][...] for d in dirs]
        y_off = [_dot(cb[d], st[d].astype(BF16)) for d in dirs]
        dt_x = [_dot_exact_rhs(dt_all[d], emats[d]) for d in dirs]
        acum_t = [_rows_transposed(sels[d], acum[d]) for d in dirs]
        tot = [acum[d][tot_rows[d]:tot_rows[d] + 1, :] for d in dirs]
        decay_x = [_dot_exact_rhs(jnp.concatenate([jnp.exp(acum[d]), jnp.exp(tot[d] - acum[d])], axis=0), emats[d])
                   for d in dirs]
        xd = [x[d] * dt_x[d] for d in dirs]
        lhs, rhs = [], []
        for d in dirs:
            mixes, parts = [], []
            for j in range(4):
                ln = lanes0[d] + j
                dd = acum[d][:, ln:ln + 1] - acum_t[d][j:j + 1, :]
                mixes.append((sc[d] * jnp.exp(jnp.where(masks[d], dd, NEG))).astype(BF16))
                parts.append(jnp.where(hid == j, xd[d], 0.0).astype(BF16))
            lhs.append(jnp.concatenate(mixes, axis=1))
            rhs.append(jnp.concatenate(parts, axis=0))
        y = [_dot(lhs[d], rhs[d]) for d in dirs]
        upd = [_dot_tn(bb[d].astype(BF16), (xd[d] * decay_x[d][ch:]).astype(BF16)) for d in dirs]
        for d in dirs:
            ea_x = decay_x[d][:ch]
            etot_x = ea_x[tot_rows[d]:tot_rows[d] + 1, :]
            y_refs[d][pl.ds(bases[d], ch), :] = y[d] + y_off[d] * ea_x
            st_refs[d][...] = st[d] * etot_x + upd[d]
        return carry

    lax.fori_loop(0, nch, scan_body, 0)

    def fin_body(c, carry):
        base = pl.multiple_of(c * ch, ch)
        y = yf_scr[pl.ds(base, ch), :] + yb_scr[pl.ds(base, ch), :] + xc_scr[pl.ds(base, ch), :] * dsk_x
        y = y * _silu(z_ref[0, pl.ds(base, ch), :])
        out_ref[0, pl.ds(base, ch), :] = _rms(y, nw_ref[...])
        return carry

    lax.fori_loop(0, nch, fin_body, 0)


def _ssd_mixer(proj3, small3, conv_w, conv_b, hp, norm_w):
    b, length, _ = proj3.shape
    gw = 4 * SSD_HEAD_DIM
    xs0 = SSD_INNER // gw
    b0 = 2 * SSD_INNER // SSD_STATE
    c0 = b0 + SSD_GROUPS
    nb_x = SSD_INNER // SSD_STATE
    seq = lambda w, off: pl.BlockSpec((1, length, w), lambda i, g: (i, 0, off + g))
    par = lambda r, w, off: pl.BlockSpec((r, w), lambda i, g: (0, off + g))
    return pl.pallas_call(
        _ssd_kernel,
        grid=(b, SSD_GROUPS),
        in_specs=[
            seq(gw, 0), seq(gw, xs0), seq(SSD_STATE, b0), seq(SSD_STATE, c0),
            pl.BlockSpec((1, length, LANES), lambda i, g: (i, 0, 0)),
            par(CONV_K, gw, 0), par(CONV_K, SSD_STATE, nb_x), par(CONV_K, SSD_STATE, nb_x + SSD_GROUPS),
            par(1, gw, 0), par(1, SSD_STATE, nb_x), par(1, SSD_STATE, nb_x + SSD_GROUPS),
            pl.BlockSpec((SUBLANES, LANES), lambda i, g: (0, 0)),
            par(1, gw, 0),
        ],
        out_specs=pl.BlockSpec((1, length, gw), lambda i, g: (i, 0, g)),
        out_shape=jax.ShapeDtypeStruct((b, length, SSD_INNER), F32),
        scratch_shapes=[
            pltpu.VMEM((length, gw), F32), pltpu.VMEM((length, SSD_STATE), F32),
            pltpu.VMEM((length, SSD_STATE), F32), pltpu.VMEM((length, gw), F32), pltpu.VMEM((length, gw), F32),
            pltpu.VMEM((SSD_STATE, gw), F32), pltpu.VMEM((SSD_STATE, gw), F32),
        ],
        compiler_params=_cparams(("parallel", "arbitrary")),
        name="ssd_mixer",
    )(proj3, proj3, proj3, proj3, small3, conv_w, conv_w, conv_w, conv_b, conv_b, conv_b, hp, norm_w)


def _gdn_kernel(q_ref, k_ref, v_ref, z_ref, sm_ref, cwq_ref, cwk_ref, cwv_ref, hp_ref, nw_ref, out_ref,
                qn_scr, kn_scr, vn_scr, u_scr, wq_scr, qk_scr, kd_scr, et_scr, o_scr):
    ch = GDN_CHUNK
    length = z_ref.shape[1]
    nch = length // ch
    dk = GDN_HEAD_DIM
    head = pl.program_id(1)
    shift = lax.rem(LANES - head, LANES)
    lane_a = 2 * SSD_HEADS
    lane_beta = lane_a + 2 * GDN_HEADS
    hp = pltpu.roll(hp_ref[...], shift, 1)
    bias_row = hp[0:1, :]
    acoef_row = jnp.exp(hp[1:2, :])

    row = lax.broadcasted_iota(I32, (ch, ch), 0)
    col = lax.broadcasted_iota(I32, (ch, ch), 1)
    eye = jnp.where(row == col, 1.0, 0.0)
    n_double = (ch - 1).bit_length() - 1

    def conv_body(c, carry):
        base = pl.multiple_of(c * ch, ch)
        q = _conv_silu_chunk(q_ref, cwq_ref, None, c, nch, ch)
        k = _conv_silu_chunk(k_ref, cwk_ref, None, c, nch, ch)
        qn_scr[pl.ds(base, ch), :] = q * (lax.rsqrt(jnp.sum(q * q, axis=-1, keepdims=True) + EPS) * dk ** -0.5)
        kn_scr[pl.ds(base, ch), :] = k * lax.rsqrt(jnp.sum(k * k, axis=-1, keepdims=True) + EPS)
        vn_scr[pl.ds(base, ch), :] = _conv_silu_chunk(v_ref, cwv_ref, None, c, nch, ch)
        return carry

    lax.fori_loop(0, nch, conv_body, 0)

    incl = ((row >= col), (row <= col))
    strict = ((row > col), (row < col))
    lmat = tuple(jnp.where(m, 1.0, 0.0).astype(BF16) for m in incl)
    umat = tuple(jnp.where(m, 1.0, 0.0) for m in strict)
    jref = (0, ch - 1)
    tot_row = (ch - 1, 0)
    group = min(GDN_PREP_CHUNKS, nch)

    def prep_body(gi, carry):
        chains = []
        for ci in range(group):
            c = gi * group + ci
            base = pl.multiple_of(c * ch, ch)
            q = qn_scr[pl.ds(base, ch), :]
            k = kn_scr[pl.ds(base, ch), :]
            v = vn_scr[pl.ds(base, ch), :]
            sm = pltpu.roll(sm_ref[0, pl.ds(base, ch), :], shift, 1)
            g_all = -acoef_row * _softplus(sm + bias_row)
            beta_all = jax.nn.sigmoid(sm)
            kbf = k.astype(BF16)
            kq = _dot_nt(jnp.concatenate([kbf, q.astype(BF16)], axis=0), kbf)
            for di in (0, 1):
                ln = lane_a + di * GDN_HEADS
                lb = lane_beta + di * GDN_HEADS
                chains.append(dict(c=c, di=di, q=q, k=k, v=v, kq=kq, g=g_all[:, ln:ln + 1],
                                   beta=beta_all[:, lb:lb + 1]))
        for t in chains:
            t["dmat"] = _dot_exact_lhs(lmat[t["di"]], t["g"] * umat[t["di"]], terms=2)
        for t in chains:
            di = t["di"]
            t["kb"] = t["k"] * t["beta"]
            dec = t["dmat"][:, jref[di]:jref[di] + 1] + t["g"][jref[di]:jref[di] + 1, :]
            t["dec"] = dec
            t["tot"] = dec[tot_row[di]:tot_row[di] + 1, :]
            gam = jnp.exp(jnp.where(incl[di], t["dmat"], NEG))
            t["qk"] = (t["kq"][ch:] * gam).astype(BF16)
            a = jnp.where(strict[di], t["kq"][:ch] * (gam * t["beta"]), 0.0)
            t["p"] = eye - a
            t["ab"] = a.astype(BF16)
        for t in chains:
            t["x"] = _dot(t["ab"], t["ab"])
        for i in range(n_double):
            last = i == n_double - 1
            for t in chains:
                xb = t["x"].astype(BF16)
                pb = t["p"].astype(BF16)
                t["r"] = _dot(xb, pb) if last else _dot(xb, jnp.concatenate([pb, xb], axis=1))
            for t in chains:
                t["p"] = t["p"] + t["r"][:, :ch]
                if not last:
                    t["x"] = t["r"][:, ch:]
        for t in chains:
            t["edec"] = jnp.exp(t["dec"])
            rhs = jnp.concatenate([t["v"] * t["beta"], t["kb"] * t["edec"]], axis=1).astype(BF16)
            t["uw"] = _dot(t["p"].astype(BF16), rhs)
        for t in chains:
            c, di = t["c"], t["di"]
            base = pl.multiple_of(c * ch, ch)
            base2 = pl.multiple_of(c * 2 * ch, 2 * ch)
            u_scr[di, pl.ds(base, ch), :] = t["uw"][:, :dk]
            wq_scr[di, pl.ds(base2, 2 * ch), :] = jnp.concatenate(
                [t["uw"][:, dk:], t["q"] * t["edec"]], axis=0).astype(BF16)
            qk_scr[di, pl.ds(base, ch), :] = t["qk"]
            kd_scr[di, pl.ds(base, ch), :] = (t["k"] * jnp.exp(t["tot"] - t["dec"])).astype(BF16)
            et_scr[di, pl.ds(pl.multiple_of(c * SUBLANES, SUBLANES), SUBLANES), :] = jnp.broadcast_to(
                jnp.exp(t["tot"]), (SUBLANES, LANES))
        return carry

    lax.fori_loop(0, nch // group, prep_body, 0)

    def rec_body(ci, carry):
        cs = (ci, nch - 1 - ci)
        bases = [pl.multiple_of(c * ch, ch) for c in cs]
        ws = [_dot(wq_scr[di, pl.ds(pl.multiple_of(cs[di] * 2 * ch, 2 * ch), 2 * ch), :], carry[di].astype(BF16))
              for di in (0, 1)]
        v_new = [(u_scr[di, pl.ds(bases[di], ch), :] - ws[di][:ch]).astype(BF16) for di in (0, 1)]
        upd = [_dot_tn(kd_scr[di, pl.ds(bases[di], ch), :], v_new[di]) for di in (0, 1)]
        intra = [_dot(qk_scr[di, pl.ds(bases[di], ch), :], v_new[di]) for di in (0, 1)]
        s_new = []
        for di in (0, 1):
            et = et_scr[di, pl.ds(pl.multiple_of(cs[di] * SUBLANES, SUBLANES), 1), :]
            s_new.append(carry[di] * et + upd[di])
            o_scr[di, pl.ds(bases[di], ch), :] = ws[di][ch:] + intra[di]
        return tuple(s_new)

    zero = jnp.zeros((dk, dk), F32)
    lax.fori_loop(0, nch, rec_body, (zero, zero))

    def fin_body(c, carry):
        base = pl.multiple_of(c * ch, ch)
        o = o_scr[0, pl.ds(base, ch), :] + o_scr[1, pl.ds(base, ch), :]
        out_ref[0, pl.ds(base, ch), :] = _rms(o, nw_ref[...]) * _silu(z_ref[0, pl.ds(base, ch), :])
        return carry

    lax.fori_loop(0, nch, fin_body, 0)


def _gdn_mixer(proj3, small3, conv_w, hp, norm_w):
    b, length, _ = proj3.shape
    dk = GDN_HEAD_DIM
    nch = length // GDN_CHUNK
    q0 = (2 * SSD_INNER + 2 * SSD_GROUPS * SSD_STATE) // dk
    seq = lambda off: pl.BlockSpec((1, length, dk), lambda i, h: (i, 0, off + h))
    par = lambda off: pl.BlockSpec((CONV_K, dk), lambda i, h: (0, off + h))
    return pl.pallas_call(
        _gdn_kernel,
        grid=(b, GDN_HEADS),
        in_specs=[
            seq(q0), seq(q0 + GDN_HEADS), seq(q0 + 2 * GDN_HEADS), seq(q0 + 3 * GDN_HEADS),
            pl.BlockSpec((1, length, LANES), lambda i, h: (i, 0, 0)),
            par(0), par(GDN_HEADS), par(2 * GDN_HEADS),
            pl.BlockSpec((SUBLANES, LANES), lambda i, h: (0, 0)),
            pl.BlockSpec((1, dk), lambda i, h: (0, 0)),
        ],
        out_specs=pl.BlockSpec((1, length, dk), lambda i, h: (i, 0, h)),
        out_shape=jax.ShapeDtypeStruct((b, length, GDN_INNER), F32),
        scratch_shapes=[
            pltpu.VMEM((length, dk), F32), pltpu.VMEM((length, dk), F32), pltpu.VMEM((length, dk), F32),
            pltpu.VMEM((2, length, dk), F32),
            pltpu.VMEM((2, 2 * length, dk), BF16),
            pltpu.VMEM((2, length, GDN_CHUNK), BF16),
            pltpu.VMEM((2, length, dk), BF16),
            pltpu.VMEM((2, nch * SUBLANES, LANES), F32),
            pltpu.VMEM((2, length, dk), F32),
        ],
        compiler_params=_cparams(("parallel", "arbitrary")),
        name="gdn_mixer",
    )(proj3, proj3, proj3, proj3, small3, conv_w, conv_w, conv_w, hp, norm_w)


def _merge_kernel(x_ref, ya_ref, yb_ref, ga_ref, gb_ref, wa_ref, wb_ref, wo_ref, gf_ref, h_ref, n_ref):
    ya = _dot(ya_ref[...].astype(BF16), wa_ref[...])
    yb = _dot(yb_ref[...].astype(BF16), wb_ref[...])
    merged = jax.nn.sigmoid(ga_ref[...]) * ya + jax.nn.sigmoid(gb_ref[...]) * yb
    h = x_ref[...] + _dot(merged.astype(BF16), wo_ref[...])
    h_ref[...] = h
    n_ref[...] = _rms(h, gf_ref[...]).astype(BF16)


def _merge(x2d, ya, yb, proj2d, wa, wb, wo, gf):
    t, d = x2d.shape
    tm = min(TM_MERGE, t)
    ga0 = (proj2d.shape[1] - 2 * d) // d
    tok = lambda: pl.BlockSpec((tm, d), lambda i: (i, 0))
    wsp = lambda: pl.BlockSpec((d, d), lambda i: (0, 0))
    return pl.pallas_call(
        _merge_kernel,
        grid=(t // tm,),
        in_specs=[tok(), tok(), tok(),
                  pl.BlockSpec((tm, d), lambda i: (i, ga0)), pl.BlockSpec((tm, d), lambda i: (i, ga0 + 1)),
                  wsp(), wsp(), wsp(), pl.BlockSpec((1, d), lambda i: (0, 0))],
        out_specs=[tok(), tok()],
        out_shape=[jax.ShapeDtypeStruct((t, d), F32), jax.ShapeDtypeStruct((t, d), BF16)],
        compiler_params=_cparams(("parallel",)),
        name="merge",
    )(x2d, ya, yb, proj2d, proj2d, wa, wb, wo, gf)


def _topk_store(problems, k):
    state = [p[0] for p in problems]
    iotas = [lax.broadcasted_iota(I32, s.shape, 0).astype(F32) for s in state]
    for i in range(k):
        ms = [jnp.max(s, axis=0, keepdims=True) for s in state]
        cands = [jnp.where(s == m, io, float(s.shape[0])) for s, m, io in zip(state, ms, iotas)]
        ixs = [jnp.min(c, axis=0, keepdims=True) for c in cands]
        hits = [c == ix for c, ix in zip(cands, ixs)]
        for (_, vals_ref, outs_ref, payload), m, ix, hit in zip(problems, ms, ixs, hits):
            vals_ref[i:i + 1, :] = m
            outs_ref[i:i + 1, :] = (ix if payload is None
                                    else jnp.max(jnp.where(hit, payload, -1.0), axis=0, keepdims=True))
        state = [jnp.where(hit, -jnp.inf, s) for s, hit in zip(state, hits)]


def _route_kernel(n_ref, wq_ref, keys_ref, e_ref, gate_ref, q_scr, e_scr, g_scr, tv_scr, ti_scr, bv_scr, be_scr):
    kk = PEER_TOPK
    half = kk // 2
    tm = n_ref.shape[0]
    ntile = tm // LANES
    q_scr[...] = _dot(n_ref[...], wq_ref[...])

    def head_body(h, carry):
        st = []
        for z in range(2):
            off = pl.multiple_of((2 * h + z) * PEER_HALF, PEER_HALF)
            st.append(_dot_nt(keys_ref[2 * h + z], q_scr[:, pl.ds(off, PEER_HALF)].astype(BF16)))
        rows = pl.ds(pl.multiple_of(h * kk, kk), kk)
        for lt in range(ntile):
            lanes = slice(lt * LANES, (lt + 1) * LANES)
            _topk_store([(st[z][:, lanes], tv_scr.at[lt, z], ti_scr.at[lt, z], None) for z in range(2)], kk)
        problems = []
        for lt in range(ntile):
            s0, s1 = tv_scr[lt, 0], tv_scr[lt, 1]
            e0, i1 = ti_scr[lt, 0] * float(PEER_KEYS), ti_scr[lt, 1]
            cs = [s0[0:1] + s1] + [s0[i:i + 1] + s1[0:half] for i in range(1, half)] + [s0[half:] + s1[0:1]]
            ce = [e0[0:1] + i1] + [e0[i:i + 1] + i1[0:half] for i in range(1, half)] + [e0[half:] + i1[0:1]]
            problems.append((jnp.concatenate(cs, axis=0), bv_scr.at[lt], be_scr.at[lt], jnp.concatenate(ce, axis=0)))
        _topk_store(problems, kk)
        for lt in range(ntile):
            lanes = slice(lt * LANES, (lt + 1) * LANES)
            best = bv_scr[lt]
            ex = jnp.exp(best - best[0:1])
            e_scr[rows, lanes] = be_scr[lt]
            g_scr[rows, lanes] = ex / jnp.sum(ex, axis=0, keepdims=True)
        return carry

    lax.fori_loop(0, PEER_HEADS, head_body, 0)
    e_ref[...] = e_scr[...].T.astype(I32)
    gate_ref[...] = g_scr[...].T


def _route(n2, wq, keys):
    t, d = n2.shape
    tm = min(TM_ROUTE, t)
    nq = wq.shape[1]
    hk = PEER_HEADS * PEER_TOPK
    out = lambda: pl.BlockSpec((tm, hk), lambda i: (i, 0))
    return pl.pallas_call(
        _route_kernel,
        grid=(t // tm,),
        in_specs=[pl.BlockSpec((tm, d), lambda i: (i, 0)),
                  pl.BlockSpec((d, nq), lambda i: (0, 0)),
                  pl.BlockSpec(keys.shape, lambda i: (0, 0, 0))],
        out_specs=[out(), out()],
        out_shape=[jax.ShapeDtypeStruct((t, hk), I32), jax.ShapeDtypeStruct((t, hk), F32)],
        scratch_shapes=[pltpu.VMEM((tm, nq), F32), pltpu.VMEM((hk, tm), F32), pltpu.VMEM((hk, tm), F32),
                        pltpu.VMEM((tm // LANES, 2, PEER_TOPK, LANES), F32),
                        pltpu.VMEM((tm // LANES, 2, PEER_TOPK, LANES), F32),
                        pltpu.VMEM((tm // LANES, PEER_TOPK, LANES), F32),
                        pltpu.VMEM((tm // LANES, PEER_TOPK, LANES), F32)],
        compiler_params=_cparams(("parallel",)),
        name="peer_route",
    )(n2, wq, keys)


def _expert_kernel(n_ref, e_ref, gate_ref, u_ref, v_ref, out_ref, g_scr):
    tm = n_ref.shape[0]
    nk = PEER_KEYS
    step = pl.program_id(1)

    @pl.when(step == 0)
    def _():
        out_ref[...] = jnp.zeros_like(out_ref)
        sub = lax.broadcasted_iota(I32, (nk, LANES), 0)

        def tok_body(t, carry):
            re = e_ref[pl.ds(t, 1), :]
            rg = gate_ref[pl.ds(t, 1), :]
            ghi = rg.astype(BF16).astype(F32)
            hit_i = sub == (re >> 7)
            at = jnp.concatenate([jnp.where(hit_i, ghi, 0.0).astype(BF16),
                                  jnp.where(hit_i, rg - ghi, 0.0).astype(BF16)], axis=1)
            bt1 = jnp.where(sub == (re & (nk - 1)), 1.0, 0.0).astype(BF16)
            bt = jnp.concatenate([bt1, bt1], axis=1)
            g_scr[pl.ds(t, nk, stride=G_PITCH), :] = _dot_nt(at, bt)
            return carry

        lax.fori_loop(0, tm, tok_body, 0, unroll=TOKEN_UNROLL)

    hid = _dot_nt(n_ref[...], u_ref[...])
    act = 0.5 * hid * (1.0 + lax.erf(hid * (0.5 ** 0.5)))
    gates = [g_scr[pl.ds(pl.multiple_of((step * EXP_ROWS + r) * G_PITCH, SUBLANES), tm), :]
             for r in range(EXP_ROWS)]
    wts = (act * jnp.concatenate(gates, axis=1)).astype(BF16)
    out_ref[...] = out_ref[...] + _dot(wts, v_ref[...])


def _experts(n2, e, gate, eu, ev):
    t, d = n2.shape
    tm = min(TM_EXP, t)
    assert tm == TM_EXP, "gate slab pitch is derived from TM_EXP"
    assert PEER_KEYS == LANES, "expert ids are decoded with a 7-bit shift"
    hk = e.shape[1]
    er = EXP_ROWS * PEER_KEYS
    nsteps = eu.shape[0] // er
    return pl.pallas_call(
        _expert_kernel,
        grid=(t // tm, nsteps),
        in_specs=[pl.BlockSpec((tm, d), lambda i, s: (i, 0)),
                  pl.BlockSpec((tm, hk), lambda i, s: (i, 0)),
                  pl.BlockSpec((tm, hk), lambda i, s: (i, 0)),
                  pl.BlockSpec((er, d), lambda i, s: (s, 0)),
                  pl.BlockSpec((er, d), lambda i, s: (s, 0))],
        out_specs=pl.BlockSpec((tm, d), lambda i, s: (i, 0)),
        out_shape=jax.ShapeDtypeStruct((t, d), F32),
        scratch_shapes=[pltpu.VMEM((PEER_KEYS * G_PITCH, LANES), F32)],
        compiler_params=_cparams(("parallel", "arbitrary")),
        name="peer_experts",
    )(n2, e, gate, eu, ev)


def _ple_kernel(h_ref, f_ref, p_ref, gp_ref, wg_ref, wp_ref, gf_ref, y_ref):
    h = h_ref[...] + f_ref[...]
    gate = jax.nn.sigmoid(_dot(_rms(h, gp_ref[...]).astype(BF16), wg_ref[...]))
    h = h + gate * _dot(p_ref[...].astype(BF16), wp_ref[...])
    y_ref[...] = _rms(h, gf_ref[...])


def _ple(h1, ffn, p2d, gp, wg, wp, gfin):
    t, d = h1.shape
    pd = p2d.shape[1]
    tm = min(TM_PLE, t)
    vec = lambda: pl.BlockSpec((1, d), lambda i: (0, 0))
    tok = lambda: pl.BlockSpec((tm, d), lambda i: (i, 0))
    return pl.pallas_call(
        _ple_kernel,
        grid=(t // tm,),
        in_specs=[tok(), tok(), pl.BlockSpec((tm, pd), lambda i: (i, 0)),
                  vec(), pl.BlockSpec((d, d), lambda i: (0, 0)), pl.BlockSpec((pd, d), lambda i: (0, 0)), vec()],
        out_specs=tok(),
        out_shape=jax.ShapeDtypeStruct((t, d), F32),
        compiler_params=_cparams(("parallel",)),
        name="ple_final",
    )(h1, ffn, p2d, gp, wg, wp, gfin)


def _pad_lanes(v):
    return jnp.pad(v, (0, LANES - v.shape[0]))


def _prepare(norm_mix, w_in, ssd_conv_w, ssd_conv_b, ssd_a_log, ssd_dt_bias, ssd_d, ssd_norm, w_ssd_out,
             gdn_conv_w, gdn_a_log, gdn_dt_bias, gdn_norm, w_gdn_out, w_out, norm_ffn, peer_query, peer_keys,
             expert_u, expert_v, norm_ple, w_ple_gate, w_ple_proj, norm_final):
    w = w_in[0]
    xbc = SSD_INNER + 2 * SSD_GROUPS * SSD_STATE
    c_dt = SSD_INNER + xbc
    c_qkv = c_dt + 2 * SSD_HEADS
    c_a = c_qkv + 4 * GDN_INNER
    c_ga = c_a + 4 * GDN_HEADS
    w_main = jnp.concatenate([w[:, :c_dt], w[:, c_qkv:c_a], w[:, c_ga:]], axis=1).astype(BF16)
    w_small = jnp.concatenate([w[:, c_dt:c_qkv], w[:, c_a:c_ga]], axis=1)
    w_small = jnp.pad(w_small, ((0, 0), (0, LANES - w_small.shape[1]))).astype(BF16)
    hp = jnp.stack([
        _pad_lanes(jnp.concatenate([ssd_dt_bias[0].reshape(-1), gdn_dt_bias[0].reshape(-1)])),
        _pad_lanes(jnp.concatenate([ssd_a_log[0].reshape(-1), gdn_a_log[0].reshape(-1)])),
        _pad_lanes(ssd_d[0]),
    ])
    hp = jnp.pad(hp, ((0, SUBLANES - hp.shape[0]), (0, 0)))
    row = lambda v: v.reshape(1, -1)
    return dict(
        g_mix=row(norm_mix[0]), w_main=w_main, w_small=w_small, hp=hp,
        ssd_conv_w=ssd_conv_w[0], ssd_conv_b=row(ssd_conv_b[0]), ssd_norm=row(ssd_norm[0]),
        gdn_conv_w=gdn_conv_w[0], gdn_norm=row(gdn_norm[0]),
        wa=w_ssd_out[0].astype(BF16), wb=w_gdn_out[0].astype(BF16), wo=w_out[0].astype(BF16),
        g_ffn=row(norm_ffn[0]), wq=peer_query[0].astype(BF16),
        keys=peer_keys[0].reshape(2 * PEER_HEADS, PEER_KEYS, PEER_HALF).astype(BF16),
        eu=expert_u[0].astype(BF16), ev=expert_v[0].astype(BF16),
        g_ple=row(norm_ple[0]), wg=w_ple_gate[0].astype(BF16), wp=w_ple_proj[0].astype(BF16),
        g_fin=row(norm_final),
    )


def _trunk(x, ple, p):
    b, length, d = x.shape
    t = b * length
    x2d = x.reshape(t, d)
    proj, small = _in_proj(x2d, p["g_mix"], p["w_main"], p["w_small"])
    proj3 = proj.reshape(b, length, -1)
    small3 = small.reshape(b, length, LANES)
    ya = _ssd_mixer(proj3, small3, p["ssd_conv_w"], p["ssd_conv_b"], p["hp"], p["ssd_norm"])
    yb = _gdn_mixer(proj3, small3, p["gdn_conv_w"], p["hp"], p["gdn_norm"])
    h1, n2 = _merge(x2d, ya.reshape(t, -1), yb.reshape(t, -1), proj, p["wa"], p["wb"], p["wo"], p["g_ffn"])
    e, gate = _route(n2, p["wq"], p["keys"])
    ffn = _experts(n2, e, gate, p["eu"], p["ev"])
    y = _ple(h1, ffn, ple.reshape(t, -1), p["g_ple"], p["wg"], p["wp"], p["g_fin"])
    return y.reshape(b, length, d)


def kernel(x_prompt, x_sample, p_prompt, p_sample, norm_mix, w_in, ssd_conv_w, ssd_conv_b, ssd_a_log, ssd_dt_bias, ssd_d, ssd_norm, w_ssd_out, gdn_conv_w, gdn_a_log, gdn_dt_bias, gdn_norm, w_gdn_out, w_out, norm_ffn, peer_query, peer_keys, expert_u, expert_v, norm_ple, w_ple_gate, w_ple_proj, norm_final):
    assert w_in.shape[0] == 1, "single-layer trunk"
    p = _prepare(norm_mix, w_in, ssd_conv_w, ssd_conv_b, ssd_a_log, ssd_dt_bias, ssd_d, ssd_norm, w_ssd_out,
                 gdn_conv_w, gdn_a_log, gdn_dt_bias, gdn_norm, w_gdn_out, w_out, norm_ffn, peer_query, peer_keys,
                 expert_u, expert_v, norm_ple, w_ple_gate, w_ple_proj, norm_final)
    return (_trunk(x_prompt, p_prompt[0], p), _trunk(x_sample, p_sample[0], p))
```

```python
import jax
import jax.numpy as jnp
from jax import lax
from jax.experimental import pallas as pl
from jax.experimental.pallas import tpu as pltpu

F32 = jnp.float32
BF16 = jnp.bfloat16
I32 = jnp.int32

EPS = 1e-6
CONV_K = 4
SSD_HEADS = 16
SSD_HEAD_DIM = 64
SSD_GROUPS = 4
SSD_STATE = 128
SSD_INNER = SSD_HEADS * SSD_HEAD_DIM
GDN_HEADS = 8
GDN_HEAD_DIM = 128
GDN_INNER = GDN_HEADS * GDN_HEAD_DIM
PEER_HEADS = 8
PEER_KEYS = 128
PEER_TOPK = 16
PEER_HALF = 128

LANES = 128
SUBLANES = 8
SSD_CHUNK = 128
GDN_CHUNK = 128
GDN_PREP_CHUNKS = 8
GDN_INV_BLOCK = 8
VMEM_LIMIT = 56 * 1024 * 1024

TM_PROJ = 1024
TN_PROJ = 1024
TM_MERGE = 256
TM_EXP = 512
EXP_ROWS = 4
G_PITCH = TM_EXP + 8
TOKEN_UNROLL = 16
TM_PLE = 512

NEG = -1e30


def _dot(a, b):
    return jnp.dot(a, b, preferred_element_type=F32)


def _dot_nt(a, b):
    return lax.dot_general(a, b, (((1,), (1,)), ((), ())), preferred_element_type=F32)


def _dot_tn(a, b):
    return lax.dot_general(a, b, (((0,), (0,)), ((), ())), preferred_element_type=F32)


def _bf16_terms(m, terms):
    out = []
    for _ in range(terms - 1):
        t = m.astype(BF16)
        out.append(t)
        m = m - t.astype(F32)
    out.append(m.astype(BF16))
    return out


def _dot_exact_lhs(lb, m, terms=3):
    n = m.shape[1]
    r = _dot(lb, jnp.concatenate(_bf16_terms(m, terms), axis=1))
    return sum(r[:, i * n:(i + 1) * n] for i in range(1, terms)) + r[:, :n]


def _dot_exact_rhs(m, eb, terms=2):
    return _dot(jnp.concatenate(_bf16_terms(m, terms), axis=1), jnp.concatenate([eb] * terms, axis=0))


def _rows_transposed(sel, m, terms=3):
    n = m.shape[0]
    r = _dot_nt(sel, jnp.concatenate(_bf16_terms(m, terms), axis=0))
    return sum(r[:, i * n:(i + 1) * n] for i in range(1, terms)) + r[:, :n]


def _softplus(x):
    return jnp.maximum(x, 0.0) + jnp.log1p(jnp.exp(-jnp.abs(x)))


def _silu(x):
    return x * jax.nn.sigmoid(x)


def _rms(x, g):
    return x * lax.rsqrt(jnp.mean(x * x, axis=-1, keepdims=True) + EPS) * g


def _cparams(sem):
    return pltpu.CompilerParams(dimension_semantics=sem, vmem_limit_bytes=VMEM_LIMIT)


def _in_proj_kernel(x_ref, g_ref, wm_ref, ws_ref, main_ref, small_ref, n_scr):
    @pl.when(pl.program_id(1) == 0)
    def _():
        n = _rms(x_ref[...], g_ref[...]).astype(BF16)
        n_scr[...] = n
        small_ref[...] = _dot(n, ws_ref[...])

    main_ref[...] = _dot(n_scr[...], wm_ref[...])


def _in_proj(x2d, g, w_main, w_small):
    t, d = x2d.shape
    n_main = w_main.shape[1]
    tm = min(TM_PROJ, t)
    return pl.pallas_call(
        _in_proj_kernel,
        grid=(t // tm, n_main // TN_PROJ),
        in_specs=[
            pl.BlockSpec((tm, d), lambda i, j: (i, 0)),
            pl.BlockSpec((1, d), lambda i, j: (0, 0)),
            pl.BlockSpec((d, TN_PROJ), lambda i, j: (0, j)),
            pl.BlockSpec((d, LANES), lambda i, j: (0, 0)),
        ],
        out_specs=[
            pl.BlockSpec((tm, TN_PROJ), lambda i, j: (i, j)),
            pl.BlockSpec((tm, LANES), lambda i, j: (i, 0)),
        ],
        out_shape=[jax.ShapeDtypeStruct((t, n_main), F32), jax.ShapeDtypeStruct((t, LANES), F32)],
        scratch_shapes=[pltpu.VMEM((tm, d), BF16)],
        compiler_params=_cparams(("parallel", "arbitrary")),
        name="in_proj",
    )(x2d, g, w_main, w_small)


def _conv_silu_chunk(src_ref, w_ref, b_ref, c, nch, rows):
    length = src_ref.shape[1]
    base = pl.multiple_of(c * rows, rows)
    main = src_ref[0, pl.ds(base, rows), :]
    prev = src_ref[0, pl.ds(pl.multiple_of(jnp.maximum(base - SUBLANES, 0), SUBLANES), SUBLANES), :]
    nxt = src_ref[0, pl.ds(pl.multiple_of(jnp.minimum(base + rows, length - SUBLANES), SUBLANES), SUBLANES), :]
    prev = jnp.where(c > 0, prev, 0.0)
    nxt = jnp.where(c < nch - 1, nxt, 0.0)
    ext = jnp.concatenate([prev, main, nxt], axis=0)
    w = w_ref[...]
    lo = SUBLANES - CONV_K // 2
    y = ext[lo:lo + rows] * w[0:1]
    for j in range(1, CONV_K):
        y = y + ext[lo + j:lo + j + rows] * w[j:j + 1]
    if b_ref is not None:
        y = y + b_ref[...]
    return _silu(y)


def _expand_heads(t, lane0, width):
    r = t.shape[0]
    hid = lax.broadcasted_iota(I32, (r, 4 * width), 1) // width
    out = jnp.broadcast_to(t[:, lane0 + 3:lane0 + 4], (r, 4 * width))
    for j in (2, 1, 0):
        out = jnp.where(hid == j, t[:, lane0 + j:lane0 + j + 1], out)
    return out


def _ssd_kernel(z_ref, xs_ref, bm_ref, cm_ref, sm_ref, cwx_ref, cwb_ref, cwc_ref, cbx_ref, cbb_ref, cbc_ref,
                hp_ref, nw_ref, out_ref, xc_scr, bc_scr, cc_scr, yf_scr, yb_scr, stf_scr, stb_scr):
    ch = SSD_CHUNK
    length = z_ref.shape[1]
    nch = length // ch
    gwid = 4 * SSD_HEAD_DIM
    grp = pl.program_id(1)
    shift = lax.rem(LANES - 4 * grp, LANES)
    hp = pltpu.roll(hp_ref[...], shift, 1)
    bias_row = hp[0:1, :]
    a_row = -jnp.exp(hp[1:2, :])
    dsk_x = _expand_heads(hp[2:3, :], 0, SSD_HEAD_DIM)

    row = lax.broadcasted_iota(I32, (ch, ch), 0)
    col = lax.broadcasted_iota(I32, (ch, ch), 1)
    hid = lax.broadcasted_iota(I32, (ch, gwid), 1) // SSD_HEAD_DIM

    def conv_body(c, carry):
        base = pl.multiple_of(c * ch, ch)
        xc_scr[pl.ds(base, ch), :] = _conv_silu_chunk(xs_ref, cwx_ref, cbx_ref, c, nch, ch)
        bc_scr[pl.ds(base, ch), :] = _conv_silu_chunk(bm_ref, cwb_ref, cbb_ref, c, nch, ch)
        cc_scr[pl.ds(base, ch), :] = _conv_silu_chunk(cm_ref, cwc_ref, cbc_ref, c, nch, ch)
        return carry

    lax.fori_loop(0, nch, conv_body, 0)

    masks = ((row >= col), (row <= col))
    lmats = tuple(jnp.where(m, 1.0, 0.0).astype(BF16) for m in masks)
    lanes0 = (0, SSD_HEADS)
    tot_rows = (ch - 1, 0)
    erow = lax.broadcasted_iota(I32, (LANES, gwid), 0)
    ecol = lax.broadcasted_iota(I32, (LANES, gwid), 1) // SSD_HEAD_DIM
    emats = tuple(jnp.where(erow == ecol + l0, 1.0, 0.0).astype(BF16) for l0 in lanes0)
    srow = lax.broadcasted_iota(I32, (SUBLANES, LANES), 0)
    scol = lax.broadcasted_iota(I32, (SUBLANES, LANES), 1)
    sels = tuple(jnp.where(scol == srow + l0, 1.0, 0.0).astype(BF16) for l0 in lanes0)
    st_refs = (stf_scr, stb_scr)
    y_refs = (yf_scr, yb_scr)
    dirs = (0, 1)

    stf_scr[...] = jnp.zeros_like(stf_scr)
    stb_scr[...] = jnp.zeros_like(stb_scr)

    def scan_body(ci, carry):
        bases = (pl.multiple_of(ci * ch, ch), pl.multiple_of((nch - 1 - ci) * ch, ch))
        x = [xc_scr[pl.ds(b, ch), :] for b in bases]
        bb = [bc_scr[pl.ds(b, ch), :] for b in bases]
        cb = [cc_scr[pl.ds(b, ch), :].astype(BF16) for b in bases]
        dt_all = [_softplus(pltpu.roll(sm_ref[0, pl.ds(b, ch), :], shift, 1) + bias_row) for b in bases]
        acum = [_dot_exact_lhs(lmats[d], dt_all[d] * a_row) for d in dirs]
        sc = [jnp.where(masks[d], _dot_nt(cb[d], bb[d].astype(BF16)), 0.0) for d in dirs]
        st = [st_refs[d][...] for d in dirs]
        y_off = [_dot(cb[d], st[d].astype(BF16)) for d in dirs]
        dt_x = [_dot_exact_rhs(dt_all[d], emats[d]) for d in dirs]
        acum_t = [_rows_transposed(sels[d], acum[d]) for d in dirs]
        tot = [acum[d][tot_rows[d]:tot_rows[d] + 1, :] for d in dirs]
        decay_x = [_dot_exact_rhs(jnp.concatenate([jnp.exp(acum[d]), jnp.exp(tot[d] - acum[d])], axis=0), emats[d])
                   for d in dirs]
        xd = [x[d] * dt_x[d] for d in dirs]
        lhs, rhs = [], []
        for d in dirs:
            mixes, parts = [], []
            for j in range(4):
                ln = lanes0[d] + j
                dd = acum[d][:, ln:ln + 1] - acum_t[d][j:j + 1, :]
                mixes.append((sc[d] * jnp.exp(jnp.where(masks[d], dd, NEG))).astype(BF16))
                parts.append(jnp.where(hid == j, xd[d], 0.0).astype(BF16))
            lhs.append(jnp.concatenate(mixes, axis=1))
            rhs.append(jnp.concatenate(parts, axis=0))
        y = [_dot(lhs[d], rhs[d]) for d in dirs]
        upd = [_dot_tn(bb[d].astype(BF16), (xd[d] * decay_x[d][ch:]).astype(BF16)) for d in dirs]
        for d in dirs:
            ea_x = decay_x[d][:ch]
            etot_x = ea_x[tot_rows[d]:tot_rows[d] + 1, :]
            y_refs[d][pl.ds(bases[d], ch), :] = y[d] + y_off[d] * ea_x
            st_refs[d][...] = st[d] * etot_x + upd[d]
        return carry

    lax.fori_loop(0, nch, scan_body, 0)

    def fin_body(c, carry):
        base = pl.multiple_of(c * ch, ch)
        y = yf_scr[pl.ds(base, ch), :] + yb_scr[pl.ds(base, ch), :] + xc_scr[pl.ds(base, ch), :] * dsk_x
        y = y * _silu(z_ref[0, pl.ds(base, ch), :])
        out_ref[0, pl.ds(base, ch), :] = _rms(y, nw_ref[...])
        return carry

    lax.fori_loop(0, nch, fin_body, 0)


def _ssd_mixer(proj3, small3, conv_w, conv_b, hp, norm_w):
    b, length, _ = proj3.shape
    gw = 4 * SSD_HEAD_DIM
    xs0 = SSD_INNER // gw
    b0 = 2 * SSD_INNER // SSD_STATE
    c0 = b0 + SSD_GROUPS
    nb_x = SSD_INNER // SSD_STATE
    seq = lambda w, off: pl.BlockSpec((1, length, w), lambda i, g: (i, 0, off + g))
    par = lambda r, w, off: pl.BlockSpec((r, w), lambda i, g: (0, off + g))
    return pl.pallas_call(
        _ssd_kernel,
        grid=(b, SSD_GROUPS),
        in_specs=[
            seq(gw, 0), seq(gw, xs0), seq(SSD_STATE, b0), seq(SSD_STATE, c0),
            pl.BlockSpec((1, length, LANES), lambda i, g: (i, 0, 0)),
            par(CONV_K, gw, 0), par(CONV_K, SSD_STATE, nb_x), par(CONV_K, SSD_STATE, nb_x + SSD_GROUPS),
            par(1, gw, 0), par(1, SSD_STATE, nb_x), par(1, SSD_STATE, nb_x + SSD_GROUPS),
            pl.BlockSpec((SUBLANES, LANES), lambda i, g: (0, 0)),
            par(1, gw, 0),
        ],
        out_specs=pl.BlockSpec((1, length, gw), lambda i, g: (i, 0, g)),
        out_shape=jax.ShapeDtypeStruct((b, length, SSD_INNER), F32),
        scratch_shapes=[
            pltpu.VMEM((length, gw), F32), pltpu.VMEM((length, SSD_STATE), F32),
            pltpu.VMEM((length, SSD_STATE), F32), pltpu.VMEM((length, gw), F32), pltpu.VMEM((length, gw), F32),
            pltpu.VMEM((SSD_STATE, gw), F32), pltpu.VMEM((SSD_STATE, gw), F32),
        ],
        compiler_params=_cparams(("parallel", "arbitrary")),
        name="ssd_mixer",
    )(proj3, proj3, proj3, proj3, small3, conv_w, conv_w, conv_w, conv_b, conv_b, conv_b, hp, norm_w)


def _gdn_kernel(q_ref, k_ref, v_ref, z_ref, sm_ref, cwq_ref, cwk_ref, cwv_ref, hp_ref, nw_ref, out_ref,
                qn_scr, kn_scr, vn_scr, u_scr, wq_scr, qk_scr, kd_scr, et_scr, o_scr):
    ch = GDN_CHUNK
    length = z_ref.shape[1]
    nch = length // ch
    dk = GDN_HEAD_DIM
    head = pl.program_id(1)
    shift = lax.rem(LANES - head, LANES)
    lane_a = 2 * SSD_HEADS
    lane_beta = lane_a + 2 * GDN_HEADS
    hp = pltpu.roll(hp_ref[...], shift, 1)
    bias_row = hp[0:1, :]
    acoef_row = jnp.exp(hp[1:2, :])

    row = lax.broadcasted_iota(I32, (ch, ch), 0)
    col = lax.broadcasted_iota(I32, (ch, ch), 1)
    eye = jnp.where(row == col, 1.0, 0.0)
    n_double = (GDN_INV_BLOCK - 1).bit_length() - 1
    levels = []
    size = GDN_INV_BLOCK
    inside = (row // size) == (col // size)
    levels.append(inside)
    while size < ch:
        size *= 2
        merged = (row // size) == (col // size)
        levels.append(merged & jnp.logical_not(inside))
        inside = merged

    def conv_body(c, carry):
        base = pl.multiple_of(c * ch, ch)
        q = _conv_silu_chunk(q_ref, cwq_ref, None, c, nch, ch)
        k = _conv_silu_chunk(k_ref, cwk_ref, None, c, nch, ch)
        qn_scr[pl.ds(base, ch), :] = q * (lax.rsqrt(jnp.sum(q * q, axis=-1, keepdims=True) + EPS) * dk ** -0.5)
        kn_scr[pl.ds(base, ch), :] = k * lax.rsqrt(jnp.sum(k * k, axis=-1, keepdims=True) + EPS)
        vn_scr[pl.ds(base, ch), :] = _conv_silu_chunk(v_ref, cwv_ref, None, c, nch, ch)
        return carry

    lax.fori_loop(0, nch, conv_body, 0)

    incl = ((row >= col), (row <= col))
    strict = ((row > col), (row < col))
    lmat = tuple(jnp.where(m, 1.0, 0.0).astype(BF16) for m in incl)
    umat = tuple(jnp.where(m, 1.0, 0.0) for m in strict)
    jref = (0, ch - 1)
    tot_row = (ch - 1, 0)
    group = min(GDN_PREP_CHUNKS, nch)

    def prep_body(gi, carry):
        chains = []
        for ci in range(group):
            c = gi * group + ci
            base = pl.multiple_of(c * ch, ch)
            q = qn_scr[pl.ds(base, ch), :]
            k = kn_scr[pl.ds(base, ch), :]
            v = vn_scr[pl.ds(base, ch), :]
            sm = pltpu.roll(sm_ref[0, pl.ds(base, ch), :], shift, 1)
            g_all = -acoef_row * _softplus(sm + bias_row)
            beta_all = jax.nn.sigmoid(sm)
            kbf = k.astype(BF16)
            kq = _dot_nt(jnp.concatenate([kbf, q.astype(BF16)], axis=0), kbf)
            for di in (0, 1):
                ln = lane_a + di * GDN_HEADS
                lb = lane_beta + di * GDN_HEADS
                chains.append(dict(c=c, di=di, q=q, k=k, v=v, kq=kq, g=g_all[:, ln:ln + 1],
                                   beta=beta_all[:, lb:lb + 1]))
        for t in chains:
            t["dmat"] = _dot_exact_lhs(lmat[t["di"]], t["g"] * umat[t["di"]], terms=2)
        for t in chains:
            di = t["di"]
            t["kb"] = t["k"] * t["beta"]
            dec = t["dmat"][:, jref[di]:jref[di] + 1] + t["g"][jref[di]:jref[di] + 1, :]
            t["dec"] = dec
            t["tot"] = dec[tot_row[di]:tot_row[di] + 1, :]
            gam = jnp.exp(jnp.where(incl[di], t["dmat"], NEG))
            t["qk"] = (t["kq"][ch:] * gam).astype(BF16)
            a = jnp.where(strict[di], t["kq"][:ch] * (gam * t["beta"]), 0.0)
            ad = jnp.where(levels[0], a, 0.0)
            t["p"] = eye - ad
            t["ab"] = ad.astype(BF16)
            t["off"] = [jnp.where(m, a, 0.0).astype(BF16) for m in levels[1:]]
        for t in chains:
            t["x"] = _dot(t["ab"], t["ab"])
        for i in range(n_double):
            last = i == n_double - 1
            for t in chains:
                xb = t["x"].astype(BF16)
                pb = t["p"].astype(BF16)
                t["r"] = _dot(xb, pb) if last else _dot(xb, jnp.concatenate([pb, xb], axis=1))
            for t in chains:
                t["p"] = t["p"] + t["r"][:, :ch]
                if not last:
                    t["x"] = t["r"][:, ch:]
        for lv in range(len(levels) - 1):
            for t in chains:
                t["m"] = _dot(t["off"][lv], t["p"].astype(BF16))
            for t in chains:
                t["p"] = t["p"] - _dot(t["p"].astype(BF16), t["m"].astype(BF16))
        for t in chains:
            t["edec"] = jnp.exp(t["dec"])
            rhs = jnp.concatenate([t["v"] * t["beta"], t["kb"] * t["edec"]], axis=1).astype(BF16)
            t["uw"] = _dot(t["p"].astype(BF16), rhs)
        for t in chains:
            c, di = t["c"], t["di"]
            base = pl.multiple_of(c * ch, ch)
            base2 = pl.multiple_of(c * 2 * ch, 2 * ch)
            u_scr[di, pl.ds(base, ch), :] = t["uw"][:, :dk]
            wq_scr[di, pl.ds(base2, 2 * ch), :] = jnp.concatenate(
                [t["uw"][:, dk:], t["q"] * t["edec"]], axis=0).astype(BF16)
            qk_scr[di, pl.ds(base, ch), :] = t["qk"]
            kd_scr[di, pl.ds(base, ch), :] = (t["k"] * jnp.exp(t["tot"] - t["dec"])).astype(BF16)
            et_scr[di, pl.ds(pl.multiple_of(c * SUBLANES, SUBLANES), SUBLANES), :] = jnp.broadcast_to(
                jnp.exp(t["tot"]), (SUBLANES, LANES))
        return carry

    lax.fori_loop(0, nch // group, prep_body, 0)

    def rec_body(ci, carry):
        cs = (ci, nch - 1 - ci)
        bases = [pl.multiple_of(c * ch, ch) for c in cs]
        ws = [_dot(wq_scr[di, pl.ds(pl.multiple_of(cs[di] * 2 * ch, 2 * ch), 2 * ch), :], carry[di].astype(BF16))
              for di in (0, 1)]
        v_new = [(u_scr[di, pl.ds(bases[di], ch), :] - ws[di][:ch]).astype(BF16) for di in (0, 1)]
        upd = [_dot_tn(kd_scr[di, pl.ds(bases[di], ch), :], v_new[di]) for di in (0, 1)]
        intra = [_dot(qk_scr[di, pl.ds(bases[di], ch), :], v_new[di]) for di in (0, 1)]
        s_new = []
        for di in (0, 1):
            et = et_scr[di, pl.ds(pl.multiple_of(cs[di] * SUBLANES, SUBLANES), 1), :]
            s_new.append(carry[di] * et + upd[di])
            o_scr[di, pl.ds(bases[di], ch), :] = ws[di][ch:] + intra[di]
        return tuple(s_new)

    zero = jnp.zeros((dk, dk), F32)
    lax.fori_loop(0, nch, rec_body, (zero, zero))

    def fin_body(c, carry):
        base = pl.multiple_of(c * ch, ch)
        o = o_scr[0, pl.ds(base, ch), :] + o_scr[1, pl.ds(base, ch), :]
        out_ref[0, pl.ds(base, ch), :] = _rms(o, nw_ref[...]) * _silu(z_ref[0, pl.ds(base, ch), :])
        return carry

    lax.fori_loop(0, nch, fin_body, 0)


def _gdn_mixer(proj3, small3, conv_w, hp, norm_w):
    b, length, _ = proj3.shape
    dk = GDN_HEAD_DIM
    nch = length // GDN_CHUNK
    q0 = (2 * SSD_INNER + 2 * SSD_GROUPS * SSD_STATE) // dk
    seq = lambda off: pl.BlockSpec((1, length, dk), lambda i, h: (i, 0, off + h))
    par = lambda off: pl.BlockSpec((CONV_K, dk), lambda i, h: (0, off + h))
    return pl.pallas_call(
        _gdn_kernel,
        grid=(b, GDN_HEADS),
        in_specs=[
            seq(q0), seq(q0 + GDN_HEADS), seq(q0 + 2 * GDN_HEADS), seq(q0 + 3 * GDN_HEADS),
            pl.BlockSpec((1, length, LANES), lambda i, h: (i, 0, 0)),
            par(0), par(GDN_HEADS), par(2 * GDN_HEADS),
            pl.BlockSpec((SUBLANES, LANES), lambda i, h: (0, 0)),
            pl.BlockSpec((1, dk), lambda i, h: (0, 0)),
        ],
        out_specs=pl.BlockSpec((1, length, dk), lambda i, h: (i, 0, h)),
        out_shape=jax.ShapeDtypeStruct((b, length, GDN_INNER), F32),
        scratch_shapes=[
            pltpu.VMEM((length, dk), F32), pltpu.VMEM((length, dk), F32), pltpu.VMEM((length, dk), F32),
            pltpu.VMEM((2, length, dk), F32),
            pltpu.VMEM((2, 2 * length, dk), BF16),
            pltpu.VMEM((2, length, GDN_CHUNK), BF16),
            pltpu.VMEM((2, length, dk), BF16),
            pltpu.VMEM((2, nch * SUBLANES, LANES), F32),
            pltpu.VMEM((2, length, dk), F32),
        ],
        compiler_params=_cparams(("parallel", "arbitrary")),
        name="gdn_mixer",
    )(proj3, proj3, proj3, proj3, small3, conv_w, conv_w, conv_w, hp, norm_w)


def _merge_kernel(x_ref, ya_ref, yb_ref, ga_ref, gb_ref, wa_ref, wb_ref, wo_ref, gf_ref, h_ref, n_ref):
    ya = _dot(ya_ref[...].astype(BF16), wa_ref[...])
    yb = _dot(yb_ref[...].astype(BF16), wb_ref[...])
    merged = jax.nn.sigmoid(ga_ref[...]) * ya + jax.nn.sigmoid(gb_ref[...]) * yb
    h = x_ref[...] + _dot(merged.astype(BF16), wo_ref[...])
    h_ref[...] = h
    n_ref[...] = _rms(h, gf_ref[...]).astype(BF16)


def _merge(x2d, ya, yb, proj2d, wa, wb, wo, gf):
    t, d = x2d.shape
    tm = min(TM_MERGE, t)
    ga0 = (proj2d.shape[1] - 2 * d) // d
    tok = lambda: pl.BlockSpec((tm, d), lambda i: (i, 0))
    wsp = lambda: pl.BlockSpec((d, d), lambda i: (0, 0))
    return pl.pallas_call(
        _merge_kernel,
        grid=(t // tm,),
        in_specs=[tok(), tok(), tok(),
                  pl.BlockSpec((tm, d), lambda i: (i, ga0)), pl.BlockSpec((tm, d), lambda i: (i, ga0 + 1)),
                  wsp(), wsp(), wsp(), pl.BlockSpec((1, d), lambda i: (0, 0))],
        out_specs=[tok(), tok()],
        out_shape=[jax.ShapeDtypeStruct((t, d), F32), jax.ShapeDtypeStruct((t, d), BF16)],
        compiler_params=_cparams(("parallel",)),
        name="merge",
    )(x2d, ya, yb, proj2d, proj2d, wa, wb, wo, gf)


def _topk_store(problems, k):
    state = [p[0] for p in problems]
    iotas = [lax.broadcasted_iota(I32, s.shape, 0).astype(F32) for s in state]
    for i in range(k):
        ms = [jnp.max(s, axis=0, keepdims=True) for s in state]
        cands = [jnp.where(s == m, io, float(s.shape[0])) for s, m, io in zip(state, ms, iotas)]
        ixs = [jnp.min(c, axis=0, keepdims=True) for c in cands]
        hits = [c == ix for c, ix in zip(cands, ixs)]
        for (_, vals_ref, outs_ref, payload), m, ix, hit in zip(problems, ms, ixs, hits):
            vals_ref[i:i + 1, :] = m
            outs_ref[i:i + 1, :] = (ix if payload is None
                                    else jnp.max(jnp.where(hit, payload, -1.0), axis=0, keepdims=True))
        state = [jnp.where(hit, -jnp.inf, s) for s, hit in zip(state, hits)]


def _route_unit(tok_ref, wq_ref, keys_ref, e_scr, gt_scr, tv_scr, ti_scr, bv_scr, be_scr, head, tile):
    kk = PEER_TOPK
    half = kk // 2
    toks = tok_ref[pl.ds(pl.multiple_of(tile * LANES, LANES), LANES), :]
    qry = _dot(toks, wq_ref[...])
    st = [_dot_nt(keys_ref[0, z], qry[:, z * PEER_HALF:(z + 1) * PEER_HALF].astype(BF16)) for z in range(2)]
    _topk_store([(st[z], tv_scr.at[z], ti_scr.at[z], None) for z in range(2)], kk)
    s0, s1 = tv_scr[0], tv_scr[1]
    e0, i1 = ti_scr[0] * float(PEER_KEYS), ti_scr[1]
    cs = [s0[0:1] + s1] + [s0[i:i + 1] + s1[0:half] for i in range(1, half)] + [s0[half:] + s1[0:1]]
    ce = [e0[0:1] + i1] + [e0[i:i + 1] + i1[0:half] for i in range(1, half)] + [e0[half:] + i1[0:1]]
    _topk_store([(jnp.concatenate(cs, axis=0), bv_scr, be_scr, jnp.concatenate(ce, axis=0))], kk)
    best = bv_scr[...]
    ex = jnp.exp(best - best[0:1])
    rows = pl.ds(pl.multiple_of(head * kk, kk), kk)
    lanes = pl.ds(pl.multiple_of(tile * LANES, LANES), LANES)
    e_scr[rows, lanes] = be_scr[...]
    gt_scr[rows, lanes] = ex / jnp.sum(ex, axis=0, keepdims=True)


def _peer_kernel(nm_ref, nr_ref, wq_ref, keys_ref, u_ref, v_ref, out_ref,
                 g_scr, e_scr, gt_scr, et_scr, gtt_scr, tv_scr, ti_scr, bv_scr, be_scr):
    tm = nm_ref.shape[0]
    nk = PEER_KEYS
    ntile = tm // LANES
    tile_i = pl.program_id(0)
    step = pl.program_id(1)

    @pl.when((tile_i == 0) & (step == 0))
    def _():
        e_scr[...] = jnp.zeros_like(e_scr)
        gt_scr[...] = jnp.zeros_like(gt_scr)

    @pl.when(step == 0)
    def _():
        out_ref[...] = jnp.zeros_like(out_ref)
        et_scr[...] = e_scr[...].T.astype(I32)
        gtt_scr[...] = gt_scr[...].T
        sub = lax.broadcasted_iota(I32, (nk, LANES), 0)

        def tok_body(t, carry):
            re = et_scr[pl.ds(t, 1), :]
            rg = gtt_scr[pl.ds(t, 1), :]
            ghi = rg.astype(BF16).astype(F32)
            hit_i = sub == (re >> 7)
            at = jnp.concatenate([jnp.where(hit_i, ghi, 0.0).astype(BF16),
                                  jnp.where(hit_i, rg - ghi, 0.0).astype(BF16)], axis=1)
            bt1 = jnp.where(sub == (re & (nk - 1)), 1.0, 0.0).astype(BF16)
            bt = jnp.concatenate([bt1, bt1], axis=1)
            g_scr[pl.ds(t, nk, stride=G_PITCH), :] = _dot_nt(at, bt)
            return carry

        lax.fori_loop(0, tm, tok_body, 0, unroll=TOKEN_UNROLL)

    _route_unit(nr_ref, wq_ref, keys_ref, e_scr, gt_scr, tv_scr, ti_scr, bv_scr, be_scr,
                step // ntile, lax.rem(step, ntile))

    hid = _dot_nt(nm_ref[...], u_ref[...])
    act = 0.5 * hid * (1.0 + lax.erf(hid * (0.5 ** 0.5)))
    gates = [g_scr[pl.ds(pl.multiple_of((step * EXP_ROWS + r) * G_PITCH, SUBLANES), tm), :]
             for r in range(EXP_ROWS)]
    wts = (act * jnp.concatenate(gates, axis=1)).astype(BF16)
    out_ref[...] = out_ref[...] + _dot(wts, v_ref[...])


def _peer(n2, wq, keys, eu, ev):
    t, d = n2.shape
    tm = min(TM_EXP, t)
    assert tm == TM_EXP, "gate slab pitch is derived from TM_EXP"
    assert PEER_KEYS == LANES, "expert ids are decoded with a 7-bit shift"
    hk = PEER_HEADS * PEER_TOPK
    er = EXP_ROWS * PEER_KEYS
    nsteps = eu.shape[0] // er
    nt = t // tm
    assert nsteps == PEER_HEADS * (tm // LANES), "one routing unit per expert step"
    qw = 2 * PEER_HALF
    prev = lambda i, s: (jnp.maximum(i - 1, 0), 0)
    return pl.pallas_call(
        _peer_kernel,
        grid=(nt + 1, nsteps),
        in_specs=[pl.BlockSpec((tm, d), prev),
                  pl.BlockSpec((tm, d), lambda i, s: (jnp.minimum(i, nt - 1), 0)),
                  pl.BlockSpec((d, qw), lambda i, s: (0, s // (tm // LANES))),
                  pl.BlockSpec((1, 2, PEER_KEYS, PEER_HALF), lambda i, s: (s // (tm // LANES), 0, 0, 0)),
                  pl.BlockSpec((er, d), lambda i, s: (s, 0)),
                  pl.BlockSpec((er, d), lambda i, s: (s, 0))],
        out_specs=pl.BlockSpec((tm, d), prev),
        out_shape=jax.ShapeDtypeStruct((t, d), F32),
        scratch_shapes=[pltpu.VMEM((PEER_KEYS * G_PITCH, LANES), F32),
                        pltpu.VMEM((hk, tm), F32), pltpu.VMEM((hk, tm), F32),
                        pltpu.VMEM((tm, hk), I32), pltpu.VMEM((tm, hk), F32),
                        pltpu.VMEM((2, PEER_TOPK, LANES), F32), pltpu.VMEM((2, PEER_TOPK, LANES), F32),
                        pltpu.VMEM((PEER_TOPK, LANES), F32), pltpu.VMEM((PEER_TOPK, LANES), F32)],
        compiler_params=_cparams(("arbitrary", "arbitrary")),
        name="peer_ffn",
    )(n2, n2, wq, keys, eu, ev)


def _ple_kernel(h_ref, f_ref, p_ref, gp_ref, wg_ref, wp_ref, gf_ref, y_ref):
    h = h_ref[...] + f_ref[...]
    gate = jax.nn.sigmoid(_dot(_rms(h, gp_ref[...]).astype(BF16), wg_ref[...]))
    h = h + gate * _dot(p_ref[...].astype(BF16), wp_ref[...])
    y_ref[...] = _rms(h, gf_ref[...])


def _ple(h1, ffn, p2d, gp, wg, wp, gfin):
    t, d = h1.shape
    pd = p2d.shape[1]
    tm = min(TM_PLE, t)
    vec = lambda: pl.BlockSpec((1, d), lambda i: (0, 0))
    tok = lambda: pl.BlockSpec((tm, d), lambda i: (i, 0))
    return pl.pallas_call(
        _ple_kernel,
        grid=(t // tm,),
        in_specs=[tok(), tok(), pl.BlockSpec((tm, pd), lambda i: (i, 0)),
                  vec(), pl.BlockSpec((d, d), lambda i: (0, 0)), pl.BlockSpec((pd, d), lambda i: (0, 0)), vec()],
        out_specs=tok(),
        out_shape=jax.ShapeDtypeStruct((t, d), F32),
        compiler_params=_cparams(("parallel",)),
        name="ple_final",
    )(h1, ffn, p2d, gp, wg, wp, gfin)


def _pad_lanes(v):
    return jnp.pad(v, (0, LANES - v.shape[0]))


def _prepare(norm_mix, w_in, ssd_conv_w, ssd_conv_b, ssd_a_log, ssd_dt_bias, ssd_d, ssd_norm, w_ssd_out,
             gdn_conv_w, gdn_a_log, gdn_dt_bias, gdn_norm, w_gdn_out, w_out, norm_ffn, peer_query, peer_keys,
             expert_u, expert_v, norm_ple, w_ple_gate, w_ple_proj, norm_final):
    w = w_in[0]
    xbc = SSD_INNER + 2 * SSD_GROUPS * SSD_STATE
    c_dt = SSD_INNER + xbc
    c_qkv = c_dt + 2 * SSD_HEADS
    c_a = c_qkv + 4 * GDN_INNER
    c_ga = c_a + 4 * GDN_HEADS
    w_main = jnp.concatenate([w[:, :c_dt], w[:, c_qkv:c_a], w[:, c_ga:]], axis=1).astype(BF16)
    w_small = jnp.concatenate([w[:, c_dt:c_qkv], w[:, c_a:c_ga]], axis=1)
    w_small = jnp.pad(w_small, ((0, 0), (0, LANES - w_small.shape[1]))).astype(BF16)
    hp = jnp.stack([
        _pad_lanes(jnp.concatenate([ssd_dt_bias[0].reshape(-1), gdn_dt_bias[0].reshape(-1)])),
        _pad_lanes(jnp.concatenate([ssd_a_log[0].reshape(-1), gdn_a_log[0].reshape(-1)])),
        _pad_lanes(ssd_d[0]),
    ])
    hp = jnp.pad(hp, ((0, SUBLANES - hp.shape[0]), (0, 0)))
    row = lambda v: v.reshape(1, -1)
    return dict(
        g_mix=row(norm_mix[0]), w_main=w_main, w_small=w_small, hp=hp,
        ssd_conv_w=ssd_conv_w[0], ssd_conv_b=row(ssd_conv_b[0]), ssd_norm=row(ssd_norm[0]),
        gdn_conv_w=gdn_conv_w[0], gdn_norm=row(gdn_norm[0]),
        wa=w_ssd_out[0].astype(BF16), wb=w_gdn_out[0].astype(BF16), wo=w_out[0].astype(BF16),
        g_ffn=row(norm_ffn[0]), wq=peer_query[0].astype(BF16),
        keys=peer_keys[0].astype(BF16),
        eu=expert_u[0].astype(BF16), ev=expert_v[0].astype(BF16),
        g_ple=row(norm_ple[0]), wg=w_ple_gate[0].astype(BF16), wp=w_ple_proj[0].astype(BF16),
        g_fin=row(norm_final),
    )


def _trunk(x, ple, p):
    b, length, d = x.shape
    t = b * length
    x2d = x.reshape(t, d)
    proj, small = _in_proj(x2d, p["g_mix"], p["w_main"], p["w_small"])
    proj3 = proj.reshape(b, length, -1)
    small3 = small.reshape(b, length, LANES)
    ya = _ssd_mixer(proj3, small3, p["ssd_conv_w"], p["ssd_conv_b"], p["hp"], p["ssd_norm"])
    yb = _gdn_mixer(proj3, small3, p["gdn_conv_w"], p["hp"], p["gdn_norm"])
    h1, n2 = _merge(x2d, ya.reshape(t, -1), yb.reshape(t, -1), proj, p["wa"], p["wb"], p["wo"], p["g_ffn"])
    ffn = _peer(n2, p["wq"], p["keys"], p["eu"], p["ev"])
    y = _ple(h1, ffn, ple.reshape(t, -1), p["g_ple"], p["wg"], p["wp"], p["g_fin"])
    return y.reshape(b, length, d)


def kernel(x_prompt, x_sample, p_prompt, p_sample, norm_mix, w_in, ssd_conv_w, ssd_conv_b, ssd_a_log, ssd_dt_bias, ssd_d, ssd_norm, w_ssd_out, gdn_conv_w, gdn_a_log, gdn_dt_bias, gdn_norm, w_gdn_out, w_out, norm_ffn, peer_query, peer_keys, expert_u, expert_v, norm_ple, w_ple_gate, w_ple_proj, norm_final):
    assert w_in.shape[0] == 1, "single-layer trunk"
    p = _prepare(norm_mix, w_in, ssd_conv_w, ssd_conv_b, ssd_a_log, ssd_dt_bias, ssd_d, ssd_norm, w_ssd_out,
                 gdn_conv_w, gdn_a_log, gdn_dt_bias, gdn_norm, w_gdn_out, w_out, norm_ffn, peer_query, peer_keys,
                 expert_u, expert_v, norm_ple, w_ple_gate, w_ple_proj, norm_final)
    return (_trunk(x_prompt, p_prompt[0], p), _trunk(x_sample, p_sample[0], p))
```

```python
import jax
import jax.numpy as jnp
from jax import lax
from jax.experimental import pallas as pl
from jax.experimental.pallas import tpu as pltpu

F32 = jnp.float32
BF16 = jnp.bfloat16
I32 = jnp.int32

EPS = 1e-6
CONV_K = 4
SSD_HEADS = 16
SSD_HEAD_DIM = 64
SSD_GROUPS = 4
SSD_STATE = 128
SSD_INNER = SSD_HEADS * SSD_HEAD_DIM
GDN_HEADS = 8
GDN_HEAD_DIM = 128
GDN_INNER = GDN_HEADS * GDN_HEAD_DIM
PEER_HEADS = 8
PEER_KEYS = 128
PEER_TOPK = 16
PEER_HALF = 128

LANES = 128
SUBLANES = 8
SSD_CHUNK = 128
GDN_CHUNK = 128
GDN_PREP_CHUNKS = 8
GDN_INV_BLOCK = 8
VMEM_LIMIT = 56 * 1024 * 1024

TM_PROJ = 1024
TN_PROJ = 1024
TM_MERGE = 256
TM_EXP = 512
EXP_ROWS = 4
G_PITCH = TM_EXP + 8
TOKEN_UNROLL = 32
TM_PLE = 512

NEG = -1e30


def _dot(a, b):
    return jnp.dot(a, b, preferred_element_type=F32)


def _dot_nt(a, b):
    return lax.dot_general(a, b, (((1,), (1,)), ((), ())), preferred_element_type=F32)


def _dot_tn(a, b):
    return lax.dot_general(a, b, (((0,), (0,)), ((), ())), preferred_element_type=F32)


def _bf16_terms(m, terms):
    out = []
    for _ in range(terms - 1):
        t = m.astype(BF16)
        out.append(t)
        m = m - t.astype(F32)
    out.append(m.astype(BF16))
    return out


def _dot_exact_lhs(lb, m, terms=3):
    n = m.shape[1]
    r = _dot(lb, jnp.concatenate(_bf16_terms(m, terms), axis=1))
    return sum(r[:, i * n:(i + 1) * n] for i in range(1, terms)) + r[:, :n]


def _dot_exact_rhs(m, eb, terms=2):
    return _dot(jnp.concatenate(_bf16_terms(m, terms), axis=1), jnp.concatenate([eb] * terms, axis=0))


def _rows_transposed(sel, m, terms=3):
    n = m.shape[0]
    r = _dot_nt(sel, jnp.concatenate(_bf16_terms(m, terms), axis=0))
    return sum(r[:, i * n:(i + 1) * n] for i in range(1, terms)) + r[:, :n]


def _softplus(x):
    return jnp.maximum(x, 0.0) + jnp.log1p(jnp.exp(-jnp.abs(x)))


def _silu(x):
    return x * jax.nn.sigmoid(x)


def _rms(x, g):
    return x * lax.rsqrt(jnp.mean(x * x, axis=-1, keepdims=True) + EPS) * g


def _cparams(sem):
    return pltpu.CompilerParams(dimension_semantics=sem, vmem_limit_bytes=VMEM_LIMIT)


def _in_proj_kernel(x_ref, g_ref, wm_ref, ws_ref, main_ref, small_ref, n_scr):
    @pl.when(pl.program_id(1) == 0)
    def _():
        n = _rms(x_ref[...], g_ref[...]).astype(BF16)
        n_scr[...] = n
        small_ref[...] = _dot(n, ws_ref[...])

    main_ref[...] = _dot(n_scr[...], wm_ref[...])


def _in_proj(x2d, g, w_main, w_small):
    t, d = x2d.shape
    n_main = w_main.shape[1]
    tm = min(TM_PROJ, t)
    return pl.pallas_call(
        _in_proj_kernel,
        grid=(t // tm, n_main // TN_PROJ),
        in_specs=[
            pl.BlockSpec((tm, d), lambda i, j: (i, 0)),
            pl.BlockSpec((1, d), lambda i, j: (0, 0)),
            pl.BlockSpec((d, TN_PROJ), lambda i, j: (0, j)),
            pl.BlockSpec((d, LANES), lambda i, j: (0, 0)),
        ],
        out_specs=[
            pl.BlockSpec((tm, TN_PROJ), lambda i, j: (i, j)),
            pl.BlockSpec((tm, LANES), lambda i, j: (i, 0)),
        ],
        out_shape=[jax.ShapeDtypeStruct((t, n_main), F32), jax.ShapeDtypeStruct((t, LANES), F32)],
        scratch_shapes=[pltpu.VMEM((tm, d), BF16)],
        compiler_params=_cparams(("parallel", "arbitrary")),
        name="in_proj",
    )(x2d, g, w_main, w_small)


def _conv_silu_chunk(src_ref, w_ref, b_ref, c, nch, rows):
    length = src_ref.shape[1]
    base = pl.multiple_of(c * rows, rows)
    main = src_ref[0, pl.ds(base, rows), :]
    prev = src_ref[0, pl.ds(pl.multiple_of(jnp.maximum(base - SUBLANES, 0), SUBLANES), SUBLANES), :]
    nxt = src_ref[0, pl.ds(pl.multiple_of(jnp.minimum(base + rows, length - SUBLANES), SUBLANES), SUBLANES), :]
    prev = jnp.where(c > 0, prev, 0.0)
    nxt = jnp.where(c < nch - 1, nxt, 0.0)
    ext = jnp.concatenate([prev, main, nxt], axis=0)
    w = w_ref[...]
    lo = SUBLANES - CONV_K // 2
    y = ext[lo:lo + rows] * w[0:1]
    for j in range(1, CONV_K):
        y = y + ext[lo + j:lo + j + rows] * w[j:j + 1]
    if b_ref is not None:
        y = y + b_ref[...]
    return _silu(y)


def _expand_heads(t, lane0, width):
    r = t.shape[0]
    hid = lax.broadcasted_iota(I32, (r, 4 * width), 1) // width
    out = jnp.broadcast_to(t[:, lane0 + 3:lane0 + 4], (r, 4 * width))
    for j in (2, 1, 0):
        out = jnp.where(hid == j, t[:, lane0 + j:lane0 + j + 1], out)
    return out


def _ssd_kernel(z_ref, xs_ref, bm_ref, cm_ref, sm_ref, cwx_ref, cwb_ref, cwc_ref, cbx_ref, cbb_ref, cbc_ref,
                hp_ref, nw_ref, out_ref, xc_scr, bc_scr, cc_scr, yf_scr, yb_scr, stf_scr, stb_scr):
    ch = SSD_CHUNK
    length = z_ref.shape[1]
    nch = length // ch
    gwid = 4 * SSD_HEAD_DIM
    grp = pl.program_id(1)
    shift = lax.rem(LANES - 4 * grp, LANES)
    hp = pltpu.roll(hp_ref[...], shift, 1)
    bias_row = hp[0:1, :]
    a_row = -jnp.exp(hp[1:2, :])
    dsk_x = _expand_heads(hp[2:3, :], 0, SSD_HEAD_DIM)

    row = lax.broadcasted_iota(I32, (ch, ch), 0)
    col = lax.broadcasted_iota(I32, (ch, ch), 1)
    hid = lax.broadcasted_iota(I32, (ch, gwid), 1) // SSD_HEAD_DIM

    def conv_body(c, carry):
        base = pl.multiple_of(c * ch, ch)
        xc_scr[pl.ds(base, ch), :] = _conv_silu_chunk(xs_ref, cwx_ref, cbx_ref, c, nch, ch)
        bc_scr[pl.ds(base, ch), :] = _conv_silu_chunk(bm_ref, cwb_ref, cbb_ref, c, nch, ch)
        cc_scr[pl.ds(base, ch), :] = _conv_silu_chunk(cm_ref, cwc_ref, cbc_ref, c, nch, ch)
        return carry

    lax.fori_loop(0, nch, conv_body, 0)

    masks = ((row >= col), (row <= col))
    lmats = tuple(jnp.where(m, 1.0, 0.0).astype(BF16) for m in masks)
    lanes0 = (0, SSD_HEADS)
    tot_rows = (ch - 1, 0)
    erow = lax.broadcasted_iota(I32, (LANES, gwid), 0)
    ecol = lax.broadcasted_iota(I32, (LANES, gwid), 1) // SSD_HEAD_DIM
    emats = tuple(jnp.where(erow == ecol + l0, 1.0, 0.0).astype(BF16) for l0 in lanes0)
    srow = lax.broadcasted_iota(I32, (SUBLANES, LANES), 0)
    scol = lax.broadcasted_iota(I32, (SUBLANES, LANES), 1)
    sels = tuple(jnp.where(scol == srow + l0, 1.0, 0.0).astype(BF16) for l0 in lanes0)
    st_refs = (stf_scr, stb_scr)
    y_refs = (yf_scr, yb_scr)
    dirs = (0, 1)

    stf_scr[...] = jnp.zeros_like(stf_scr)
    stb_scr[...] = jnp.zeros_like(stb_scr)

    def scan_body(ci, carry):
        bases = (pl.multiple_of(ci * ch, ch), pl.multiple_of((nch - 1 - ci) * ch, ch))
        x = [xc_scr[pl.ds(b, ch), :] for b in bases]
        bb = [bc_scr[pl.ds(b, ch), :] for b in bases]
        cb = [cc_scr[pl.ds(b, ch), :].astype(BF16) for b in bases]
        dt_all = [_softplus(pltpu.roll(sm_ref[0, pl.ds(b, ch), :], shift, 1) + bias_row) for b in bases]
        acum = [_dot_exact_lhs(lmats[d], dt_all[d] * a_row) for d in dirs]
        sc = [jnp.where(masks[d], _dot_nt(cb[d], bb[d].astype(BF16)), 0.0) for d in dirs]
        st = [st_refs[d][...] for d in dirs]
        y_off = [_dot(cb[d], st[d].astype(BF16)) for d in dirs]
        dt_x = [_dot_exact_rhs(dt_all[d], emats[d]) for d in dirs]
        acum_t = [_rows_transposed(sels[d], acum[d]) for d in dirs]
        tot = [acum[d][tot_rows[d]:tot_rows[d] + 1, :] for d in dirs]
        decay_x = [_dot_exact_rhs(jnp.concatenate([jnp.exp(acum[d]), jnp.exp(tot[d] - acum[d])], axis=0), emats[d])
                   for d in dirs]
        xd = [x[d] * dt_x[d] for d in dirs]
        lhs, rhs = [], []
        for d in dirs:
            mixes, parts = [], []
            for j in range(4):
                ln = lanes0[d] + j
                dd = acum[d][:, ln:ln + 1] - acum_t[d][j:j + 1, :]
                mixes.append((sc[d] * jnp.exp(jnp.where(masks[d], dd, NEG))).astype(BF16))
                parts.append(jnp.where(hid == j, xd[d], 0.0).astype(BF16))
            lhs.append(jnp.concatenate(mixes, axis=1))
            rhs.append(jnp.concatenate(parts, axis=0))
        y = [_dot(lhs[d], rhs[d]) for d in dirs]
        upd = [_dot_tn(bb[d].astype(BF16), (xd[d] * decay_x[d][ch:]).astype(BF16)) for d in dirs]
        for d in dirs:
            ea_x = decay_x[d][:ch]
            etot_x = ea_x[tot_rows[d]:tot_rows[d] + 1, :]
            y_refs[d][pl.ds(bases[d], ch), :] = y[d] + y_off[d] * ea_x
            st_refs[d][...] = st[d] * etot_x + upd[d]
        return carry

    lax.fori_loop(0, nch, scan_body, 0)

    def fin_body(c, carry):
        base = pl.multiple_of(c * ch, ch)
        y = yf_scr[pl.ds(base, ch), :] + yb_scr[pl.ds(base, ch), :] + xc_scr[pl.ds(base, ch), :] * dsk_x
        y = y * _silu(z_ref[0, pl.ds(base, ch), :])
        out_ref[0, pl.ds(base, ch), :] = _rms(y, nw_ref[...])
        return carry

    lax.fori_loop(0, nch, fin_body, 0)


def _ssd_mixer(proj3, small3, conv_w, conv_b, hp, norm_w):
    b, length, _ = proj3.shape
    gw = 4 * SSD_HEAD_DIM
    xs0 = SSD_INNER // gw
    b0 = 2 * SSD_INNER // SSD_STATE
    c0 = b0 + SSD_GROUPS
    nb_x = SSD_INNER // SSD_STATE
    seq = lambda w, off: pl.BlockSpec((1, length, w), lambda i, g: (i, 0, off + g))
    par = lambda r, w, off: pl.BlockSpec((r, w), lambda i, g: (0, off + g))
    return pl.pallas_call(
        _ssd_kernel,
        grid=(b, SSD_GROUPS),
        in_specs=[
            seq(gw, 0), seq(gw, xs0), seq(SSD_STATE, b0), seq(SSD_STATE, c0),
            pl.BlockSpec((1, length, LANES), lambda i, g: (i, 0, 0)),
            par(CONV_K, gw, 0), par(CONV_K, SSD_STATE, nb_x), par(CONV_K, SSD_STATE, nb_x + SSD_GROUPS),
            par(1, gw, 0), par(1, SSD_STATE, nb_x), par(1, SSD_STATE, nb_x + SSD_GROUPS),
            pl.BlockSpec((SUBLANES, LANES), lambda i, g: (0, 0)),
            par(1, gw, 0),
        ],
        out_specs=pl.BlockSpec((1, length, gw), lambda i, g: (i, 0, g)),
        out_shape=jax.ShapeDtypeStruct((b, length, SSD_INNER), F32),
        scratch_shapes=[
            pltpu.VMEM((length, gw), F32), pltpu.VMEM((length, SSD_STATE), F32),
            pltpu.VMEM((length, SSD_STATE), F32), pltpu.VMEM((length, gw), F32), pltpu.VMEM((length, gw), F32),
            pltpu.VMEM((SSD_STATE, gw), F32), pltpu.VMEM((SSD_STATE, gw), F32),
        ],
        compiler_params=_cparams(("parallel", "arbitrary")),
        name="ssd_mixer",
    )(proj3, proj3, proj3, proj3, small3, conv_w, conv_w, conv_w, conv_b, conv_b, conv_b, hp, norm_w)


def _gdn_kernel(q_ref, k_ref, v_ref, z_ref, sm_ref, cwq_ref, cwk_ref, cwv_ref, hp_ref, nw_ref, out_ref,
                qn_scr, kn_scr, vn_scr, u_scr, wq_scr, qk_scr, kd_scr, et_scr, o_scr):
    ch = GDN_CHUNK
    length = z_ref.shape[1]
    nch = length // ch
    dk = GDN_HEAD_DIM
    head = pl.program_id(1)
    shift = lax.rem(LANES - head, LANES)
    lane_a = 2 * SSD_HEADS
    lane_beta = lane_a + 2 * GDN_HEADS
    hp = pltpu.roll(hp_ref[...], shift, 1)
    bias_row = hp[0:1, :]
    acoef_row = jnp.exp(hp[1:2, :])

    row = lax.broadcasted_iota(I32, (ch, ch), 0)
    col = lax.broadcasted_iota(I32, (ch, ch), 1)
    eye = jnp.where(row == col, 1.0, 0.0)
    n_double = (GDN_INV_BLOCK - 1).bit_length() - 1
    levels = []
    size = GDN_INV_BLOCK
    inside = (row // size) == (col // size)
    levels.append(inside)
    while size < ch:
        size *= 2
        merged = (row // size) == (col // size)
        levels.append(merged & jnp.logical_not(inside))
        inside = merged

    def conv_body(c, carry):
        base = pl.multiple_of(c * ch, ch)
        q = _conv_silu_chunk(q_ref, cwq_ref, None, c, nch, ch)
        k = _conv_silu_chunk(k_ref, cwk_ref, None, c, nch, ch)
        qn_scr[pl.ds(base, ch), :] = q * (lax.rsqrt(jnp.sum(q * q, axis=-1, keepdims=True) + EPS) * dk ** -0.5)
        kn_scr[pl.ds(base, ch), :] = k * lax.rsqrt(jnp.sum(k * k, axis=-1, keepdims=True) + EPS)
        vn_scr[pl.ds(base, ch), :] = _conv_silu_chunk(v_ref, cwv_ref, None, c, nch, ch)
        return carry

    lax.fori_loop(0, nch, conv_body, 0)

    incl = ((row >= col), (row <= col))
    strict = ((row > col), (row < col))
    lmat = tuple(jnp.where(m, 1.0, 0.0).astype(BF16) for m in incl)
    umat = tuple(jnp.where(m, 1.0, 0.0) for m in strict)
    jref = (0, ch - 1)
    tot_row = (ch - 1, 0)
    group = min(GDN_PREP_CHUNKS, nch)

    def prep_body(gi, carry):
        chains = []
        for ci in range(group):
            c = gi * group + ci
            base = pl.multiple_of(c * ch, ch)
            q = qn_scr[pl.ds(base, ch), :]
            k = kn_scr[pl.ds(base, ch), :]
            v = vn_scr[pl.ds(base, ch), :]
            sm = pltpu.roll(sm_ref[0, pl.ds(base, ch), :], shift, 1)
            g_all = -acoef_row * _softplus(sm + bias_row)
            beta_all = jax.nn.sigmoid(sm)
            kbf = k.astype(BF16)
            kq = _dot_nt(jnp.concatenate([kbf, q.astype(BF16)], axis=0), kbf)
            for di in (0, 1):
                ln = lane_a + di * GDN_HEADS
                lb = lane_beta + di * GDN_HEADS
                chains.append(dict(c=c, di=di, q=q, k=k, v=v, kq=kq, g=g_all[:, ln:ln + 1],
                                   beta=beta_all[:, lb:lb + 1]))
        for t in chains:
            t["dmat"] = _dot_exact_lhs(lmat[t["di"]], t["g"] * umat[t["di"]], terms=2)
        for t in chains:
            di = t["di"]
            t["kb"] = t["k"] * t["beta"]
            dec = t["dmat"][:, jref[di]:jref[di] + 1] + t["g"][jref[di]:jref[di] + 1, :]
            t["dec"] = dec
            t["tot"] = dec[tot_row[di]:tot_row[di] + 1, :]
            gam = jnp.exp(jnp.where(incl[di], t["dmat"], NEG))
            t["qk"] = (t["kq"][ch:] * gam).astype(BF16)
            a = jnp.where(strict[di], t["kq"][:ch] * (gam * t["beta"]), 0.0)
            ad = jnp.where(levels[0], a, 0.0)
            t["p"] = eye - ad
            t["ab"] = ad.astype(BF16)
            t["off"] = [jnp.where(m, a, 0.0).astype(BF16) for m in levels[1:]]
        for t in chains:
            t["x"] = _dot(t["ab"], t["ab"])
        for i in range(n_double):
            last = i == n_double - 1
            for t in chains:
                xb = t["x"].astype(BF16)
                pb = t["p"].astype(BF16)
                t["r"] = _dot(xb, pb) if last else _dot(xb, jnp.concatenate([pb, xb], axis=1))
            for t in chains:
                t["p"] = t["p"] + t["r"][:, :ch]
                if not last:
                    t["x"] = t["r"][:, ch:]
        for lv in range(len(levels) - 1):
            for t in chains:
                t["m"] = _dot(t["off"][lv], t["p"].astype(BF16))
            for t in chains:
                t["p"] = t["p"] - _dot(t["p"].astype(BF16), t["m"].astype(BF16))
        for t in chains:
            t["edec"] = jnp.exp(t["dec"])
            rhs = jnp.concatenate([t["v"] * t["beta"], t["kb"] * t["edec"]], axis=1).astype(BF16)
            t["uw"] = _dot(t["p"].astype(BF16), rhs)
        for t in chains:
            c, di = t["c"], t["di"]
            base = pl.multiple_of(c * ch, ch)
            base2 = pl.multiple_of(c * 2 * ch, 2 * ch)
            u_scr[di, pl.ds(base, ch), :] = t["uw"][:, :dk]
            wq_scr[di, pl.ds(base2, 2 * ch), :] = jnp.concatenate(
                [t["uw"][:, dk:], t["q"] * t["edec"]], axis=0).astype(BF16)
            qk_scr[di, pl.ds(base, ch), :] = t["qk"]
            kd_scr[di, pl.ds(base, ch), :] = (t["k"] * jnp.exp(t["tot"] - t["dec"])).astype(BF16)
            et_scr[di, pl.ds(pl.multiple_of(c * SUBLANES, SUBLANES), SUBLANES), :] = jnp.broadcast_to(
                jnp.exp(t["tot"]), (SUBLANES, LANES))
        return carry

    lax.fori_loop(0, nch // group, prep_body, 0)

    def rec_body(ci, carry):
        cs = (ci, nch - 1 - ci)
        bases = [pl.multiple_of(c * ch, ch) for c in cs]
        ws = [_dot(wq_scr[di, pl.ds(pl.multiple_of(cs[di] * 2 * ch, 2 * ch), 2 * ch), :], carry[di].astype(BF16))
              for di in (0, 1)]
        v_new = [(u_scr[di, pl.ds(bases[di], ch), :] - ws[di][:ch]).astype(BF16) for di in (0, 1)]
        upd = [_dot_tn(kd_scr[di, pl.ds(bases[di], ch), :], v_new[di]) for di in (0, 1)]
        intra = [_dot(qk_scr[di, pl.ds(bases[di], ch), :], v_new[di]) for di in (0, 1)]
        s_new = []
        for di in (0, 1):
            et = et_scr[di, pl.ds(pl.multiple_of(cs[di] * SUBLANES, SUBLANES), 1), :]
            s_new.append(carry[di] * et + upd[di])
            o_scr[di, pl.ds(bases[di], ch), :] = ws[di][ch:] + intra[di]
        return tuple(s_new)

    zero = jnp.zeros((dk, dk), F32)
    lax.fori_loop(0, nch, rec_body, (zero, zero))

    def fin_body(c, carry):
        base = pl.multiple_of(c * ch, ch)
        o = o_scr[0, pl.ds(base, ch), :] + o_scr[1, pl.ds(base, ch), :]
        out_ref[0, pl.ds(base, ch), :] = _rms(o, nw_ref[...]) * _silu(z_ref[0, pl.ds(base, ch), :])
        return carry

    lax.fori_loop(0, nch, fin_body, 0)


def _gdn_mixer(proj3, small3, conv_w, hp, norm_w):
    b, length, _ = proj3.shape
    dk = GDN_HEAD_DIM
    nch = length // GDN_CHUNK
    q0 = (2 * SSD_INNER + 2 * SSD_GROUPS * SSD_STATE) // dk
    seq = lambda off: pl.BlockSpec((1, length, dk), lambda i, h: (i, 0, off + h))
    par = lambda off: pl.BlockSpec((CONV_K, dk), lambda i, h: (0, off + h))
    return pl.pallas_call(
        _gdn_kernel,
        grid=(b, GDN_HEADS),
        in_specs=[
            seq(q0), seq(q0 + GDN_HEADS), seq(q0 + 2 * GDN_HEADS), seq(q0 + 3 * GDN_HEADS),
            pl.BlockSpec((1, length, LANES), lambda i, h: (i, 0, 0)),
            par(0), par(GDN_HEADS), par(2 * GDN_HEADS),
            pl.BlockSpec((SUBLANES, LANES), lambda i, h: (0, 0)),
            pl.BlockSpec((1, dk), lambda i, h: (0, 0)),
        ],
        out_specs=pl.BlockSpec((1, length, dk), lambda i, h: (i, 0, h)),
        out_shape=jax.ShapeDtypeStruct((b, length, GDN_INNER), F32),
        scratch_shapes=[
            pltpu.VMEM((length, dk), F32), pltpu.VMEM((length, dk), F32), pltpu.VMEM((length, dk), F32),
            pltpu.VMEM((2, length, dk), F32),
            pltpu.VMEM((2, 2 * length, dk), BF16),
            pltpu.VMEM((2, length, GDN_CHUNK), BF16),
            pltpu.VMEM((2, length, dk), BF16),
            pltpu.VMEM((2, nch * SUBLANES, LANES), F32),
            pltpu.VMEM((2, length, dk), F32),
        ],
        compiler_params=_cparams(("parallel", "arbitrary")),
        name="gdn_mixer",
    )(proj3, proj3, proj3, proj3, small3, conv_w, conv_w, conv_w, hp, norm_w)


def _merge_kernel(x_ref, ya_ref, yb_ref, ga_ref, gb_ref, wa_ref, wb_ref, wo_ref, gf_ref, h_ref, n_ref):
    ya = _dot(ya_ref[...].astype(BF16), wa_ref[...])
    yb = _dot(yb_ref[...].astype(BF16), wb_ref[...])
    merged = jax.nn.sigmoid(ga_ref[...]) * ya + jax.nn.sigmoid(gb_ref[...]) * yb
    h = x_ref[...] + _dot(merged.astype(BF16), wo_ref[...])
    h_ref[...] = h
    n_ref[...] = _rms(h, gf_ref[...]).astype(BF16)


def _merge(x2d, ya, yb, proj2d, wa, wb, wo, gf):
    t, d = x2d.shape
    tm = min(TM_MERGE, t)
    ga0 = (proj2d.shape[1] - 2 * d) // d
    tok = lambda: pl.BlockSpec((tm, d), lambda i: (i, 0))
    wsp = lambda: pl.BlockSpec((d, d), lambda i: (0, 0))
    return pl.pallas_call(
        _merge_kernel,
        grid=(t // tm,),
        in_specs=[tok(), tok(), tok(),
                  pl.BlockSpec((tm, d), lambda i: (i, ga0)), pl.BlockSpec((tm, d), lambda i: (i, ga0 + 1)),
                  wsp(), wsp(), wsp(), pl.BlockSpec((1, d), lambda i: (0, 0))],
        out_specs=[tok(), tok()],
        out_shape=[jax.ShapeDtypeStruct((t, d), F32), jax.ShapeDtypeStruct((t, d), BF16)],
        compiler_params=_cparams(("parallel",)),
        name="merge",
    )(x2d, ya, yb, proj2d, proj2d, wa, wb, wo, gf)


def _topk_store(problems, k):
    state = [p[0] for p in problems]
    iotas = [lax.broadcasted_iota(I32, s.shape, 0).astype(F32) for s in state]
    for i in range(k):
        ms = [jnp.max(s, axis=0, keepdims=True) for s in state]
        cands = [jnp.where(s == m, io, float(s.shape[0])) for s, m, io in zip(state, ms, iotas)]
        ixs = [jnp.min(c, axis=0, keepdims=True) for c in cands]
        hits = [c == ix for c, ix in zip(cands, ixs)]
        for (_, vals_ref, outs_ref, payload), m, ix, hit in zip(problems, ms, ixs, hits):
            vals_ref[i:i + 1, :] = m
            outs_ref[i:i + 1, :] = (ix if payload is None
                                    else jnp.max(jnp.where(hit, payload, -1.0), axis=0, keepdims=True))
        state = [jnp.where(hit, -jnp.inf, s) for s, hit in zip(state, hits)]


def _route_unit(tok_ref, wq_ref, keys_ref, e_scr, gt_scr, tv_scr, ti_scr, bv_scr, be_scr, head, tile):
    kk = PEER_TOPK
    half = kk // 2
    toks = tok_ref[pl.ds(pl.multiple_of(tile * LANES, LANES), LANES), :]
    qry = _dot(toks, wq_ref[...])
    st = [_dot_nt(keys_ref[0, z], qry[:, z * PEER_HALF:(z + 1) * PEER_HALF].astype(BF16)) for z in range(2)]
    _topk_store([(st[z], tv_scr.at[z], ti_scr.at[z], None) for z in range(2)], kk)
    s0, s1 = tv_scr[0], tv_scr[1]
    e0, i1 = ti_scr[0] * float(PEER_KEYS), ti_scr[1]
    cs = [s0[0:1] + s1] + [s0[i:i + 1] + s1[0:half] for i in range(1, half)] + [s0[half:] + s1[0:1]]
    ce = [e0[0:1] + i1] + [e0[i:i + 1] + i1[0:half] for i in range(1, half)] + [e0[half:] + i1[0:1]]
    _topk_store([(jnp.concatenate(cs, axis=0), bv_scr, be_scr, jnp.concatenate(ce, axis=0))], kk)
    best = bv_scr[...]
    ex = jnp.exp(best - best[0:1])
    rows = pl.ds(pl.multiple_of(head * kk, kk), kk)
    lanes = pl.ds(pl.multiple_of(tile * LANES, LANES), LANES)
    e_scr[rows, lanes] = be_scr[...]
    gt_scr[rows, lanes] = ex / jnp.sum(ex, axis=0, keepdims=True)


def _peer_kernel(nm_ref, nr_ref, wq_ref, keys_ref, u_ref, v_ref, out_ref,
                 g_scr, e_scr, gt_scr, et_scr, gtt_scr, tv_scr, ti_scr, bv_scr, be_scr):
    tm = nm_ref.shape[0]
    nk = PEER_KEYS
    ntile = tm // LANES
    tile_i = pl.program_id(0)
    step = pl.program_id(1)

    @pl.when((tile_i == 0) & (step == 0))
    def _():
        e_scr[...] = jnp.zeros_like(e_scr)
        gt_scr[...] = jnp.zeros_like(gt_scr)

    @pl.when(step == 0)
    def _():
        out_ref[...] = jnp.zeros_like(out_ref)
        et_scr[...] = e_scr[...].T.astype(I32)
        gtt_scr[...] = gt_scr[...].T
        sub = lax.broadcasted_iota(I32, (nk, LANES), 0)

        zero = jnp.zeros((nk, LANES), BF16)

        def pair_body(tp, carry):
            t0 = 2 * tp
            ats, bts = [], []
            for o in range(2):
                re = et_scr[pl.ds(t0 + o, 1), :]
                rg = 0.5 * gtt_scr[pl.ds(t0 + o, 1), :]
                ats.append(jnp.where(sub == (re >> 7), rg, 0.0).astype(BF16))
                bts.append(jnp.where(sub == (re & (nk - 1)), 1.0, 0.0).astype(BF16))
            bt = jnp.concatenate([jnp.concatenate([bts[0], zero], axis=1),
                                  jnp.concatenate([zero, bts[1]], axis=1)], axis=0)
            g2 = _dot_nt(jnp.concatenate(ats, axis=1), bt)
            for o in range(2):
                g_scr[pl.ds(t0 + o, nk, stride=G_PITCH), :] = g2[:, o * LANES:(o + 1) * LANES]
            return carry

        lax.fori_loop(0, tm // 2, pair_body, 0, unroll=TOKEN_UNROLL // 2)

    _route_unit(nr_ref, wq_ref, keys_ref, e_scr, gt_scr, tv_scr, ti_scr, bv_scr, be_scr,
                step // ntile, lax.rem(step, ntile))

    hid = _dot_nt(nm_ref[...], u_ref[...])
    act = hid * (1.0 + lax.erf(hid * (0.5 ** 0.5)))
    gates = [g_scr[pl.ds(pl.multiple_of((step * EXP_ROWS + r) * G_PITCH, SUBLANES), tm), :]
             for r in range(EXP_ROWS)]
    wts = (act * jnp.concatenate(gates, axis=1)).astype(BF16)
    out_ref[...] = out_ref[...] + _dot(wts, v_ref[...])


def _peer(n2, wq, keys, eu, ev):
    t, d = n2.shape
    tm = min(TM_EXP, t)
    assert tm == TM_EXP, "gate slab pitch is derived from TM_EXP"
    assert PEER_KEYS == LANES, "expert ids are decoded with a 7-bit shift"
    hk = PEER_HEADS * PEER_TOPK
    er = EXP_ROWS * PEER_KEYS
    nsteps = eu.shape[0] // er
    nt = t // tm
    assert nsteps == PEER_HEADS * (tm // LANES), "one routing unit per expert step"
    qw = 2 * PEER_HALF
    prev = lambda i, s: (jnp.maximum(i - 1, 0), 0)
    return pl.pallas_call(
        _peer_kernel,
        grid=(nt + 1, nsteps),
        in_specs=[pl.BlockSpec((tm, d), prev),
                  pl.BlockSpec((tm, d), lambda i, s: (jnp.minimum(i, nt - 1), 0)),
                  pl.BlockSpec((d, qw), lambda i, s: (0, s // (tm // LANES))),
                  pl.BlockSpec((1, 2, PEER_KEYS, PEER_HALF), lambda i, s: (s // (tm // LANES), 0, 0, 0)),
                  pl.BlockSpec((er, d), lambda i, s: (s, 0)),
                  pl.BlockSpec((er, d), lambda i, s: (s, 0))],
        out_specs=pl.BlockSpec((tm, d), prev),
        out_shape=jax.ShapeDtypeStruct((t, d), F32),
        scratch_shapes=[pltpu.VMEM((PEER_KEYS * G_PITCH, LANES), F32),
                        pltpu.VMEM((hk, tm), F32), pltpu.VMEM((hk, tm), F32),
                        pltpu.VMEM((tm, hk), I32), pltpu.VMEM((tm, hk), F32),
                        pltpu.VMEM((2, PEER_TOPK, LANES), F32), pltpu.VMEM((2, PEER_TOPK, LANES), F32),
                        pltpu.VMEM((PEER_TOPK, LANES), F32), pltpu.VMEM((PEER_TOPK, LANES), F32)],
        compiler_params=_cparams(("arbitrary", "arbitrary")),
        name="peer_ffn",
    )(n2, n2, wq, keys, eu, ev)


def _ple_kernel(h_ref, f_ref, p_ref, gp_ref, wg_ref, wp_ref, gf_ref, y_ref):
    h = h_ref[...] + f_ref[...]
    gate = jax.nn.sigmoid(_dot(_rms(h, gp_ref[...]).astype(BF16), wg_ref[...]))
    h = h + gate * _dot(p_ref[...].astype(BF16), wp_ref[...])
    y_ref[...] = _rms(h, gf_ref[...])


def _ple(h1, ffn, p2d, gp, wg, wp, gfin):
    t, d = h1.shape
    pd = p2d.shape[1]
    tm = min(TM_PLE, t)
    vec = lambda: pl.BlockSpec((1, d), lambda i: (0, 0))
    tok = lambda: pl.BlockSpec((tm, d), lambda i: (i, 0))
    return pl.pallas_call(
        _ple_kernel,
        grid=(t // tm,),
        in_specs=[tok(), tok(), pl.BlockSpec((tm, pd), lambda i: (i, 0)),
                  vec(), pl.BlockSpec((d, d), lambda i: (0, 0)), pl.BlockSpec((pd, d), lambda i: (0, 0)), vec()],
        out_specs=tok(),
        out_shape=jax.ShapeDtypeStruct((t, d), F32),
        compiler_params=_cparams(("parallel",)),
        name="ple_final",
    )(h1, ffn, p2d, gp, wg, wp, gfin)


def _pad_lanes(v):
    return jnp.pad(v, (0, LANES - v.shape[0]))


def _prepare(norm_mix, w_in, ssd_conv_w, ssd_conv_b, ssd_a_log, ssd_dt_bias, ssd_d, ssd_norm, w_ssd_out,
             gdn_conv_w, gdn_a_log, gdn_dt_bias, gdn_norm, w_gdn_out, w_out, norm_ffn, peer_query, peer_keys,
             expert_u, expert_v, norm_ple, w_ple_gate, w_ple_proj, norm_final):
    w = w_in[0]
    xbc = SSD_INNER + 2 * SSD_GROUPS * SSD_STATE
    c_dt = SSD_INNER + xbc
    c_qkv = c_dt + 2 * SSD_HEADS
    c_a = c_qkv + 4 * GDN_INNER
    c_ga = c_a + 4 * GDN_HEADS
    w_main = jnp.concatenate([w[:, :c_dt], w[:, c_qkv:c_a], w[:, c_ga:]], axis=1).astype(BF16)
    w_small = jnp.concatenate([w[:, c_dt:c_qkv], w[:, c_a:c_ga]], axis=1)
    w_small = jnp.pad(w_small, ((0, 0), (0, LANES - w_small.shape[1]))).astype(BF16)
    hp = jnp.stack([
        _pad_lanes(jnp.concatenate([ssd_dt_bias[0].reshape(-1), gdn_dt_bias[0].reshape(-1)])),
        _pad_lanes(jnp.concatenate([ssd_a_log[0].reshape(-1), gdn_a_log[0].reshape(-1)])),
        _pad_lanes(ssd_d[0]),
    ])
    hp = jnp.pad(hp, ((0, SUBLANES - hp.shape[0]), (0, 0)))
    row = lambda v: v.reshape(1, -1)
    return dict(
        g_mix=row(norm_mix[0]), w_main=w_main, w_small=w_small, hp=hp,
        ssd_conv_w=ssd_conv_w[0], ssd_conv_b=row(ssd_conv_b[0]), ssd_norm=row(ssd_norm[0]),
        gdn_conv_w=gdn_conv_w[0], gdn_norm=row(gdn_norm[0]),
        wa=w_ssd_out[0].astype(BF16), wb=w_gdn_out[0].astype(BF16), wo=w_out[0].astype(BF16),
        g_ffn=row(norm_ffn[0]), wq=peer_query[0].astype(BF16),
        keys=peer_keys[0].astype(BF16),
        eu=expert_u[0].astype(BF16), ev=expert_v[0].astype(BF16),
        g_ple=row(norm_ple[0]), wg=w_ple_gate[0].astype(BF16), wp=w_ple_proj[0].astype(BF16),
        g_fin=row(norm_final),
    )


def _trunk(x, ple, p):
    b, length, d = x.shape
    t = b * length
    x2d = x.reshape(t, d)
    proj, small = _in_proj(x2d, p["g_mix"], p["w_main"], p["w_small"])
    proj3 = proj.reshape(b, length, -1)
    small3 = small.reshape(b, length, LANES)
    ya = _ssd_mixer(proj3, small3, p["ssd_conv_w"], p["ssd_conv_b"], p["hp"], p["ssd_norm"])
    yb = _gdn_mixer(proj3, small3, p["gdn_conv_w"], p["hp"], p["gdn_norm"])
    h1, n2 = _merge(x2d, ya.reshape(t, -1), yb.reshape(t, -1), proj, p["wa"], p["wb"], p["wo"], p["g_ffn"])
    ffn = _peer(n2, p["wq"], p["keys"], p["eu"], p["ev"])
    y = _ple(h1, ffn, ple.reshape(t, -1), p["g_ple"], p["wg"], p["wp"], p["g_fin"])
    return y.reshape(b, length, d)


def kernel(x_prompt, x_sample, p_prompt, p_sample, norm_mix, w_in, ssd_conv_w, ssd_conv_b, ssd_a_log, ssd_dt_bias, ssd_d, ssd_norm, w_ssd_out, gdn_conv_w, gdn_a_log, gdn_dt_bias, gdn_norm, w_gdn_out, w_out, norm_ffn, peer_query, peer_keys, expert_u, expert_v, norm_ple, w_ple_gate, w_ple_proj, norm_final):
    assert w_in.shape[0] == 1, "single-layer trunk"
    p = _prepare(norm_mix, w_in, ssd_conv_w, ssd_conv_b, ssd_a_log, ssd_dt_bias, ssd_d, ssd_norm, w_ssd_out,
                 gdn_conv_w, gdn_a_log, gdn_dt_bias, gdn_norm, w_gdn_out, w_out, norm_ffn, peer_query, peer_keys,
                 expert_u, expert_v, norm_ple, w_ple_gate, w_ple_proj, norm_final)
    return (_trunk(x_prompt, p_prompt[0], p), _trunk(x_sample, p_sample[0], p))
```

```python
import jax
import jax.numpy as jnp
from jax import lax
from jax.experimental import pallas as pl
from jax.experimental.pallas import tpu as pltpu

F32 = jnp.float32
BF16 = jnp.bfloat16
I32 = jnp.int32

EPS = 1e-6
CONV_K = 4
SSD_HEADS = 16
SSD_HEAD_DIM = 64
SSD_GROUPS = 4
SSD_STATE = 128
SSD_INNER = SSD_HEADS * SSD_HEAD_DIM
GDN_HEADS = 8
GDN_HEAD_DIM = 128
GDN_INNER = GDN_HEADS * GDN_HEAD_DIM
PEER_HEADS = 8
PEER_KEYS = 128
PEER_TOPK = 16
PEER_HALF = 128

LANES = 128
SUBLANES = 8
SSD_CHUNK = 128
GDN_CHUNK = 128
GDN_PREP_CHUNKS = 8
GDN_INV_BLOCK = 8
VMEM_LIMIT = 56 * 1024 * 1024

TM_PROJ = 1024
TN_PROJ = 1024
TM_MERGE = 256
TM_EXP = 512
EXP_ROWS = 8
G_PITCH = TM_EXP + 8
TOKEN_UNROLL = 32
TM_PLE = 512

NEG = -1e30


def _dot(a, b):
    return jnp.dot(a, b, preferred_element_type=F32)


def _dot_nt(a, b):
    return lax.dot_general(a, b, (((1,), (1,)), ((), ())), preferred_element_type=F32)


def _dot_tn(a, b):
    return lax.dot_general(a, b, (((0,), (0,)), ((), ())), preferred_element_type=F32)


def _bf16_terms(m, terms):
    out = []
    for _ in range(terms - 1):
        t = m.astype(BF16)
        out.append(t)
        m = m - t.astype(F32)
    out.append(m.astype(BF16))
    return out


def _dot_exact_lhs(lb, m, terms=3):
    n = m.shape[1]
    r = _dot(lb, jnp.concatenate(_bf16_terms(m, terms), axis=1))
    return sum(r[:, i * n:(i + 1) * n] for i in range(1, terms)) + r[:, :n]


def _dot_exact_rhs(m, eb, terms=2):
    return _dot(jnp.concatenate(_bf16_terms(m, terms), axis=1), jnp.concatenate([eb] * terms, axis=0))


def _rows_transposed(sel, m, terms=3):
    n = m.shape[0]
    r = _dot_nt(sel, jnp.concatenate(_bf16_terms(m, terms), axis=0))
    return sum(r[:, i * n:(i + 1) * n] for i in range(1, terms)) + r[:, :n]


def _softplus(x):
    return jnp.maximum(x, 0.0) + jnp.log1p(jnp.exp(-jnp.abs(x)))


def _silu(x):
    return x * jax.nn.sigmoid(x)


def _rms(x, g):
    return x * lax.rsqrt(jnp.mean(x * x, axis=-1, keepdims=True) + EPS) * g


def _cparams(sem):
    return pltpu.CompilerParams(dimension_semantics=sem, vmem_limit_bytes=VMEM_LIMIT)


def _in_proj_kernel(x_ref, g_ref, wm_ref, ws_ref, main_ref, small_ref, n_scr):
    @pl.when(pl.program_id(1) == 0)
    def _():
        n = _rms(x_ref[...], g_ref[...]).astype(BF16)
        n_scr[...] = n
        small_ref[...] = _dot(n, ws_ref[...])

    main_ref[...] = _dot(n_scr[...], wm_ref[...])


def _in_proj(x2d, g, w_main, w_small):
    t, d = x2d.shape
    n_main = w_main.shape[1]
    tm = min(TM_PROJ, t)
    return pl.pallas_call(
        _in_proj_kernel,
        grid=(t // tm, n_main // TN_PROJ),
        in_specs=[
            pl.BlockSpec((tm, d), lambda i, j: (i, 0)),
            pl.BlockSpec((1, d), lambda i, j: (0, 0)),
            pl.BlockSpec((d, TN_PROJ), lambda i, j: (0, j)),
            pl.BlockSpec((d, LANES), lambda i, j: (0, 0)),
        ],
        out_specs=[
            pl.BlockSpec((tm, TN_PROJ), lambda i, j: (i, j)),
            pl.BlockSpec((tm, LANES), lambda i, j: (i, 0)),
        ],
        out_shape=[jax.ShapeDtypeStruct((t, n_main), F32), jax.ShapeDtypeStruct((t, LANES), F32)],
        scratch_shapes=[pltpu.VMEM((tm, d), BF16)],
        compiler_params=_cparams(("parallel", "arbitrary")),
        name="in_proj",
    )(x2d, g, w_main, w_small)


def _conv_silu_chunk(src_ref, w_ref, b_ref, c, nch, rows):
    length = src_ref.shape[1]
    base = pl.multiple_of(c * rows, rows)
    main = src_ref[0, pl.ds(base, rows), :]
    prev = src_ref[0, pl.ds(pl.multiple_of(jnp.maximum(base - SUBLANES, 0), SUBLANES), SUBLANES), :]
    nxt = src_ref[0, pl.ds(pl.multiple_of(jnp.minimum(base + rows, length - SUBLANES), SUBLANES), SUBLANES), :]
    prev = jnp.where(c > 0, prev, 0.0)
    nxt = jnp.where(c < nch - 1, nxt, 0.0)
    ext = jnp.concatenate([prev, main, nxt], axis=0)
    w = w_ref[...]
    lo = SUBLANES - CONV_K // 2
    y = ext[lo:lo + rows] * w[0:1]
    for j in range(1, CONV_K):
        y = y + ext[lo + j:lo + j + rows] * w[j:j + 1]
    if b_ref is not None:
        y = y + b_ref[...]
    return _silu(y)


def _expand_heads(t, lane0, width):
    r = t.shape[0]
    hid = lax.broadcasted_iota(I32, (r, 4 * width), 1) // width
    out = jnp.broadcast_to(t[:, lane0 + 3:lane0 + 4], (r, 4 * width))
    for j in (2, 1, 0):
        out = jnp.where(hid == j, t[:, lane0 + j:lane0 + j + 1], out)
    return out


def _ssd_kernel(z_ref, xs_ref, bm_ref, cm_ref, sm_ref, cwx_ref, cwb_ref, cwc_ref, cbx_ref, cbb_ref, cbc_ref,
                hp_ref, nw_ref, out_ref, xc_scr, bc_scr, cc_scr, yf_scr, yb_scr, stf_scr, stb_scr):
    ch = SSD_CHUNK
    length = z_ref.shape[1]
    nch = length // ch
    gwid = 4 * SSD_HEAD_DIM
    grp = pl.program_id(1)
    shift = lax.rem(LANES - 4 * grp, LANES)
    hp = pltpu.roll(hp_ref[...], shift, 1)
    bias_row = hp[0:1, :]
    a_row = -jnp.exp(hp[1:2, :])
    dsk_x = _expand_heads(hp[2:3, :], 0, SSD_HEAD_DIM)

    row = lax.broadcasted_iota(I32, (ch, ch), 0)
    col = lax.broadcasted_iota(I32, (ch, ch), 1)
    hid = lax.broadcasted_iota(I32, (ch, gwid), 1) // SSD_HEAD_DIM

    def conv_body(c, carry):
        base = pl.multiple_of(c * ch, ch)
        xc_scr[pl.ds(base, ch), :] = _conv_silu_chunk(xs_ref, cwx_ref, cbx_ref, c, nch, ch)
        bc_scr[pl.ds(base, ch), :] = _conv_silu_chunk(bm_ref, cwb_ref, cbb_ref, c, nch, ch)
        cc_scr[pl.ds(base, ch), :] = _conv_silu_chunk(cm_ref, cwc_ref, cbc_ref, c, nch, ch)
        return carry

    lax.fori_loop(0, nch, conv_body, 0)

    masks = ((row >= col), (row <= col))
    lmats = tuple(jnp.where(m, 1.0, 0.0).astype(BF16) for m in masks)
    lanes0 = (0, SSD_HEADS)
    tot_rows = (ch - 1, 0)
    erow = lax.broadcasted_iota(I32, (LANES, gwid), 0)
    ecol = lax.broadcasted_iota(I32, (LANES, gwid), 1) // SSD_HEAD_DIM
    emats = tuple(jnp.where(erow == ecol + l0, 1.0, 0.0).astype(BF16) for l0 in lanes0)
    srow = lax.broadcasted_iota(I32, (SUBLANES, LANES), 0)
    scol = lax.broadcasted_iota(I32, (SUBLANES, LANES), 1)
    sels = tuple(jnp.where(scol == srow + l0, 1.0, 0.0).astype(BF16) for l0 in lanes0)
    st_refs = (stf_scr, stb_scr)
    y_refs = (yf_scr, yb_scr)
    dirs = (0, 1)

    stf_scr[...] = jnp.zeros_like(stf_scr)
    stb_scr[...] = jnp.zeros_like(stb_scr)

    def scan_body(ci, carry):
        bases = (pl.multiple_of(ci * ch, ch), pl.multiple_of((nch - 1 - ci) * ch, ch))
        x = [xc_scr[pl.ds(b, ch), :] for b in bases]
        bb = [bc_scr[pl.ds(b, ch), :] for b in bases]
        cb = [cc_scr[pl.ds(b, ch), :].astype(BF16) for b in bases]
        dt_all = [_softplus(pltpu.roll(sm_ref[0, pl.ds(b, ch), :], shift, 1) + bias_row) for b in bases]
        acum = [_dot_exact_lhs(lmats[d], dt_all[d] * a_row) for d in dirs]
        sc = [jnp.where(masks[d], _dot_nt(cb[d], bb[d].astype(BF16)), 0.0) for d in dirs]
        st = [st_refs[d][...] for d in dirs]
        y_off = [_dot(cb[d], st[d].astype(BF16)) for d in dirs]
        dt_x = [_dot_exact_rhs(dt_all[d], emats[d]) for d in dirs]
        acum_t = [_rows_transposed(sels[d], acum[d]) for d in dirs]
        tot = [acum[d][tot_rows[d]:tot_rows[d] + 1, :] for d in dirs]
        decay_x = [_dot_exact_rhs(jnp.concatenate([jnp.exp(acum[d]), jnp.exp(tot[d] - acum[d])], axis=0), emats[d])
                   for d in dirs]
        xd = [x[d] * dt_x[d] for d in dirs]
        lhs, rhs = [], []
        for d in dirs:
            mixes, parts = [], []
            for j in range(4):
                ln = lanes0[d] + j
                dd = acum[d][:, ln:ln + 1] - acum_t[d][j:j + 1, :]
                mixes.append((sc[d] * jnp.exp(jnp.where(masks[d], dd, NEG))).astype(BF16))
                parts.append(jnp.where(hid == j, xd[d], 0.0).astype(BF16))
            lhs.append(jnp.concatenate(mixes, axis=1))
            rhs.append(jnp.concatenate(parts, axis=0))
        y = [_dot(lhs[d], rhs[d]) for d in dirs]
        upd = [_dot_tn(bb[d].astype(BF16), (xd[d] * decay_x[d][ch:]).astype(BF16)) for d in dirs]
        for d in dirs:
            ea_x = decay_x[d][:ch]
            etot_x = ea_x[tot_rows[d]:tot_rows[d] + 1, :]
            y_refs[d][pl.ds(bases[d], ch), :] = y[d] + y_off[d] * ea_x
            st_refs[d][...] = st[d] * etot_x + upd[d]
        return carry

    lax.fori_loop(0, nch, scan_body, 0)

    def fin_body(c, carry):
        base = pl.multiple_of(c * ch, ch)
        y = yf_scr[pl.ds(base, ch), :] + yb_scr[pl.ds(base, ch), :] + xc_scr[pl.ds(base, ch), :] * dsk_x
        y = y * _silu(z_ref[0, pl.ds(base, ch), :])
        out_ref[0, pl.ds(base, ch), :] = _rms(y, nw_ref[...])
        return carry

    lax.fori_loop(0, nch, fin_body, 0)


def _ssd_mixer(proj3, small3, conv_w, conv_b, hp, norm_w):
    b, length, _ = proj3.shape
    gw = 4 * SSD_HEAD_DIM
    xs0 = SSD_INNER // gw
    b0 = 2 * SSD_INNER // SSD_STATE
    c0 = b0 + SSD_GROUPS
    nb_x = SSD_INNER // SSD_STATE
    seq = lambda w, off: pl.BlockSpec((1, length, w), lambda i, g: (i, 0, off + g))
    par = lambda r, w, off: pl.BlockSpec((r, w), lambda i, g: (0, off + g))
    return pl.pallas_call(
        _ssd_kernel,
        grid=(b, SSD_GROUPS),
        in_specs=[
            seq(gw, 0), seq(gw, xs0), seq(SSD_STATE, b0), seq(SSD_STATE, c0),
            pl.BlockSpec((1, length, LANES), lambda i, g: (i, 0, 0)),
            par(CONV_K, gw, 0), par(CONV_K, SSD_STATE, nb_x), par(CONV_K, SSD_STATE, nb_x + SSD_GROUPS),
            par(1, gw, 0), par(1, SSD_STATE, nb_x), par(1, SSD_STATE, nb_x + SSD_GROUPS),
            pl.BlockSpec((SUBLANES, LANES), lambda i, g: (0, 0)),
            par(1, gw, 0),
        ],
        out_specs=pl.BlockSpec((1, length, gw), lambda i, g: (i, 0, g)),
        out_shape=jax.ShapeDtypeStruct((b, length, SSD_INNER), F32),
        scratch_shapes=[
            pltpu.VMEM((length, gw), F32), pltpu.VMEM((length, SSD_STATE), F32),
            pltpu.VMEM((length, SSD_STATE), F32), pltpu.VMEM((length, gw), F32), pltpu.VMEM((length, gw), F32),
            pltpu.VMEM((SSD_STATE, gw), F32), pltpu.VMEM((SSD_STATE, gw), F32),
        ],
        compiler_params=_cparams(("parallel", "arbitrary")),
        name="ssd_mixer",
    )(proj3, proj3, proj3, proj3, small3, conv_w, conv_w, conv_w, conv_b, conv_b, conv_b, hp, norm_w)


def _gdn_kernel(q_ref, k_ref, v_ref, z_ref, sm_ref, cwq_ref, cwk_ref, cwv_ref, hp_ref, nw_ref, out_ref,
                qn_scr, kn_scr, vn_scr, u_scr, wq_scr, qk_scr, kd_scr, et_scr, o_scr):
    ch = GDN_CHUNK
    length = z_ref.shape[1]
    nch = length // ch
    dk = GDN_HEAD_DIM
    head = pl.program_id(1)
    shift = lax.rem(LANES - head, LANES)
    lane_a = 2 * SSD_HEADS
    lane_beta = lane_a + 2 * GDN_HEADS
    hp = pltpu.roll(hp_ref[...], shift, 1)
    bias_row = hp[0:1, :]
    acoef_row = jnp.exp(hp[1:2, :])

    row = lax.broadcasted_iota(I32, (ch, ch), 0)
    col = lax.broadcasted_iota(I32, (ch, ch), 1)
    eye = jnp.where(row == col, 1.0, 0.0)
    n_double = (GDN_INV_BLOCK - 1).bit_length() - 1
    levels = []
    size = GDN_INV_BLOCK
    inside = (row // size) == (col // size)
    levels.append(inside)
    while size < ch:
        size *= 2
        merged = (row // size) == (col // size)
        levels.append(merged & jnp.logical_not(inside))
        inside = merged

    def conv_body(c, carry):
        base = pl.multiple_of(c * ch, ch)
        q = _conv_silu_chunk(q_ref, cwq_ref, None, c, nch, ch)
        k = _conv_silu_chunk(k_ref, cwk_ref, None, c, nch, ch)
        qn_scr[pl.ds(base, ch), :] = q * (lax.rsqrt(jnp.sum(q * q, axis=-1, keepdims=True) + EPS) * dk ** -0.5)
        kn_scr[pl.ds(base, ch), :] = k * lax.rsqrt(jnp.sum(k * k, axis=-1, keepdims=True) + EPS)
        vn_scr[pl.ds(base, ch), :] = _conv_silu_chunk(v_ref, cwv_ref, None, c, nch, ch)
        return carry

    lax.fori_loop(0, nch, conv_body, 0)

    incl = ((row >= col), (row <= col))
    strict = ((row > col), (row < col))
    lmat = tuple(jnp.where(m, 1.0, 0.0).astype(BF16) for m in incl)
    umat = tuple(jnp.where(m, 1.0, 0.0) for m in strict)
    jref = (0, ch - 1)
    tot_row = (ch - 1, 0)
    group = min(GDN_PREP_CHUNKS, nch)

    def prep_body(gi, carry):
        chains = []
        for ci in range(group):
            c = gi * group + ci
            base = pl.multiple_of(c * ch, ch)
            q = qn_scr[pl.ds(base, ch), :]
            k = kn_scr[pl.ds(base, ch), :]
            v = vn_scr[pl.ds(base, ch), :]
            sm = pltpu.roll(sm_ref[0, pl.ds(base, ch), :], shift, 1)
            g_all = -acoef_row * _softplus(sm + bias_row)
            beta_all = jax.nn.sigmoid(sm)
            kbf = k.astype(BF16)
            kq = _dot_nt(jnp.concatenate([kbf, q.astype(BF16)], axis=0), kbf)
            for di in (0, 1):
                ln = lane_a + di * GDN_HEADS
                lb = lane_beta + di * GDN_HEADS
                chains.append(dict(c=c, di=di, q=q, k=k, v=v, kq=kq, g=g_all[:, ln:ln + 1],
                                   beta=beta_all[:, lb:lb + 1]))
        for t in chains:
            t["dmat"] = _dot_exact_lhs(lmat[t["di"]], t["g"] * umat[t["di"]], terms=2)
        for t in chains:
            di = t["di"]
            t["kb"] = t["k"] * t["beta"]
            dec = t["dmat"][:, jref[di]:jref[di] + 1] + t["g"][jref[di]:jref[di] + 1, :]
            t["dec"] = dec
            t["tot"] = dec[tot_row[di]:tot_row[di] + 1, :]
            gam = jnp.exp(jnp.where(incl[di], t["dmat"], NEG))
            t["qk"] = (t["kq"][ch:] * gam).astype(BF16)
            a = jnp.where(strict[di], t["kq"][:ch] * (gam * t["beta"]), 0.0)
            ad = jnp.where(levels[0], a, 0.0)
            t["p"] = eye - ad
            t["ab"] = ad.astype(BF16)
            t["off"] = [jnp.where(m, a, 0.0).astype(BF16) for m in levels[1:]]
        for t in chains:
            t["x"] = _dot(t["ab"], t["ab"])
        for i in range(n_double):
            last = i == n_double - 1
            for t in chains:
                xb = t["x"].astype(BF16)
                pb = t["p"].astype(BF16)
                t["r"] = _dot(xb, pb) if last else _dot(xb, jnp.concatenate([pb, xb], axis=1))
            for t in chains:
                t["p"] = t["p"] + t["r"][:, :ch]
                if not last:
                    t["x"] = t["r"][:, ch:]
        for lv in range(len(levels) - 1):
            for t in chains:
                t["m"] = _dot(t["off"][lv], t["p"].astype(BF16))
            for t in chains:
                t["p"] = t["p"] - _dot(t["p"].astype(BF16), t["m"].astype(BF16))
        for t in chains:
            t["edec"] = jnp.exp(t["dec"])
            rhs = jnp.concatenate([t["v"] * t["beta"], t["kb"] * t["edec"]], axis=1).astype(BF16)
            t["uw"] = _dot(t["p"].astype(BF16), rhs)
        for t in chains:
            c, di = t["c"], t["di"]
            base = pl.multiple_of(c * ch, ch)
            base2 = pl.multiple_of(c * 2 * ch, 2 * ch)
            u_scr[di, pl.ds(base, ch), :] = t["uw"][:, :dk]
            wq_scr[di, pl.ds(base2, 2 * ch), :] = jnp.concatenate(
                [t["uw"][:, dk:], t["q"] * t["edec"]], axis=0).astype(BF16)
            qk_scr[di, pl.ds(base, ch), :] = t["qk"]
            kd_scr[di, pl.ds(base, ch), :] = (t["k"] * jnp.exp(t["tot"] - t["dec"])).astype(BF16)
            et_scr[di, pl.ds(pl.multiple_of(c * SUBLANES, SUBLANES), SUBLANES), :] = jnp.broadcast_to(
                jnp.exp(t["tot"]), (SUBLANES, LANES))
        return carry

    lax.fori_loop(0, nch // group, prep_body, 0)

    def rec_body(ci, carry):
        cs = (ci, nch - 1 - ci)
        bases = [pl.multiple_of(c * ch, ch) for c in cs]
        ws = [_dot(wq_scr[di, pl.ds(pl.multiple_of(cs[di] * 2 * ch, 2 * ch), 2 * ch), :], carry[di].astype(BF16))
              for di in (0, 1)]
        v_new = [(u_scr[di, pl.ds(bases[di], ch), :] - ws[di][:ch]).astype(BF16) for di in (0, 1)]
        upd = [_dot_tn(kd_scr[di, pl.ds(bases[di], ch), :], v_new[di]) for di in (0, 1)]
        intra = [_dot(qk_scr[di, pl.ds(bases[di], ch), :], v_new[di]) for di in (0, 1)]
        s_new = []
        for di in (0, 1):
            et = et_scr[di, pl.ds(pl.multiple_of(cs[di] * SUBLANES, SUBLANES), 1), :]
            s_new.append(carry[di] * et + upd[di])
            o_scr[di, pl.ds(bases[di], ch), :] = ws[di][ch:] + intra[di]
        return tuple(s_new)

    zero = jnp.zeros((dk, dk), F32)
    lax.fori_loop(0, nch, rec_body, (zero, zero))

    def fin_body(c, carry):
        base = pl.multiple_of(c * ch, ch)
        o = o_scr[0, pl.ds(base, ch), :] + o_scr[1, pl.ds(base, ch), :]
        out_ref[0, pl.ds(base, ch), :] = _rms(o, nw_ref[...]) * _silu(z_ref[0, pl.ds(base, ch), :])
        return carry

    lax.fori_loop(0, nch, fin_body, 0)


def _gdn_mixer(proj3, small3, conv_w, hp, norm_w):
    b, length, _ = proj3.shape
    dk = GDN_HEAD_DIM
    nch = length // GDN_CHUNK
    q0 = (2 * SSD_INNER + 2 * SSD_GROUPS * SSD_STATE) // dk
    seq = lambda off: pl.BlockSpec((1, length, dk), lambda i, h: (i, 0, off + h))
    par = lambda off: pl.BlockSpec((CONV_K, dk), lambda i, h: (0, off + h))
    return pl.pallas_call(
        _gdn_kernel,
        grid=(b, GDN_HEADS),
        in_specs=[
            seq(q0), seq(q0 + GDN_HEADS), seq(q0 + 2 * GDN_HEADS), seq(q0 + 3 * GDN_HEADS),
            pl.BlockSpec((1, length, LANES), lambda i, h: (i, 0, 0)),
            par(0), par(GDN_HEADS), par(2 * GDN_HEADS),
            pl.BlockSpec((SUBLANES, LANES), lambda i, h: (0, 0)),
            pl.BlockSpec((1, dk), lambda i, h: (0, 0)),
        ],
        out_specs=pl.BlockSpec((1, length, dk), lambda i, h: (i, 0, h)),
        out_shape=jax.ShapeDtypeStruct((b, length, GDN_INNER), F32),
        scratch_shapes=[
            pltpu.VMEM((length, dk), F32), pltpu.VMEM((length, dk), F32), pltpu.VMEM((length, dk), F32),
            pltpu.VMEM((2, length, dk), F32),
            pltpu.VMEM((2, 2 * length, dk), BF16),
            pltpu.VMEM((2, length, GDN_CHUNK), BF16),
            pltpu.VMEM((2, length, dk), BF16),
            pltpu.VMEM((2, nch * SUBLANES, LANES), F32),
            pltpu.VMEM((2, length, dk), F32),
        ],
        compiler_params=_cparams(("parallel", "arbitrary")),
        name="gdn_mixer",
    )(proj3, proj3, proj3, proj3, small3, conv_w, conv_w, conv_w, hp, norm_w)


def _merge_kernel(x_ref, ya_ref, yb_ref, ga_ref, gb_ref, wa_ref, wb_ref, wo_ref, gf_ref, h_ref, n_ref):
    ya = _dot(ya_ref[...].astype(BF16), wa_ref[...])
    yb = _dot(yb_ref[...].astype(BF16), wb_ref[...])
    merged = jax.nn.sigmoid(ga_ref[...]) * ya + jax.nn.sigmoid(gb_ref[...]) * yb
    h = x_ref[...] + _dot(merged.astype(BF16), wo_ref[...])
    h_ref[...] = h
    n_ref[...] = _rms(h, gf_ref[...]).astype(BF16)


def _merge(x2d, ya, yb, proj2d, wa, wb, wo, gf):
    t, d = x2d.shape
    tm = min(TM_MERGE, t)
    ga0 = (proj2d.shape[1] - 2 * d) // d
    tok = lambda: pl.BlockSpec((tm, d), lambda i: (i, 0))
    wsp = lambda: pl.BlockSpec((d, d), lambda i: (0, 0))
    return pl.pallas_call(
        _merge_kernel,
        grid=(t // tm,),
        in_specs=[tok(), tok(), tok(),
                  pl.BlockSpec((tm, d), lambda i: (i, ga0)), pl.BlockSpec((tm, d), lambda i: (i, ga0 + 1)),
                  wsp(), wsp(), wsp(), pl.BlockSpec((1, d), lambda i: (0, 0))],
        out_specs=[tok(), tok()],
        out_shape=[jax.ShapeDtypeStruct((t, d), F32), jax.ShapeDtypeStruct((t, d), BF16)],
        compiler_params=_cparams(("parallel",)),
        name="merge",
    )(x2d, ya, yb, proj2d, proj2d, wa, wb, wo, gf)


def _topk_store(problems, k):
    state = [p[0] for p in problems]
    iotas = [lax.broadcasted_iota(I32, s.shape, 0).astype(F32) for s in state]
    for i in range(k):
        ms = [jnp.max(s, axis=0, keepdims=True) for s in state]
        cands = [jnp.where(s == m, io, float(s.shape[0])) for s, m, io in zip(state, ms, iotas)]
        ixs = [jnp.min(c, axis=0, keepdims=True) for c in cands]
        hits = [c == ix for c, ix in zip(cands, ixs)]
        for (_, vals_ref, outs_ref, payload), m, ix, hit in zip(problems, ms, ixs, hits):
            vals_ref[i:i + 1, :] = m
            outs_ref[i:i + 1, :] = (ix if payload is None
                                    else jnp.max(jnp.where(hit, payload, -1.0), axis=0, keepdims=True))
        state = [jnp.where(hit, -jnp.inf, s) for s, hit in zip(state, hits)]


def _route_unit(tok_ref, wq_ref, keys_ref, e_scr, gt_scr, tv_scr, ti_scr, bv_scr, be_scr, head, tile):
    kk = PEER_TOPK
    half = kk // 2
    toks = tok_ref[pl.ds(pl.multiple_of(tile * LANES, LANES), LANES), :]
    qry = _dot(toks, wq_ref[...])
    st = [_dot_nt(keys_ref[0, z], qry[:, z * PEER_HALF:(z + 1) * PEER_HALF].astype(BF16)) for z in range(2)]
    _topk_store([(st[z], tv_scr.at[z], ti_scr.at[z], None) for z in range(2)], kk)
    s0, s1 = tv_scr[0], tv_scr[1]
    e0, i1 = ti_scr[0] * float(PEER_KEYS), ti_scr[1]
    cs = [s0[0:1] + s1] + [s0[i:i + 1] + s1[0:half] for i in range(1, half)] + [s0[half:] + s1[0:1]]
    ce = [e0[0:1] + i1] + [e0[i:i + 1] + i1[0:half] for i in range(1, half)] + [e0[half:] + i1[0:1]]
    _topk_store([(jnp.concatenate(cs, axis=0), bv_scr, be_scr, jnp.concatenate(ce, axis=0))], kk)
    best = bv_scr[...]
    ex = jnp.exp(best - best[0:1])
    rows = pl.ds(pl.multiple_of(head * kk, kk), kk)
    lanes = pl.ds(pl.multiple_of(tile * LANES, LANES), LANES)
    e_scr[rows, lanes] = be_scr[...]
    gt_scr[rows, lanes] = ex / jnp.sum(ex, axis=0, keepdims=True)


def _peer_kernel(nm_ref, nr_ref, wq_ref, keys_ref, u_ref, v_ref, out_ref,
                 g_scr, e_scr, gt_scr, et_scr, gtt_scr, tv_scr, ti_scr, bv_scr, be_scr):
    tm = nm_ref.shape[0]
    nk = PEER_KEYS
    ntile = tm // LANES
    tile_i = pl.program_id(0)
    step = pl.program_id(1)

    @pl.when((tile_i == 0) & (step == 0))
    def _():
        e_scr[...] = jnp.zeros_like(e_scr)
        gt_scr[...] = jnp.zeros_like(gt_scr)

    @pl.when(step == 0)
    def _():
        out_ref[...] = jnp.zeros_like(out_ref)
        et_scr[...] = e_scr[...].T.astype(I32)
        gtt_scr[...] = gt_scr[...].T
        sub = lax.broadcasted_iota(I32, (nk, LANES), 0)

        zero = jnp.zeros((nk, LANES), BF16)

        def pair_body(tp, carry):
            t0 = 2 * tp
            ats, bts = [], []
            for o in range(2):
                re = et_scr[pl.ds(t0 + o, 1), :]
                rg = 0.5 * gtt_scr[pl.ds(t0 + o, 1), :]
                ats.append(jnp.where(sub == (re >> 7), rg, 0.0).astype(BF16))
                bts.append(jnp.where(sub == (re & (nk - 1)), 1.0, 0.0).astype(BF16))
            bt = jnp.concatenate([jnp.concatenate([bts[0], zero], axis=1),
                                  jnp.concatenate([zero, bts[1]], axis=1)], axis=0)
            g2 = _dot_nt(jnp.concatenate(ats, axis=1), bt)
            for o in range(2):
                g_scr[pl.ds(t0 + o, nk, stride=G_PITCH), :] = g2[:, o * LANES:(o + 1) * LANES]
            return carry

        lax.fori_loop(0, tm // 2, pair_body, 0, unroll=TOKEN_UNROLL // 2)

    per_step = PEER_HEADS * ntile * EXP_ROWS // nk
    for k in range(per_step):
        unit = step * per_step + k
        _route_unit(nr_ref, wq_ref, keys_ref, e_scr, gt_scr, tv_scr, ti_scr, bv_scr, be_scr,
                    unit // ntile, lax.rem(unit, ntile))

    hid = _dot_nt(nm_ref[...], u_ref[...])
    act = hid * (1.0 + lax.erf(hid * (0.5 ** 0.5)))
    gates = [g_scr[pl.ds(pl.multiple_of((step * EXP_ROWS + r) * G_PITCH, SUBLANES), tm), :]
             for r in range(EXP_ROWS)]
    wts = (act * jnp.concatenate(gates, axis=1)).astype(BF16)
    out_ref[...] = out_ref[...] + _dot(wts, v_ref[...])


def _peer(n2, wq, keys, eu, ev):
    t, d = n2.shape
    tm = min(TM_EXP, t)
    assert tm == TM_EXP, "gate slab pitch is derived from TM_EXP"
    assert PEER_KEYS == LANES, "expert ids are decoded with a 7-bit shift"
    hk = PEER_HEADS * PEER_TOPK
    er = EXP_ROWS * PEER_KEYS
    nsteps = eu.shape[0] // er
    nt = t // tm
    units = PEER_HEADS * (tm // LANES)
    assert units % nsteps == 0 and (tm // LANES) % (units // nsteps) == 0, "whole routing units of one head per step"
    spr = nsteps // PEER_HEADS
    qw = 2 * PEER_HALF
    prev = lambda i, s: (jnp.maximum(i - 1, 0), 0)
    return pl.pallas_call(
        _peer_kernel,
        grid=(nt + 1, nsteps),
        in_specs=[pl.BlockSpec((tm, d), prev),
                  pl.BlockSpec((tm, d), lambda i, s: (jnp.minimum(i, nt - 1), 0)),
                  pl.BlockSpec((d, qw), lambda i, s: (0, s // spr)),
                  pl.BlockSpec((1, 2, PEER_KEYS, PEER_HALF), lambda i, s: (s // spr, 0, 0, 0)),
                  pl.BlockSpec((er, d), lambda i, s: (s, 0)),
                  pl.BlockSpec((er, d), lambda i, s: (s, 0))],
        out_specs=pl.BlockSpec((tm, d), prev),
        out_shape=jax.ShapeDtypeStruct((t, d), F32),
        scratch_shapes=[pltpu.VMEM((PEER_KEYS * G_PITCH, LANES), F32),
                        pltpu.VMEM((hk, tm), F32), pltpu.VMEM((hk, tm), F32),
                        pltpu.VMEM((tm, hk), I32), pltpu.VMEM((tm, hk), F32),
                        pltpu.VMEM((2, PEER_TOPK, LANES), F32), pltpu.VMEM((2, PEER_TOPK, LANES), F32),
                        pltpu.VMEM((PEER_TOPK, LANES), F32), pltpu.VMEM((PEER_TOPK, LANES), F32)],
        compiler_params=_cparams(("arbitrary", "arbitrary")),
        name="peer_ffn",
    )(n2, n2, wq, keys, eu, ev)


def _ple_kernel(h_ref, f_ref, p_ref, gp_ref, wg_ref, wp_ref, gf_ref, y_ref):
    h = h_ref[...] + f_ref[...]
    gate = jax.nn.sigmoid(_dot(_rms(h, gp_ref[...]).astype(BF16), wg_ref[...]))
    h = h + gate * _dot(p_ref[...].astype(BF16), wp_ref[...])
    y_ref[...] = _rms(h, gf_ref[...])


def _ple(h1, ffn, p2d, gp, wg, wp, gfin):
    t, d = h1.shape
    pd = p2d.shape[1]
    tm = min(TM_PLE, t)
    vec = lambda: pl.BlockSpec((1, d), lambda i: (0, 0))
    tok = lambda: pl.BlockSpec((tm, d), lambda i: (i, 0))
    return pl.pallas_call(
        _ple_kernel,
        grid=(t // tm,),
        in_specs=[tok(), tok(), pl.BlockSpec((tm, pd), lambda i: (i, 0)),
                  vec(), pl.BlockSpec((d, d), lambda i: (0, 0)), pl.BlockSpec((pd, d), lambda i: (0, 0)), vec()],
        out_specs=tok(),
        out_shape=jax.ShapeDtypeStruct((t, d), F32),
        compiler_params=_cparams(("parallel",)),
        name="ple_final",
    )(h1, ffn, p2d, gp, wg, wp, gfin)


def _pad_lanes(v):
    return jnp.pad(v, (0, LANES - v.shape[0]))


def _prepare(norm_mix, w_in, ssd_conv_w, ssd_conv_b, ssd_a_log, ssd_dt_bias, ssd_d, ssd_norm, w_ssd_out,
             gdn_conv_w, gdn_a_log, gdn_dt_bias, gdn_norm, w_gdn_out, w_out, norm_ffn, peer_query, peer_keys,
             expert_u, expert_v, norm_ple, w_ple_gate, w_ple_proj, norm_final):
    w = w_in[0]
    xbc = SSD_INNER + 2 * SSD_GROUPS * SSD_STATE
    c_dt = SSD_INNER + xbc
    c_qkv = c_dt + 2 * SSD_HEADS
    c_a = c_qkv + 4 * GDN_INNER
    c_ga = c_a + 4 * GDN_HEADS
    w_main = jnp.concatenate([w[:, :c_dt], w[:, c_qkv:c_a], w[:, c_ga:]], axis=1).astype(BF16)
    w_small = jnp.concatenate([w[:, c_dt:c_qkv], w[:, c_a:c_ga]], axis=1)
    w_small = jnp.pad(w_small, ((0, 0), (0, LANES - w_small.shape[1]))).astype(BF16)
    hp = jnp.stack([
        _pad_lanes(jnp.concatenate([ssd_dt_bias[0].reshape(-1), gdn_dt_bias[0].reshape(-1)])),
        _pad_lanes(jnp.concatenate([ssd_a_log[0].reshape(-1), gdn_a_log[0].reshape(-1)])),
        _pad_lanes(ssd_d[0]),
    ])
    hp = jnp.pad(hp, ((0, SUBLANES - hp.shape[0]), (0, 0)))
    row = lambda v: v.reshape(1, -1)
    return dict(
        g_mix=row(norm_mix[0]), w_main=w_main, w_small=w_small, hp=hp,
        ssd_conv_w=ssd_conv_w[0], ssd_conv_b=row(ssd_conv_b[0]), ssd_norm=row(ssd_norm[0]),
        gdn_conv_w=gdn_conv_w[0], gdn_norm=row(gdn_norm[0]),
        wa=w_ssd_out[0].astype(BF16), wb=w_gdn_out[0].astype(BF16), wo=w_out[0].astype(BF16),
        g_ffn=row(norm_ffn[0]), wq=peer_query[0].astype(BF16),
        keys=peer_keys[0].astype(BF16),
        eu=expert_u[0].astype(BF16), ev=expert_v[0].astype(BF16),
        g_ple=row(norm_ple[0]), wg=w_ple_gate[0].astype(BF16), wp=w_ple_proj[0].astype(BF16),
        g_fin=row(norm_final),
    )


def _trunk(x, ple, p):
    b, length, d = x.shape
    t = b * length
    x2d = x.reshape(t, d)
    proj, small = _in_proj(x2d, p["g_mix"], p["w_main"], p["w_small"])
    proj3 = proj.reshape(b, length, -1)
    small3 = small.reshape(b, length, LANES)
    ya = _ssd_mixer(proj3, small3, p["ssd_conv_w"], p["ssd_conv_b"], p["hp"], p["ssd_norm"])
    yb = _gdn_mixer(proj3, small3, p["gdn_conv_w"], p["hp"], p["gdn_norm"])
    h1, n2 = _merge(x2d, ya.reshape(t, -1), yb.reshape(t, -1), proj, p["wa"], p["wb"], p["wo"], p["g_ffn"])
    ffn = _peer(n2, p["wq"], p["keys"], p["eu"], p["ev"])
    y = _ple(h1, ffn, ple.reshape(t, -1), p["g_ple"], p["wg"], p["wp"], p["g_fin"])
    return y.reshape(b, length, d)


def kernel(x_prompt, x_sample, p_prompt, p_sample, norm_mix, w_in, ssd_conv_w, ssd_conv_b, ssd_a_log, ssd_dt_bias, ssd_d, ssd_norm, w_ssd_out, gdn_conv_w, gdn_a_log, gdn_dt_bias, gdn_norm, w_gdn_out, w_out, norm_ffn, peer_query, peer_keys, expert_u, expert_v, norm_ple, w_ple_gate, w_ple_proj, norm_final):
    assert w_in.shape[0] == 1, "single-layer trunk"
    p = _prepare(norm_mix, w_in, ssd_conv_w, ssd_conv_b, ssd_a_log, ssd_dt_bias, ssd_d, ssd_norm, w_ssd_out,
                 gdn_conv_w, gdn_a_log, gdn_dt_bias, gdn_norm, w_gdn_out, w_out, norm_ffn, peer_query, peer_keys,
                 expert_u, expert_v, norm_ple, w_ple_gate, w_ple_proj, norm_final)
    return (_trunk(x_prompt, p_prompt[0], p), _trunk(x_sample, p_sample[0], p))
```

```python
import jax
import jax.numpy as jnp
from jax import lax
from jax.experimental import pallas as pl
from jax.experimental.pallas import tpu as pltpu

F32 = jnp.float32
BF16 = jnp.bfloat16
I32 = jnp.int32

EPS = 1e-6
CONV_K = 4
SSD_HEADS = 16
SSD_HEAD_DIM = 64
SSD_GROUPS = 4
SSD_STATE = 128
SSD_INNER = SSD_HEADS * SSD_HEAD_DIM
GDN_HEADS = 8
GDN_HEAD_DIM = 128
GDN_INNER = GDN_HEADS * GDN_HEAD_DIM
PEER_HEADS = 8
PEER_KEYS = 128
PEER_TOPK = 16
PEER_HALF = 128

LANES = 128
SUBLANES = 8
SSD_CHUNK = 128
GDN_CHUNK = 128
GDN_PREP_CHUNKS = 8
GDN_INV_BLOCK = 8
VMEM_LIMIT = 56 * 1024 * 1024

TM_PROJ = 1024
TN_PROJ = 1024
TM_MERGE = 256
TM_EXP = 512
EXP_ROWS = 16
G_PITCH = TM_EXP // 2 + 8
TOKEN_UNROLL = 32
TM_PLE = 512

NEG = -1e30


def _dot(a, b):
    return jnp.dot(a, b, preferred_element_type=F32)


def _dot_nt(a, b):
    return lax.dot_general(a, b, (((1,), (1,)), ((), ())), preferred_element_type=F32)


def _dot_tn(a, b):
    return lax.dot_general(a, b, (((0,), (0,)), ((), ())), preferred_element_type=F32)


def _bf16_terms(m, terms):
    out = []
    for _ in range(terms - 1):
        t = m.astype(BF16)
        out.append(t)
        m = m - t.astype(F32)
    out.append(m.astype(BF16))
    return out


def _dot_exact_lhs(lb, m, terms=3):
    n = m.shape[1]
    r = _dot(lb, jnp.concatenate(_bf16_terms(m, terms), axis=1))
    return sum(r[:, i * n:(i + 1) * n] for i in range(1, terms)) + r[:, :n]


def _dot_exact_rhs(m, eb, terms=2):
    return _dot(jnp.concatenate(_bf16_terms(m, terms), axis=1), jnp.concatenate([eb] * terms, axis=0))


def _rows_transposed(sel, m, terms=3):
    n = m.shape[0]
    r = _dot_nt(sel, jnp.concatenate(_bf16_terms(m, terms), axis=0))
    return sum(r[:, i * n:(i + 1) * n] for i in range(1, terms)) + r[:, :n]


def _softplus(x):
    return jnp.maximum(x, 0.0) + jnp.log1p(jnp.exp(-jnp.abs(x)))


def _silu(x):
    return x * jax.nn.sigmoid(x)


def _rms(x, g):
    return x * lax.rsqrt(jnp.mean(x * x, axis=-1, keepdims=True) + EPS) * g


def _cparams(sem):
    return pltpu.CompilerParams(dimension_semantics=sem, vmem_limit_bytes=VMEM_LIMIT)


def _in_proj_kernel(x_ref, g_ref, wm_ref, ws_ref, main_ref, small_ref, n_scr):
    @pl.when(pl.program_id(1) == 0)
    def _():
        n = _rms(x_ref[...], g_ref[...]).astype(BF16)
        n_scr[...] = n
        small_ref[...] = _dot(n, ws_ref[...])

    main_ref[...] = _dot(n_scr[...], wm_ref[...])


def _in_proj(x2d, g, w_main, w_small):
    t, d = x2d.shape
    n_main = w_main.shape[1]
    tm = min(TM_PROJ, t)
    return pl.pallas_call(
        _in_proj_kernel,
        grid=(t // tm, n_main // TN_PROJ),
        in_specs=[
            pl.BlockSpec((tm, d), lambda i, j: (i, 0)),
            pl.BlockSpec((1, d), lambda i, j: (0, 0)),
            pl.BlockSpec((d, TN_PROJ), lambda i, j: (0, j)),
            pl.BlockSpec((d, LANES), lambda i, j: (0, 0)),
        ],
        out_specs=[
            pl.BlockSpec((tm, TN_PROJ), lambda i, j: (i, j)),
            pl.BlockSpec((tm, LANES), lambda i, j: (i, 0)),
        ],
        out_shape=[jax.ShapeDtypeStruct((t, n_main), F32), jax.ShapeDtypeStruct((t, LANES), F32)],
        scratch_shapes=[pltpu.VMEM((tm, d), BF16)],
        compiler_params=_cparams(("parallel", "arbitrary")),
        name="in_proj",
    )(x2d, g, w_main, w_small)


def _conv_silu_chunk(src_ref, w_ref, b_ref, c, nch, rows):
    length = src_ref.shape[1]
    base = pl.multiple_of(c * rows, rows)
    main = src_ref[0, pl.ds(base, rows), :]
    prev = src_ref[0, pl.ds(pl.multiple_of(jnp.maximum(base - SUBLANES, 0), SUBLANES), SUBLANES), :]
    nxt = src_ref[0, pl.ds(pl.multiple_of(jnp.minimum(base + rows, length - SUBLANES), SUBLANES), SUBLANES), :]
    prev = jnp.where(c > 0, prev, 0.0)
    nxt = jnp.where(c < nch - 1, nxt, 0.0)
    ext = jnp.concatenate([prev, main, nxt], axis=0)
    w = w_ref[...]
    lo = SUBLANES - CONV_K // 2
    y = ext[lo:lo + rows] * w[0:1]
    for j in range(1, CONV_K):
        y = y + ext[lo + j:lo + j + rows] * w[j:j + 1]
    if b_ref is not None:
        y = y + b_ref[...]
    return _silu(y)


def _expand_heads(t, lane0, width):
    r = t.shape[0]
    hid = lax.broadcasted_iota(I32, (r, 4 * width), 1) // width
    out = jnp.broadcast_to(t[:, lane0 + 3:lane0 + 4], (r, 4 * width))
    for j in (2, 1, 0):
        out = jnp.where(hid == j, t[:, lane0 + j:lane0 + j + 1], out)
    return out


def _ssd_kernel(z_ref, xs_ref, bm_ref, cm_ref, sm_ref, cwx_ref, cwb_ref, cwc_ref, cbx_ref, cbb_ref, cbc_ref,
                hp_ref, nw_ref, out_ref, xc_scr, bc_scr, cc_scr, yf_scr, yb_scr, stf_scr, stb_scr):
    ch = SSD_CHUNK
    length = z_ref.shape[1]
    nch = length // ch
    gwid = 4 * SSD_HEAD_DIM
    grp = pl.program_id(1)
    shift = lax.rem(LANES - 4 * grp, LANES)
    hp = pltpu.roll(hp_ref[...], shift, 1)
    bias_row = hp[0:1, :]
    a_row = -jnp.exp(hp[1:2, :])
    dsk_x = _expand_heads(hp[2:3, :], 0, SSD_HEAD_DIM)

    row = lax.broadcasted_iota(I32, (ch, ch), 0)
    col = lax.broadcasted_iota(I32, (ch, ch), 1)
    hid = lax.broadcasted_iota(I32, (ch, gwid), 1) // SSD_HEAD_DIM

    def conv_body(c, carry):
        base = pl.multiple_of(c * ch, ch)
        xc_scr[pl.ds(base, ch), :] = _conv_silu_chunk(xs_ref, cwx_ref, cbx_ref, c, nch, ch)
        bc_scr[pl.ds(base, ch), :] = _conv_silu_chunk(bm_ref, cwb_ref, cbb_ref, c, nch, ch)
        cc_scr[pl.ds(base, ch), :] = _conv_silu_chunk(cm_ref, cwc_ref, cbc_ref, c, nch, ch)
        return carry

    lax.fori_loop(0, nch, conv_body, 0)

    masks = ((row >= col), (row <= col))
    lmats = tuple(jnp.where(m, 1.0, 0.0).astype(BF16) for m in masks)
    lanes0 = (0, SSD_HEADS)
    tot_rows = (ch - 1, 0)
    erow = lax.broadcasted_iota(I32, (LANES, gwid), 0)
    ecol = lax.broadcasted_iota(I32, (LANES, gwid), 1) // SSD_HEAD_DIM
    emats = tuple(jnp.where(erow == ecol + l0, 1.0, 0.0).astype(BF16) for l0 in lanes0)
    srow = lax.broadcasted_iota(I32, (SUBLANES, LANES), 0)
    scol = lax.broadcasted_iota(I32, (SUBLANES, LANES), 1)
    sels = tuple(jnp.where(scol == srow + l0, 1.0, 0.0).astype(BF16) for l0 in lanes0)
    st_refs = (stf_scr, stb_scr)
    y_refs = (yf_scr, yb_scr)
    dirs = (0, 1)

    stf_scr[...] = jnp.zeros_like(stf_scr)
    stb_scr[...] = jnp.zeros_like(stb_scr)

    def scan_body(ci, carry):
        bases = (pl.multiple_of(ci * ch, ch), pl.multiple_of((nch - 1 - ci) * ch, ch))
        x = [xc_scr[pl.ds(b, ch), :] for b in bases]
        bb = [bc_scr[pl.ds(b, ch), :] for b in bases]
        cb = [cc_scr[pl.ds(b, ch), :].astype(BF16) for b in bases]
        dt_all = [_softplus(pltpu.roll(sm_ref[0, pl.ds(b, ch), :], shift, 1) + bias_row) for b in bases]
        acum = [_dot_exact_lhs(lmats[d], dt_all[d] * a_row) for d in dirs]
        sc = [jnp.where(masks[d], _dot_nt(cb[d], bb[d].astype(BF16)), 0.0) for d in dirs]
        st = [st_refs[d][...] for d in dirs]
        y_off = [_dot(cb[d], st[d].astype(BF16)) for d in dirs]
        dt_x = [_dot_exact_rhs(dt_all[d], emats[d]) for d in dirs]
        acum_t = [_rows_transposed(sels[d], acum[d]) for d in dirs]
        tot = [acum[d][tot_rows[d]:tot_rows[d] + 1, :] for d in dirs]
        decay_x = [_dot_exact_rhs(jnp.concatenate([jnp.exp(acum[d]), jnp.exp(tot[d] - acum[d])], axis=0), emats[d])
                   for d in dirs]
        xd = [x[d] * dt_x[d] for d in dirs]
        lhs, rhs = [], []
        for d in dirs:
            mixes, parts = [], []
            for j in range(4):
                ln = lanes0[d] + j
                dd = acum[d][:, ln:ln + 1] - acum_t[d][j:j + 1, :]
                mixes.append((sc[d] * jnp.exp(jnp.where(masks[d], dd, NEG))).astype(BF16))
                parts.append(jnp.where(hid == j, xd[d], 0.0).astype(BF16))
            lhs.append(jnp.concatenate(mixes, axis=1))
            rhs.append(jnp.concatenate(parts, axis=0))
        y = [_dot(lhs[d], rhs[d]) for d in dirs]
        upd = [_dot_tn(bb[d].astype(BF16), (xd[d] * decay_x[d][ch:]).astype(BF16)) for d in dirs]
        for d in dirs:
            ea_x = decay_x[d][:ch]
            etot_x = ea_x[tot_rows[d]:tot_rows[d] + 1, :]
            y_refs[d][pl.ds(bases[d], ch), :] = y[d] + y_off[d] * ea_x
            st_refs[d][...] = st[d] * etot_x + upd[d]
        return carry

    lax.fori_loop(0, nch, scan_body, 0)

    def fin_body(c, carry):
        base = pl.multiple_of(c * ch, ch)
        y = yf_scr[pl.ds(base, ch), :] + yb_scr[pl.ds(base, ch), :] + xc_scr[pl.ds(base, ch), :] * dsk_x
        y = y * _silu(z_ref[0, pl.ds(base, ch), :])
        out_ref[0, pl.ds(base, ch), :] = _rms(y, nw_ref[...])
        return carry

    lax.fori_loop(0, nch, fin_body, 0)


def _ssd_mixer(proj3, small3, conv_w, conv_b, hp, norm_w):
    b, length, _ = proj3.shape
    gw = 4 * SSD_HEAD_DIM
    xs0 = SSD_INNER // gw
    b0 = 2 * SSD_INNER // SSD_STATE
    c0 = b0 + SSD_GROUPS
    nb_x = SSD_INNER // SSD_STATE
    seq = lambda w, off: pl.BlockSpec((1, length, w), lambda i, g: (i, 0, off + g))
    par = lambda r, w, off: pl.BlockSpec((r, w), lambda i, g: (0, off + g))
    return pl.pallas_call(
        _ssd_kernel,
        grid=(b, SSD_GROUPS),
        in_specs=[
            seq(gw, 0), seq(gw, xs0), seq(SSD_STATE, b0), seq(SSD_STATE, c0),
            pl.BlockSpec((1, length, LANES), lambda i, g: (i, 0, 0)),
            par(CONV_K, gw, 0), par(CONV_K, SSD_STATE, nb_x), par(CONV_K, SSD_STATE, nb_x + SSD_GROUPS),
            par(1, gw, 0), par(1, SSD_STATE, nb_x), par(1, SSD_STATE, nb_x + SSD_GROUPS),
            pl.BlockSpec((SUBLANES, LANES), lambda i, g: (0, 0)),
            par(1, gw, 0),
        ],
        out_specs=pl.BlockSpec((1, length, gw), lambda i, g: (i, 0, g)),
        out_shape=jax.ShapeDtypeStruct((b, length, SSD_INNER), F32),
        scratch_shapes=[
            pltpu.VMEM((length, gw), F32), pltpu.VMEM((length, SSD_STATE), F32),
            pltpu.VMEM((length, SSD_STATE), F32), pltpu.VMEM((length, gw), F32), pltpu.VMEM((length, gw), F32),
            pltpu.VMEM((SSD_STATE, gw), F32), pltpu.VMEM((SSD_STATE, gw), F32),
        ],
        compiler_params=_cparams(("parallel", "arbitrary")),
        name="ssd_mixer",
    )(proj3, proj3, proj3, proj3, small3, conv_w, conv_w, conv_w, conv_b, conv_b, conv_b, hp, norm_w)


def _gdn_kernel(q_ref, k_ref, v_ref, z_ref, sm_ref, cwq_ref, cwk_ref, cwv_ref, hp_ref, nw_ref, out_ref,
                qn_scr, kn_scr, vn_scr, u_scr, wq_scr, qk_scr, kd_scr, et_scr, o_scr):
    ch = GDN_CHUNK
    length = z_ref.shape[1]
    nch = length // ch
    dk = GDN_HEAD_DIM
    head = pl.program_id(1)
    shift = lax.rem(LANES - head, LANES)
    lane_a = 2 * SSD_HEADS
    lane_beta = lane_a + 2 * GDN_HEADS
    hp = pltpu.roll(hp_ref[...], shift, 1)
    bias_row = hp[0:1, :]
    acoef_row = jnp.exp(hp[1:2, :])

    row = lax.broadcasted_iota(I32, (ch, ch), 0)
    col = lax.broadcasted_iota(I32, (ch, ch), 1)
    eye = jnp.where(row == col, 1.0, 0.0)
    n_double = (GDN_INV_BLOCK - 1).bit_length() - 1
    levels = []
    size = GDN_INV_BLOCK
    inside = (row // size) == (col // size)
    levels.append(inside)
    while size < ch:
        size *= 2
        merged = (row // size) == (col // size)
        levels.append(merged & jnp.logical_not(inside))
        inside = merged

    def conv_body(c, carry):
        base = pl.multiple_of(c * ch, ch)
        q = _conv_silu_chunk(q_ref, cwq_ref, None, c, nch, ch)
        k = _conv_silu_chunk(k_ref, cwk_ref, None, c, nch, ch)
        qn_scr[pl.ds(base, ch), :] = q * (lax.rsqrt(jnp.sum(q * q, axis=-1, keepdims=True) + EPS) * dk ** -0.5)
        kn_scr[pl.ds(base, ch), :] = k * lax.rsqrt(jnp.sum(k * k, axis=-1, keepdims=True) + EPS)
        vn_scr[pl.ds(base, ch), :] = _conv_silu_chunk(v_ref, cwv_ref, None, c, nch, ch)
        return carry

    lax.fori_loop(0, nch, conv_body, 0)

    incl = ((row >= col), (row <= col))
    strict = ((row > col), (row < col))
    lmat = tuple(jnp.where(m, 1.0, 0.0).astype(BF16) for m in incl)
    umat = tuple(jnp.where(m, 1.0, 0.0) for m in strict)
    jref = (0, ch - 1)
    tot_row = (ch - 1, 0)
    group = min(GDN_PREP_CHUNKS, nch)

    def prep_body(gi, carry):
        chains = []
        for ci in range(group):
            c = gi * group + ci
            base = pl.multiple_of(c * ch, ch)
            q = qn_scr[pl.ds(base, ch), :]
            k = kn_scr[pl.ds(base, ch), :]
            v = vn_scr[pl.ds(base, ch), :]
            sm = pltpu.roll(sm_ref[0, pl.ds(base, ch), :], shift, 1)
            g_all = -acoef_row * _softplus(sm + bias_row)
            beta_all = jax.nn.sigmoid(sm)
            kbf = k.astype(BF16)
            kq = _dot_nt(jnp.concatenate([kbf, q.astype(BF16)], axis=0), kbf)
            for di in (0, 1):
                ln = lane_a + di * GDN_HEADS
                lb = lane_beta + di * GDN_HEADS
                chains.append(dict(c=c, di=di, q=q, k=k, v=v, kq=kq, g=g_all[:, ln:ln + 1],
                                   beta=beta_all[:, lb:lb + 1]))
        for t in chains:
            t["dmat"] = _dot_exact_lhs(lmat[t["di"]], t["g"] * umat[t["di"]], terms=2)
        for t in chains:
            di = t["di"]
            t["kb"] = t["k"] * t["beta"]
            dec = t["dmat"][:, jref[di]:jref[di] + 1] + t["g"][jref[di]:jref[di] + 1, :]
            t["dec"] = dec
            t["tot"] = dec[tot_row[di]:tot_row[di] + 1, :]
            gam = jnp.exp(jnp.where(incl[di], t["dmat"], NEG))
            t["qk"] = (t["kq"][ch:] * gam).astype(BF16)
            a = jnp.where(strict[di], t["kq"][:ch] * (gam * t["beta"]), 0.0)
            ad = jnp.where(levels[0], a, 0.0)
            t["p"] = eye - ad
            t["ab"] = ad.astype(BF16)
            t["off"] = [jnp.where(m, a, 0.0).astype(BF16) for m in levels[1:]]
        for t in chains:
            t["x"] = _dot(t["ab"], t["ab"])
        for i in range(n_double):
            last = i == n_double - 1
            for t in chains:
                xb = t["x"].astype(BF16)
                pb = t["p"].astype(BF16)
                t["r"] = _dot(xb, pb) if last else _dot(xb, jnp.concatenate([pb, xb], axis=1))
            for t in chains:
                t["p"] = t["p"] + t["r"][:, :ch]
                if not last:
                    t["x"] = t["r"][:, ch:]
        for lv in range(len(levels) - 1):
            for t in chains:
                t["m"] = _dot(t["off"][lv], t["p"].astype(BF16))
            for t in chains:
                t["p"] = t["p"] - _dot(t["p"].astype(BF16), t["m"].astype(BF16))
        for t in chains:
            t["edec"] = jnp.exp(t["dec"])
            rhs = jnp.concatenate([t["v"] * t["beta"], t["kb"] * t["edec"]], axis=1).astype(BF16)
            t["uw"] = _dot(t["p"].astype(BF16), rhs)
        for t in chains:
            c, di = t["c"], t["di"]
            base = pl.multiple_of(c * ch, ch)
            base2 = pl.multiple_of(c * 2 * ch, 2 * ch)
            u_scr[di, pl.ds(base, ch), :] = t["uw"][:, :dk]
            wq_scr[di, pl.ds(base2, 2 * ch), :] = jnp.concatenate(
                [t["uw"][:, dk:], t["q"] * t["edec"]], axis=0).astype(BF16)
            qk_scr[di, pl.ds(base, ch), :] = t["qk"]
            kd_scr[di, pl.ds(base, ch), :] = (t["k"] * jnp.exp(t["tot"] - t["dec"])).astype(BF16)
            et_scr[di, pl.ds(pl.multiple_of(c * SUBLANES, SUBLANES), SUBLANES), :] = jnp.broadcast_to(
                jnp.exp(t["tot"]), (SUBLANES, LANES))
        return carry

    lax.fori_loop(0, nch // group, prep_body, 0)

    def rec_body(ci, carry):
        cs = (ci, nch - 1 - ci)
        bases = [pl.multiple_of(c * ch, ch) for c in cs]
        ws = [_dot(wq_scr[di, pl.ds(pl.multiple_of(cs[di] * 2 * ch, 2 * ch), 2 * ch), :], carry[di].astype(BF16))
              for di in (0, 1)]
        v_new = [(u_scr[di, pl.ds(bases[di], ch), :] - ws[di][:ch]).astype(BF16) for di in (0, 1)]
        upd = [_dot_tn(kd_scr[di, pl.ds(bases[di], ch), :], v_new[di]) for di in (0, 1)]
        intra = [_dot(qk_scr[di, pl.ds(bases[di], ch), :], v_new[di]) for di in (0, 1)]
        s_new = []
        for di in (0, 1):
            et = et_scr[di, pl.ds(pl.multiple_of(cs[di] * SUBLANES, SUBLANES), 1), :]
            s_new.append(carry[di] * et + upd[di])
            o_scr[di, pl.ds(bases[di], ch), :] = ws[di][ch:] + intra[di]
        return tuple(s_new)

    zero = jnp.zeros((dk, dk), F32)
    lax.fori_loop(0, nch, rec_body, (zero, zero))

    def fin_body(c, carry):
        base = pl.multiple_of(c * ch, ch)
        o = o_scr[0, pl.ds(base, ch), :] + o_scr[1, pl.ds(base, ch), :]
        out_ref[0, pl.ds(base, ch), :] = _rms(o, nw_ref[...]) * _silu(z_ref[0, pl.ds(base, ch), :])
        return carry

    lax.fori_loop(0, nch, fin_body, 0)


def _gdn_mixer(proj3, small3, conv_w, hp, norm_w):
    b, length, _ = proj3.shape
    dk = GDN_HEAD_DIM
    nch = length // GDN_CHUNK
    q0 = (2 * SSD_INNER + 2 * SSD_GROUPS * SSD_STATE) // dk
    seq = lambda off: pl.BlockSpec((1, length, dk), lambda i, h: (i, 0, off + h))
    par = lambda off: pl.BlockSpec((CONV_K, dk), lambda i, h: (0, off + h))
    return pl.pallas_call(
        _gdn_kernel,
        grid=(b, GDN_HEADS),
        in_specs=[
            seq(q0), seq(q0 + GDN_HEADS), seq(q0 + 2 * GDN_HEADS), seq(q0 + 3 * GDN_HEADS),
            pl.BlockSpec((1, length, LANES), lambda i, h: (i, 0, 0)),
            par(0), par(GDN_HEADS), par(2 * GDN_HEADS),
            pl.BlockSpec((SUBLANES, LANES), lambda i, h: (0, 0)),
            pl.BlockSpec((1, dk), lambda i, h: (0, 0)),
        ],
        out_specs=pl.BlockSpec((1, length, dk), lambda i, h: (i, 0, h)),
        out_shape=jax.ShapeDtypeStruct((b, length, GDN_INNER), F32),
        scratch_shapes=[
            pltpu.VMEM((length, dk), F32), pltpu.VMEM((length, dk), F32), pltpu.VMEM((length, dk), F32),
            pltpu.VMEM((2, length, dk), F32),
            pltpu.VMEM((2, 2 * length, dk), BF16),
            pltpu.VMEM((2, length, GDN_CHUNK), BF16),
            pltpu.VMEM((2, length, dk), BF16),
            pltpu.VMEM((2, nch * SUBLANES, LANES), F32),
            pltpu.VMEM((2, length, dk), F32),
        ],
        compiler_params=_cparams(("parallel", "arbitrary")),
        name="gdn_mixer",
    )(proj3, proj3, proj3, proj3, small3, conv_w, conv_w, conv_w, hp, norm_w)


def _merge_kernel(x_ref, ya_ref, yb_ref, ga_ref, gb_ref, wa_ref, wb_ref, wo_ref, gf_ref, h_ref, n_ref):
    ya = _dot(ya_ref[...].astype(BF16), wa_ref[...])
    yb = _dot(yb_ref[...].astype(BF16), wb_ref[...])
    merged = jax.nn.sigmoid(ga_ref[...]) * ya + jax.nn.sigmoid(gb_ref[...]) * yb
    h = x_ref[...] + _dot(merged.astype(BF16), wo_ref[...])
    h_ref[...] = h
    n_ref[...] = _rms(h, gf_ref[...]).astype(BF16)


def _merge(x2d, ya, yb, proj2d, wa, wb, wo, gf):
    t, d = x2d.shape
    tm = min(TM_MERGE, t)
    ga0 = (proj2d.shape[1] - 2 * d) // d
    tok = lambda: pl.BlockSpec((tm, d), lambda i: (i, 0))
    wsp = lambda: pl.BlockSpec((d, d), lambda i: (0, 0))
    return pl.pallas_call(
        _merge_kernel,
        grid=(t // tm,),
        in_specs=[tok(), tok(), tok(),
                  pl.BlockSpec((tm, d), lambda i: (i, ga0)), pl.BlockSpec((tm, d), lambda i: (i, ga0 + 1)),
                  wsp(), wsp(), wsp(), pl.BlockSpec((1, d), lambda i: (0, 0))],
        out_specs=[tok(), tok()],
        out_shape=[jax.ShapeDtypeStruct((t, d), F32), jax.ShapeDtypeStruct((t, d), BF16)],
        compiler_params=_cparams(("parallel",)),
        name="merge",
    )(x2d, ya, yb, proj2d, proj2d, wa, wb, wo, gf)


def _topk_store(problems, k):
    state = [p[0] for p in problems]
    iotas = [lax.broadcasted_iota(I32, s.shape, 0).astype(F32) for s in state]
    for i in range(k):
        ms = [jnp.max(s, axis=0, keepdims=True) for s in state]
        cands = [jnp.where(s == m, io, float(s.shape[0])) for s, m, io in zip(state, ms, iotas)]
        ixs = [jnp.min(c, axis=0, keepdims=True) for c in cands]
        hits = [c == ix for c, ix in zip(cands, ixs)]
        for (_, vals_ref, outs_ref, payload), m, ix, hit in zip(problems, ms, ixs, hits):
            vals_ref[i:i + 1, :] = m
            outs_ref[i:i + 1, :] = (ix if payload is None
                                    else jnp.max(jnp.where(hit, payload, -1.0), axis=0, keepdims=True))
        state = [jnp.where(hit, -jnp.inf, s) for s, hit in zip(state, hits)]


def _route_unit(tok_ref, wq_ref, keys_ref, e_scr, gt_scr, tv_scr, ti_scr, bv_scr, be_scr, head, tile):
    kk = PEER_TOPK
    half = kk // 2
    toks = tok_ref[pl.ds(pl.multiple_of(tile * LANES, LANES), LANES), :]
    qry = _dot(toks, wq_ref[...])
    st = [_dot_nt(keys_ref[0, z], qry[:, z * PEER_HALF:(z + 1) * PEER_HALF].astype(BF16)) for z in range(2)]
    _topk_store([(st[z], tv_scr.at[z], ti_scr.at[z], None) for z in range(2)], kk)
    s0, s1 = tv_scr[0], tv_scr[1]
    e0, i1 = ti_scr[0] * float(PEER_KEYS), ti_scr[1]
    cs = [s0[0:1] + s1] + [s0[i:i + 1] + s1[0:half] for i in range(1, half)] + [s0[half:] + s1[0:1]]
    ce = [e0[0:1] + i1] + [e0[i:i + 1] + i1[0:half] for i in range(1, half)] + [e0[half:] + i1[0:1]]
    _topk_store([(jnp.concatenate(cs, axis=0), bv_scr, be_scr, jnp.concatenate(ce, axis=0))], kk)
    best = bv_scr[...]
    ex = jnp.exp(best - best[0:1])
    rows = pl.ds(pl.multiple_of(head * kk, kk), kk)
    lanes = pl.ds(pl.multiple_of(tile * LANES, LANES), LANES)
    e_scr[rows, lanes] = be_scr[...]
    gt_scr[rows, lanes] = ex / jnp.sum(ex, axis=0, keepdims=True)


def _peer_kernel(nm_ref, nr_ref, wq_ref, keys_ref, u_ref, v_ref, out_ref,
                 g_scr, e_scr, gt_scr, et_scr, gtt_scr, tv_scr, ti_scr, bv_scr, be_scr):
    tm = nm_ref.shape[0]
    nk = PEER_KEYS
    ntile = tm // LANES
    tile_i = pl.program_id(0)
    step = pl.program_id(1)

    @pl.when((tile_i == 0) & (step == 0))
    def _():
        e_scr[...] = jnp.zeros_like(e_scr)
        gt_scr[...] = jnp.zeros_like(gt_scr)

    @pl.when(step == 0)
    def _():
        out_ref[...] = jnp.zeros_like(out_ref)
        et_scr[...] = e_scr[...].T.astype(I32)
        gtt_scr[...] = gt_scr[...].T
        sub = lax.broadcasted_iota(I32, (nk, LANES), 0)

        zero = jnp.zeros((nk, LANES), BF16)

        def pair_body(tp, carry):
            t0 = 2 * tp
            ats, bts = [], []
            for o in range(2):
                re = et_scr[pl.ds(t0 + o, 1), :]
                rg = 0.5 * gtt_scr[pl.ds(t0 + o, 1), :]
                ats.append(jnp.where(sub == (re >> 7), rg, 0.0).astype(BF16))
                bts.append(jnp.where(sub == (re & (nk - 1)), 1.0, 0.0).astype(BF16))
            bt = jnp.concatenate([jnp.concatenate([bts[0], zero], axis=1),
                                  jnp.concatenate([zero, bts[1]], axis=1)], axis=0)
            g2 = _dot_nt(jnp.concatenate(ats, axis=1), bt)
            lo = pltpu.bitcast(g2[:, :LANES].astype(BF16).astype(F32), jnp.uint32) >> 16
            hi = pltpu.bitcast(g2[:, LANES:].astype(BF16).astype(F32), jnp.uint32) & jnp.uint32(0xFFFF0000)
            g_scr[pl.ds(tp, nk, stride=G_PITCH), :] = lo | hi
            return carry

        lax.fori_loop(0, tm // 2, pair_body, 0, unroll=TOKEN_UNROLL // 2)

    per_step = PEER_HEADS * ntile * EXP_ROWS // nk
    for k in range(per_step):
        unit = step * per_step + k
        _route_unit(nr_ref, wq_ref, keys_ref, e_scr, gt_scr, tv_scr, ti_scr, bv_scr, be_scr,
                    unit // ntile, lax.rem(unit, ntile))

    hid = _dot_nt(nm_ref[...], u_ref[...])
    act = hid * (1.0 + lax.erf(hid * (0.5 ** 0.5)))
    gates = [pltpu.bitcast(g_scr[pl.ds(pl.multiple_of((step * EXP_ROWS + r) * G_PITCH, SUBLANES), tm // 2), :], BF16)
             for r in range(EXP_ROWS)]
    wts = act.astype(BF16) * jnp.concatenate(gates, axis=1)
    out_ref[...] = out_ref[...] + _dot(wts, v_ref[...])


def _peer(n2, wq, keys, eu, ev):
    t, d = n2.shape
    tm = min(TM_EXP, t)
    assert tm == TM_EXP, "gate slab pitch is derived from TM_EXP"
    assert PEER_KEYS == LANES, "expert ids are decoded with a 7-bit shift"
    hk = PEER_HEADS * PEER_TOPK
    er = EXP_ROWS * PEER_KEYS
    nsteps = eu.shape[0] // er
    nt = t // tm
    units = PEER_HEADS * (tm // LANES)
    assert units % nsteps == 0 and (tm // LANES) % (units // nsteps) == 0, "whole routing units of one head per step"
    spr = nsteps // PEER_HEADS
    qw = 2 * PEER_HALF
    prev = lambda i, s: (jnp.maximum(i - 1, 0), 0)
    return pl.pallas_call(
        _peer_kernel,
        grid=(nt + 1, nsteps),
        in_specs=[pl.BlockSpec((tm, d), prev),
                  pl.BlockSpec((tm, d), lambda i, s: (jnp.minimum(i, nt - 1), 0)),
                  pl.BlockSpec((d, qw), lambda i, s: (0, s // spr)),
                  pl.BlockSpec((1, 2, PEER_KEYS, PEER_HALF), lambda i, s: (s // spr, 0, 0, 0)),
                  pl.BlockSpec((er, d), lambda i, s: (s, 0)),
                  pl.BlockSpec((er, d), lambda i, s: (s, 0))],
        out_specs=pl.BlockSpec((tm, d), prev),
        out_shape=jax.ShapeDtypeStruct((t, d), F32),
        scratch_shapes=[pltpu.VMEM((PEER_KEYS * G_PITCH, LANES), jnp.uint32),
                        pltpu.VMEM((hk, tm), F32), pltpu.VMEM((hk, tm), F32),
                        pltpu.VMEM((tm, hk), I32), pltpu.VMEM((tm, hk), F32),
                        pltpu.VMEM((2, PEER_TOPK, LANES), F32), pltpu.VMEM((2, PEER_TOPK, LANES), F32),
                        pltpu.VMEM((PEER_TOPK, LANES), F32), pltpu.VMEM((PEER_TOPK, LANES), F32)],
        compiler_params=_cparams(("arbitrary", "arbitrary")),
        name="peer_ffn",
    )(n2, n2, wq, keys, eu, ev)


def _ple_kernel(h_ref, f_ref, p_ref, gp_ref, wg_ref, wp_ref, gf_ref, y_ref):
    h = h_ref[...] + f_ref[...]
    gate = jax.nn.sigmoid(_dot(_rms(h, gp_ref[...]).astype(BF16), wg_ref[...]))
    h = h + gate * _dot(p_ref[...].astype(BF16), wp_ref[...])
    y_ref[...] = _rms(h, gf_ref[...])


def _ple(h1, ffn, p2d, gp, wg, wp, gfin):
    t, d = h1.shape
    pd = p2d.shape[1]
    tm = min(TM_PLE, t)
    vec = lambda: pl.BlockSpec((1, d), lambda i: (0, 0))
    tok = lambda: pl.BlockSpec((tm, d), lambda i: (i, 0))
    return pl.pallas_call(
        _ple_kernel,
        grid=(t // tm,),
        in_specs=[tok(), tok(), pl.BlockSpec((tm, pd), lambda i: (i, 0)),
                  vec(), pl.BlockSpec((d, d), lambda i: (0, 0)), pl.BlockSpec((pd, d), lambda i: (0, 0)), vec()],
        out_specs=tok(),
        out_shape=jax.ShapeDtypeStruct((t, d), F32),
        compiler_params=_cparams(("parallel",)),
        name="ple_final",
    )(h1, ffn, p2d, gp, wg, wp, gfin)


def _pad_lanes(v):
    return jnp.pad(v, (0, LANES - v.shape[0]))


def _prepare(norm_mix, w_in, ssd_conv_w, ssd_conv_b, ssd_a_log, ssd_dt_bias, ssd_d, ssd_norm, w_ssd_out,
             gdn_conv_w, gdn_a_log, gdn_dt_bias, gdn_norm, w_gdn_out, w_out, norm_ffn, peer_query, peer_keys,
             expert_u, expert_v, norm_ple, w_ple_gate, w_ple_proj, norm_final):
    w = w_in[0]
    xbc = SSD_INNER + 2 * SSD_GROUPS * SSD_STATE
    c_dt = SSD_INNER + xbc
    c_qkv = c_dt + 2 * SSD_HEADS
    c_a = c_qkv + 4 * GDN_INNER
    c_ga = c_a + 4 * GDN_HEADS
    w_main = jnp.concatenate([w[:, :c_dt], w[:, c_qkv:c_a], w[:, c_ga:]], axis=1).astype(BF16)
    w_small = jnp.concatenate([w[:, c_dt:c_qkv], w[:, c_a:c_ga]], axis=1)
    w_small = jnp.pad(w_small, ((0, 0), (0, LANES - w_small.shape[1]))).astype(BF16)
    hp = jnp.stack([
        _pad_lanes(jnp.concatenate([ssd_dt_bias[0].reshape(-1), gdn_dt_bias[0].reshape(-1)])),
        _pad_lanes(jnp.concatenate([ssd_a_log[0].reshape(-1), gdn_a_log[0].reshape(-1)])),
        _pad_lanes(ssd_d[0]),
    ])
    hp = jnp.pad(hp, ((0, SUBLANES - hp.shape[0]), (0, 0)))
    row = lambda v: v.reshape(1, -1)
    return dict(
        g_mix=row(norm_mix[0]), w_main=w_main, w_small=w_small, hp=hp,
        ssd_conv_w=ssd_conv_w[0], ssd_conv_b=row(ssd_conv_b[0]), ssd_norm=row(ssd_norm[0]),
        gdn_conv_w=gdn_conv_w[0], gdn_norm=row(gdn_norm[0]),
        wa=w_ssd_out[0].astype(BF16), wb=w_gdn_out[0].astype(BF16), wo=w_out[0].astype(BF16),
        g_ffn=row(norm_ffn[0]), wq=peer_query[0].astype(BF16),
        keys=peer_keys[0].astype(BF16),
        eu=expert_u[0].astype(BF16), ev=expert_v[0].astype(BF16),
        g_ple=row(norm_ple[0]), wg=w_ple_gate[0].astype(BF16), wp=w_ple_proj[0].astype(BF16),
        g_fin=row(norm_final),
    )


def _trunk(x, ple, p):
    b, length, d = x.shape
    t = b * length
    x2d = x.reshape(t, d)
    proj, small = _in_proj(x2d, p["g_mix"], p["w_main"], p["w_small"])
    proj3 = proj.reshape(b, length, -1)
    small3 = small.reshape(b, length, LANES)
    ya = _ssd_mixer(proj3, small3, p["ssd_conv_w"], p["ssd_conv_b"], p["hp"], p["ssd_norm"])
    yb = _gdn_mixer(proj3, small3, p["gdn_conv_w"], p["hp"], p["gdn_norm"])
    h1, n2 = _merge(x2d, ya.reshape(t, -1), yb.reshape(t, -1), proj, p["wa"], p["wb"], p["wo"], p["g_ffn"])
    ffn = _peer(n2, p["wq"], p["keys"], p["eu"], p["ev"])
    y = _ple(h1, ffn, ple.reshape(t, -1), p["g_ple"], p["wg"], p["wp"], p["g_fin"])
    return y.reshape(b, length, d)


def kernel(x_prompt, x_sample, p_prompt, p_sample, norm_mix, w_in, ssd_conv_w, ssd_conv_b, ssd_a_log, ssd_dt_bias, ssd_d, ssd_norm, w_ssd_out, gdn_conv_w, gdn_a_log, gdn_dt_bias, gdn_norm, w_gdn_out, w_out, norm_ffn, peer_query, peer_keys, expert_u, expert_v, norm_ple, w_ple_gate, w_ple_proj, norm_final):
    assert w_in.shape[0] == 1, "single-layer trunk"
    p = _prepare(norm_mix, w_in, ssd_conv_w, ssd_conv_b, ssd_a_log, ssd_dt_bias, ssd_d, ssd_norm, w_ssd_out,
                 gdn_conv_w, gdn_a_log, gdn_dt_bias, gdn_norm, w_gdn_out, w_out, norm_ffn, peer_query, peer_keys,
                 expert_u, expert_v, norm_ple, w_ple_gate, w_ple_proj, norm_final)
    return (_trunk(x_prompt, p_prompt[0], p), _trunk(x_sample, p_sample[0], p))
```

```python
import jax
import jax.numpy as jnp
from jax import lax
from jax.experimental import pallas as pl
from jax.experimental.pallas import tpu as pltpu

F32 = jnp.float32
BF16 = jnp.bfloat16
I32 = jnp.int32

EPS = 1e-6
CONV_K = 4
SSD_HEADS = 16
SSD_HEAD_DIM = 64
SSD_GROUPS = 4
SSD_STATE = 128
SSD_INNER = SSD_HEADS * SSD_HEAD_DIM
GDN_HEADS = 8
GDN_HEAD_DIM = 128
GDN_INNER = GDN_HEADS * GDN_HEAD_DIM
PEER_HEADS = 8
PEER_KEYS = 128
PEER_TOPK = 16
PEER_HALF = 128

LANES = 128
SUBLANES = 8
SSD_CHUNK = 128
SSD_STEP_GROUPS = 2
GDN_CHUNK = 128
GDN_PREP_CHUNKS = 8
GDN_STEP_HEADS = 2
GDN_INV_BLOCK = 8
VMEM_LIMIT = 56 * 1024 * 1024

TM_PROJ = 1024
TN_PROJ = 1024
TM_MERGE = 256
TM_EXP = 512
EXP_ROWS = 16
G_PITCH = TM_EXP // 2 + 8
TOKEN_UNROLL = 32
TM_PLE = 512

NEG = -1e30


def _dot(a, b):
    return jnp.dot(a, b, preferred_element_type=F32)


def _dot_nt(a, b):
    return lax.dot_general(a, b, (((1,), (1,)), ((), ())), preferred_element_type=F32)


def _dot_tn(a, b):
    return lax.dot_general(a, b, (((0,), (0,)), ((), ())), preferred_element_type=F32)


def _bf16_terms(m, terms):
    out = []
    for _ in range(terms - 1):
        t = m.astype(BF16)
        out.append(t)
        m = m - t.astype(F32)
    out.append(m.astype(BF16))
    return out


def _dot_exact_lhs(lb, m, terms=3):
    n = m.shape[1]
    r = _dot(lb, jnp.concatenate(_bf16_terms(m, terms), axis=1))
    return sum(r[:, i * n:(i + 1) * n] for i in range(1, terms)) + r[:, :n]


def _dot_exact_rhs(m, eb, terms=2):
    return _dot(jnp.concatenate(_bf16_terms(m, terms), axis=1), jnp.concatenate([eb] * terms, axis=0))


def _rows_transposed(sel, m, terms=3):
    n = m.shape[0]
    r = _dot_nt(sel, jnp.concatenate(_bf16_terms(m, terms), axis=0))
    return sum(r[:, i * n:(i + 1) * n] for i in range(1, terms)) + r[:, :n]


def _softplus(x):
    return jnp.maximum(x, 0.0) + jnp.log1p(jnp.exp(-jnp.abs(x)))


def _silu(x):
    return x * jax.nn.sigmoid(x)


def _rms(x, g):
    return x * lax.rsqrt(jnp.mean(x * x, axis=-1, keepdims=True) + EPS) * g


def _cparams(sem):
    return pltpu.CompilerParams(dimension_semantics=sem, vmem_limit_bytes=VMEM_LIMIT)


def _in_proj_kernel(x_ref, g_ref, wm_ref, ws_ref, main_ref, small_ref, n_scr):
    @pl.when(pl.program_id(1) == 0)
    def _():
        n = _rms(x_ref[...], g_ref[...]).astype(BF16)
        n_scr[...] = n
        small_ref[...] = _dot(n, ws_ref[...])

    main_ref[...] = _dot(n_scr[...], wm_ref[...])


def _in_proj(x2d, g, w_main, w_small):
    t, d = x2d.shape
    n_main = w_main.shape[1]
    tm = min(TM_PROJ, t)
    return pl.pallas_call(
        _in_proj_kernel,
        grid=(t // tm, n_main // TN_PROJ),
        in_specs=[
            pl.BlockSpec((tm, d), lambda i, j: (i, 0)),
            pl.BlockSpec((1, d), lambda i, j: (0, 0)),
            pl.BlockSpec((d, TN_PROJ), lambda i, j: (0, j)),
            pl.BlockSpec((d, LANES), lambda i, j: (0, 0)),
        ],
        out_specs=[
            pl.BlockSpec((tm, TN_PROJ), lambda i, j: (i, j)),
            pl.BlockSpec((tm, LANES), lambda i, j: (i, 0)),
        ],
        out_shape=[jax.ShapeDtypeStruct((t, n_main), F32), jax.ShapeDtypeStruct((t, LANES), F32)],
        scratch_shapes=[pltpu.VMEM((tm, d), BF16)],
        compiler_params=_cparams(("parallel", "arbitrary")),
        name="in_proj",
    )(x2d, g, w_main, w_small)


def _conv_silu_chunk(src_ref, w_ref, b_ref, c, nch, rows, lanes=slice(None)):
    length = src_ref.shape[1]
    base = pl.multiple_of(c * rows, rows)
    main = src_ref[0, pl.ds(base, rows), lanes]
    prev = src_ref[0, pl.ds(pl.multiple_of(jnp.maximum(base - SUBLANES, 0), SUBLANES), SUBLANES), lanes]
    nxt = src_ref[0, pl.ds(pl.multiple_of(jnp.minimum(base + rows, length - SUBLANES), SUBLANES), SUBLANES), lanes]
    prev = jnp.where(c > 0, prev, 0.0)
    nxt = jnp.where(c < nch - 1, nxt, 0.0)
    ext = jnp.concatenate([prev, main, nxt], axis=0)
    w = w_ref[:, lanes]
    lo = SUBLANES - CONV_K // 2
    y = ext[lo:lo + rows] * w[0:1]
    for j in range(1, CONV_K):
        y = y + ext[lo + j:lo + j + rows] * w[j:j + 1]
    if b_ref is not None:
        y = y + b_ref[:, lanes]
    return _silu(y)


def _expand_heads(t, lane0, width):
    r = t.shape[0]
    hid = lax.broadcasted_iota(I32, (r, 4 * width), 1) // width
    out = jnp.broadcast_to(t[:, lane0 + 3:lane0 + 4], (r, 4 * width))
    for j in (2, 1, 0):
        out = jnp.where(hid == j, t[:, lane0 + j:lane0 + j + 1], out)
    return out


def _ssd_kernel(z_ref, xs_ref, bm_ref, cm_ref, sm_ref, cwx_ref, cwb_ref, cwc_ref, cbx_ref, cbb_ref, cbc_ref,
                hp_ref, nw_ref, out_ref, xc_scr, bc_scr, cc_scr, y_scr, st_scr):
    ch = SSD_CHUNK
    length = z_ref.shape[1]
    nch = length // ch
    gwid = 4 * SSD_HEAD_DIM
    ng = SSD_STEP_GROUPS
    gsteps = range(ng)
    shifts = [lax.rem(LANES - 4 * (pl.program_id(1) * ng + gg), LANES) for gg in gsteps]
    hps = [pltpu.roll(hp_ref[...], s, 1) for s in shifts]
    bias_rows = [hp[0:1, :] for hp in hps]
    a_rows = [-jnp.exp(hp[1:2, :]) for hp in hps]
    xlanes = [slice(gg * gwid, (gg + 1) * gwid) for gg in gsteps]
    slanes = [slice(gg * SSD_STATE, (gg + 1) * SSD_STATE) for gg in gsteps]

    row = lax.broadcasted_iota(I32, (ch, ch), 0)
    col = lax.broadcasted_iota(I32, (ch, ch), 1)
    hid = lax.broadcasted_iota(I32, (ch, gwid), 1) // SSD_HEAD_DIM

    def conv_body(c, carry):
        base = pl.multiple_of(c * ch, ch)
        for gg in gsteps:
            xc_scr[gg, pl.ds(base, ch), :] = _conv_silu_chunk(xs_ref, cwx_ref, cbx_ref, c, nch, ch, xlanes[gg])
            bc_scr[gg, pl.ds(base, ch), :] = _conv_silu_chunk(bm_ref, cwb_ref, cbb_ref, c, nch, ch, slanes[gg])
            cc_scr[gg, pl.ds(base, ch), :] = _conv_silu_chunk(cm_ref, cwc_ref, cbc_ref, c, nch, ch, slanes[gg])
        return carry

    lax.fori_loop(0, nch, conv_body, 0)

    masks = ((row >= col), (row <= col))
    lmats = tuple(jnp.where(m, 1.0, 0.0).astype(BF16) for m in masks)
    lanes0 = (0, SSD_HEADS)
    tot_rows = (ch - 1, 0)
    erow = lax.broadcasted_iota(I32, (LANES, gwid), 0)
    ecol = lax.broadcasted_iota(I32, (LANES, gwid), 1) // SSD_HEAD_DIM
    emats = tuple(jnp.where(erow == ecol + l0, 1.0, 0.0).astype(BF16) for l0 in lanes0)
    srow = lax.broadcasted_iota(I32, (SUBLANES, LANES), 0)
    scol = lax.broadcasted_iota(I32, (SUBLANES, LANES), 1)
    sels = tuple(jnp.where(scol == srow + l0, 1.0, 0.0).astype(BF16) for l0 in lanes0)
    chains = [(gg, d) for gg in gsteps for d in (0, 1)]

    st_scr[...] = jnp.zeros_like(st_scr)

    def scan_body(ci, carry):
        bases = (pl.multiple_of(ci * ch, ch), pl.multiple_of((nch - 1 - ci) * ch, ch))
        x = [xc_scr[gg, pl.ds(bases[d], ch), :] for gg, d in chains]
        bb = [bc_scr[gg, pl.ds(bases[d], ch), :].astype(BF16) for gg, d in chains]
        cb = [cc_scr[gg, pl.ds(bases[d], ch), :].astype(BF16) for gg, d in chains]
        dt_all = [_softplus(pltpu.roll(sm_ref[0, pl.ds(bases[d], ch), :], shifts[gg], 1) + bias_rows[gg])
                  for gg, d in chains]
        acum = [_dot_exact_lhs(lmats[d], dt * a_rows[gg]) for (gg, d), dt in zip(chains, dt_all)]
        sc = [jnp.where(masks[d], _dot_nt(c_, b_), 0.0) for (gg, d), c_, b_ in zip(chains, cb, bb)]
        st = [st_scr[gg, d] for gg, d in chains]
        y_off = [_dot(c_, s_.astype(BF16)) for c_, s_ in zip(cb, st)]
        dt_x = [_dot_exact_rhs(dt, emats[d]) for (gg, d), dt in zip(chains, dt_all)]
        acum_t = [_rows_transposed(sels[d], a_) for (gg, d), a_ in zip(chains, acum)]
        tot = [a_[tot_rows[d]:tot_rows[d] + 1, :] for (gg, d), a_ in zip(chains, acum)]
        decay_x = [_dot_exact_rhs(jnp.concatenate([jnp.exp(a_), jnp.exp(t_ - a_)], axis=0), emats[d])
                   for (gg, d), a_, t_ in zip(chains, acum, tot)]
        xd = [x_ * d_ for x_, d_ in zip(x, dt_x)]
        lhs, rhs = [], []
        for k, (gg, d) in enumerate(chains):
            mixes, parts = [], []
            for j in range(4):
                ln = lanes0[d] + j
                dd = acum[k][:, ln:ln + 1] - acum_t[k][j:j + 1, :]
                mixes.append((sc[k] * jnp.exp(jnp.where(masks[d], dd, NEG))).astype(BF16))
                parts.append(jnp.where(hid == j, xd[k], 0.0).astype(BF16))
            lhs.append(jnp.concatenate(mixes, axis=1))
            rhs.append(jnp.concatenate(parts, axis=0))
        y = [_dot(l_, r_) for l_, r_ in zip(lhs, rhs)]
        upd = [_dot_tn(b_, (x_ * d_[ch:]).astype(BF16)) for b_, x_, d_ in zip(bb, xd, decay_x)]
        for k, (gg, d) in enumerate(chains):
            ea_x = decay_x[k][:ch]
            etot_x = ea_x[tot_rows[d]:tot_rows[d] + 1, :]
            y_scr[gg, d, pl.ds(bases[d], ch), :] = y[k] + y_off[k] * ea_x
            st_scr[gg, d] = st[k] * etot_x + upd[k]
        return carry

    lax.fori_loop(0, nch, scan_body, 0)

    dsk_x = [_expand_heads(hp[2:3, :], 0, SSD_HEAD_DIM) for hp in hps]

    def fin_body(c, carry):
        rows = pl.ds(pl.multiple_of(c * ch, ch), ch)
        for gg in gsteps:
            y = y_scr[gg, 0, rows, :] + y_scr[gg, 1, rows, :] + xc_scr[gg, rows, :] * dsk_x[gg]
            y = y * _silu(z_ref[0, rows, xlanes[gg]])
            out_ref[0, rows, xlanes[gg]] = _rms(y, nw_ref[:, xlanes[gg]])
        return carry

    lax.fori_loop(0, nch, fin_body, 0)


def _ssd_mixer(proj3, small3, conv_w, conv_b, hp, norm_w):
    b, length, _ = proj3.shape
    ng = SSD_STEP_GROUPS
    steps = SSD_GROUPS // ng
    gw = ng * 4 * SSD_HEAD_DIM
    sw = ng * SSD_STATE
    xs0 = SSD_INNER // gw
    b0 = 2 * SSD_INNER // sw
    c0 = b0 + steps
    nb_x = SSD_INNER // sw
    seq = lambda w, off: pl.BlockSpec((1, length, w), lambda i, g: (i, 0, off + g))
    par = lambda r, w, off: pl.BlockSpec((r, w), lambda i, g: (0, off + g))
    return pl.pallas_call(
        _ssd_kernel,
        grid=(b, steps),
        in_specs=[
            seq(gw, 0), seq(gw, xs0), seq(sw, b0), seq(sw, c0),
            pl.BlockSpec((1, length, LANES), lambda i, g: (i, 0, 0)),
            par(CONV_K, gw, 0), par(CONV_K, sw, nb_x), par(CONV_K, sw, nb_x + steps),
            par(1, gw, 0), par(1, sw, nb_x), par(1, sw, nb_x + steps),
            pl.BlockSpec((SUBLANES, LANES), lambda i, g: (0, 0)),
            par(1, gw, 0),
        ],
        out_specs=pl.BlockSpec((1, length, gw), lambda i, g: (i, 0, g)),
        out_shape=jax.ShapeDtypeStruct((b, length, SSD_INNER), F32),
        scratch_shapes=[
            pltpu.VMEM((ng, length, 4 * SSD_HEAD_DIM), F32),
            pltpu.VMEM((ng, length, SSD_STATE), F32), pltpu.VMEM((ng, length, SSD_STATE), F32),
            pltpu.VMEM((ng, 2, length, 4 * SSD_HEAD_DIM), F32),
            pltpu.VMEM((ng, 2, SSD_STATE, 4 * SSD_HEAD_DIM), F32),
        ],
        compiler_params=_cparams(("parallel", "arbitrary")),
        name="ssd_mixer",
    )(proj3, proj3, proj3, proj3, small3, conv_w, conv_w, conv_w, conv_b, conv_b, conv_b, hp, norm_w)


def _gdn_kernel(q_ref, k_ref, v_ref, z_ref, sm_ref, cwq_ref, cwk_ref, cwv_ref, hp_ref, nw_ref, out_ref,
                qn_scr, kn_scr, vn_scr, u_scr, wq_scr, qk_scr, kd_scr, et_scr, o_scr):
    ch = GDN_CHUNK
    length = z_ref.shape[1]
    nch = length // ch
    dk = GDN_HEAD_DIM
    nh = GDN_STEP_HEADS
    lane_a = 2 * SSD_HEADS
    lane_beta = lane_a + 2 * GDN_HEADS

    def head_shift(hh):
        return lax.rem(LANES - (pl.program_id(1) * nh + hh), LANES)

    def head_lanes(hh):
        return pl.ds(pl.multiple_of(hh * dk, dk), dk)

    row = lax.broadcasted_iota(I32, (ch, ch), 0)
    col = lax.broadcasted_iota(I32, (ch, ch), 1)
    eye = jnp.where(row == col, 1.0, 0.0)
    n_double = (GDN_INV_BLOCK - 1).bit_length() - 1
    levels = []
    size = GDN_INV_BLOCK
    inside = (row // size) == (col // size)
    levels.append(inside)
    while size < ch:
        size *= 2
        merged = (row // size) == (col // size)
        levels.append(merged & jnp.logical_not(inside))
        inside = merged

    def conv_body(idx, carry):
        hh, c = idx // nch, lax.rem(idx, nch)
        base = pl.multiple_of(c * ch, ch)
        lanes = head_lanes(hh)
        q = _conv_silu_chunk(q_ref, cwq_ref, None, c, nch, ch, lanes)
        k = _conv_silu_chunk(k_ref, cwk_ref, None, c, nch, ch, lanes)
        qn_scr[hh, pl.ds(base, ch), :] = q * (lax.rsqrt(jnp.sum(q * q, axis=-1, keepdims=True) + EPS) * dk ** -0.5)
        kn_scr[hh, pl.ds(base, ch), :] = k * lax.rsqrt(jnp.sum(k * k, axis=-1, keepdims=True) + EPS)
        vn_scr[hh, pl.ds(base, ch), :] = _conv_silu_chunk(v_ref, cwv_ref, None, c, nch, ch, lanes)
        return carry

    lax.fori_loop(0, nh * nch, conv_body, 0)

    incl = ((row >= col), (row <= col))
    strict = ((row > col), (row < col))
    lmat = tuple(jnp.where(m, 1.0, 0.0).astype(BF16) for m in incl)
    umat = tuple(jnp.where(m, 1.0, 0.0) for m in strict)
    jref = (0, ch - 1)
    tot_row = (ch - 1, 0)
    group = min(GDN_PREP_CHUNKS, nch)

    def prep_body(idx, carry):
        hh, gi = idx // (nch // group), lax.rem(idx, nch // group)
        shift = head_shift(hh)
        hp = pltpu.roll(hp_ref[...], shift, 1)
        bias_row = hp[0:1, :]
        acoef_row = jnp.exp(hp[1:2, :])
        chains = []
        for ci in range(group):
            c = gi * group + ci
            base = pl.multiple_of(c * ch, ch)
            q = qn_scr[hh, pl.ds(base, ch), :]
            k = kn_scr[hh, pl.ds(base, ch), :]
            v = vn_scr[hh, pl.ds(base, ch), :]
            sm = pltpu.roll(sm_ref[0, pl.ds(base, ch), :], shift, 1)
            g_all = -acoef_row * _softplus(sm + bias_row)
            beta_all = jax.nn.sigmoid(sm)
            kbf = k.astype(BF16)
            kq = _dot_nt(jnp.concatenate([kbf, q.astype(BF16)], axis=0), kbf)
            for di in (0, 1):
                ln = lane_a + di * GDN_HEADS
                lb = lane_beta + di * GDN_HEADS
                chains.append(dict(c=c, di=di, q=q, k=k, v=v, kq=kq, g=g_all[:, ln:ln + 1],
                                   beta=beta_all[:, lb:lb + 1]))
        for t in chains:
            t["dmat"] = _dot_exact_lhs(lmat[t["di"]], t["g"] * umat[t["di"]], terms=2)
        for t in chains:
            di = t["di"]
            t["kb"] = t["k"] * t["beta"]
            dec = t["dmat"][:, jref[di]:jref[di] + 1] + t["g"][jref[di]:jref[di] + 1, :]
            t["dec"] = dec
            t["tot"] = dec[tot_row[di]:tot_row[di] + 1, :]
            gam = jnp.exp(jnp.where(incl[di], t["dmat"], NEG))
            t["qk"] = (t["kq"][ch:] * gam).astype(BF16)
            a = jnp.where(strict[di], t["kq"][:ch] * (gam * t["beta"]), 0.0)
            ad = jnp.where(levels[0], a, 0.0)
            t["p"] = eye - ad
            t["ab"] = ad.astype(BF16)
            t["off"] = [jnp.where(m, a, 0.0).astype(BF16) for m in levels[1:]]
        for t in chains:
            t["x"] = _dot(t["ab"], t["ab"])
        for i in range(n_double):
            last = i == n_double - 1
            for t in chains:
                xb = t["x"].astype(BF16)
                pb = t["p"].astype(BF16)
                t["r"] = _dot(xb, pb) if last else _dot(xb, jnp.concatenate([pb, xb], axis=1))
            for t in chains:
                t["p"] = t["p"] + t["r"][:, :ch]
                if not last:
                    t["x"] = t["r"][:, ch:]
        for lv in range(len(levels) - 1):
            for t in chains:
                t["m"] = _dot(t["off"][lv], t["p"].astype(BF16))
            for t in chains:
                t["p"] = t["p"] - _dot(t["p"].astype(BF16), t["m"].astype(BF16))
        for t in chains:
            t["edec"] = jnp.exp(t["dec"])
            rhs = jnp.concatenate([t["v"] * t["beta"], t["kb"] * t["edec"]], axis=1).astype(BF16)
            t["uw"] = _dot(t["p"].astype(BF16), rhs)
        for t in chains:
            c, di = t["c"], t["di"]
            base = pl.multiple_of(c * ch, ch)
            base2 = pl.multiple_of(c * 2 * ch, 2 * ch)
            u_scr[hh, di, pl.ds(base, ch), :] = t["uw"][:, :dk]
            wq_scr[hh, di, pl.ds(base2, 2 * ch), :] = jnp.concatenate(
                [t["uw"][:, dk:], t["q"] * t["edec"]], axis=0).astype(BF16)
            qk_scr[hh, di, pl.ds(base, ch), :] = t["qk"]
            kd_scr[hh, di, pl.ds(base, ch), :] = (t["k"] * jnp.exp(t["tot"] - t["dec"])).astype(BF16)
            et_scr[hh, di, pl.ds(pl.multiple_of(c * SUBLANES, SUBLANES), SUBLANES), :] = jnp.broadcast_to(
                jnp.exp(t["tot"]), (SUBLANES, LANES))
        return carry

    lax.fori_loop(0, nh * (nch // group), prep_body, 0)

    chains = [(hh, di) for hh in range(nh) for di in (0, 1)]

    def rec_body(ci, carry):
        cs = (ci, nch - 1 - ci)
        bases = [pl.multiple_of(c * ch, ch) for c in cs]
        ws = [_dot(wq_scr[hh, di, pl.ds(pl.multiple_of(cs[di] * 2 * ch, 2 * ch), 2 * ch), :], s.astype(BF16))
              for (hh, di), s in zip(chains, carry)]
        v_new = [(u_scr[hh, di, pl.ds(bases[di], ch), :] - w[:ch]).astype(BF16) for (hh, di), w in zip(chains, ws)]
        upd = [_dot_tn(kd_scr[hh, di, pl.ds(bases[di], ch), :], vn) for (hh, di), vn in zip(chains, v_new)]
        intra = [_dot(qk_scr[hh, di, pl.ds(bases[di], ch), :], vn) for (hh, di), vn in zip(chains, v_new)]
        s_new = []
        for j, (hh, di) in enumerate(chains):
            et = et_scr[hh, di, pl.ds(pl.multiple_of(cs[di] * SUBLANES, SUBLANES), 1), :]
            s_new.append(carry[j] * et + upd[j])
            o_scr[hh, di, pl.ds(bases[di], ch), :] = ws[j][ch:] + intra[j]
        return tuple(s_new)

    zero = jnp.zeros((dk, dk), F32)
    lax.fori_loop(0, nch, rec_body, (zero,) * len(chains))

    def fin_body(idx, carry):
        hh, c = idx // nch, lax.rem(idx, nch)
        rows = pl.ds(pl.multiple_of(c * ch, ch), ch)
        lanes = head_lanes(hh)
        o = o_scr[hh, 0, rows, :] + o_scr[hh, 1, rows, :]
        out_ref[0, rows, lanes] = _rms(o, nw_ref[...]) * _silu(z_ref[0, rows, lanes])
        return carry

    lax.fori_loop(0, nh * nch, fin_body, 0)


def _gdn_mixer(proj3, small3, conv_w, hp, norm_w):
    b, length, _ = proj3.shape
    dk = GDN_HEAD_DIM
    nch = length // GDN_CHUNK
    nh = GDN_STEP_HEADS
    groups = GDN_HEADS // nh
    q0 = (2 * SSD_INNER + 2 * SSD_GROUPS * SSD_STATE) // (nh * dk)
    seq = lambda off: pl.BlockSpec((1, length, nh * dk), lambda i, h: (i, 0, off + h))
    par = lambda off: pl.BlockSpec((CONV_K, nh * dk), lambda i, h: (0, off + h))
    return pl.pallas_call(
        _gdn_kernel,
        grid=(b, groups),
        in_specs=[
            seq(q0), seq(q0 + groups), seq(q0 + 2 * groups), seq(q0 + 3 * groups),
            pl.BlockSpec((1, length, LANES), lambda i, h: (i, 0, 0)),
            par(0), par(groups), par(2 * groups),
            pl.BlockSpec((SUBLANES, LANES), lambda i, h: (0, 0)),
            pl.BlockSpec((1, dk), lambda i, h: (0, 0)),
        ],
        out_specs=pl.BlockSpec((1, length, nh * dk), lambda i, h: (i, 0, h)),
        out_shape=jax.ShapeDtypeStruct((b, length, GDN_INNER), F32),
        scratch_shapes=[
            pltpu.VMEM((nh, length, dk), F32), pltpu.VMEM((nh, length, dk), F32), pltpu.VMEM((nh, length, dk), F32),
            pltpu.VMEM((nh, 2, length, dk), F32),
            pltpu.VMEM((nh, 2, 2 * length, dk), BF16),
            pltpu.VMEM((nh, 2, length, GDN_CHUNK), BF16),
            pltpu.VMEM((nh, 2, length, dk), BF16),
            pltpu.VMEM((nh, 2, nch * SUBLANES, LANES), F32),
            pltpu.VMEM((nh, 2, length, dk), F32),
        ],
        compiler_params=_cparams(("parallel", "arbitrary")),
        name="gdn_mixer",
    )(proj3, proj3, proj3, proj3, small3, conv_w, conv_w, conv_w, hp, norm_w)


def _merge_kernel(x_ref, ya_ref, yb_ref, ga_ref, gb_ref, wa_ref, wb_ref, wo_ref, gf_ref, h_ref, n_ref):
    ya = _dot(ya_ref[...].astype(BF16), wa_ref[...])
    yb = _dot(yb_ref[...].astype(BF16), wb_ref[...])
    merged = jax.nn.sigmoid(ga_ref[...]) * ya + jax.nn.sigmoid(gb_ref[...]) * yb
    h = x_ref[...] + _dot(merged.astype(BF16), wo_ref[...])
    h_ref[...] = h
    n_ref[...] = _rms(h, gf_ref[...]).astype(BF16)


def _merge(x2d, ya, yb, proj2d, wa, wb, wo, gf):
    t, d = x2d.shape
    tm = min(TM_MERGE, t)
    ga0 = (proj2d.shape[1] - 2 * d) // d
    tok = lambda: pl.BlockSpec((tm, d), lambda i: (i, 0))
    wsp = lambda: pl.BlockSpec((d, d), lambda i: (0, 0))
    return pl.pallas_call(
        _merge_kernel,
        grid=(t // tm,),
        in_specs=[tok(), tok(), tok(),
                  pl.BlockSpec((tm, d), lambda i: (i, ga0)), pl.BlockSpec((tm, d), lambda i: (i, ga0 + 1)),
                  wsp(), wsp(), wsp(), pl.BlockSpec((1, d), lambda i: (0, 0))],
        out_specs=[tok(), tok()],
        out_shape=[jax.ShapeDtypeStruct((t, d), F32), jax.ShapeDtypeStruct((t, d), BF16)],
        compiler_params=_cparams(("parallel",)),
        name="merge",
    )(x2d, ya, yb, proj2d, proj2d, wa, wb, wo, gf)


def _topk_store(problems, k):
    state = [p[0] for p in problems]
    iotas = [lax.broadcasted_iota(I32, s.shape, 0).astype(F32) for s in state]
    for i in range(k):
        ms = [jnp.max(s, axis=0, keepdims=True) for s in state]
        cands = [jnp.where(s == m, io, float(s.shape[0])) for s, m, io in zip(state, ms, iotas)]
        ixs = [jnp.min(c, axis=0, keepdims=True) for c in cands]
        hits = [c == ix for c, ix in zip(cands, ixs)]
        for (_, vals_ref, outs_ref, payload), m, ix, hit in zip(problems, ms, ixs, hits):
            vals_ref[i:i + 1, :] = m
            outs_ref[i:i + 1, :] = (ix if payload is None
                                    else jnp.max(jnp.where(hit, payload, -1.0), axis=0, keepdims=True))
        state = [jnp.where(hit, -jnp.inf, s) for s, hit in zip(state, hits)]


def _route_unit(tok_ref, wq_ref, keys_ref, e_scr, gt_scr, tv_scr, ti_scr, bv_scr, be_scr, head, tile):
    kk = PEER_TOPK
    half = kk // 2
    toks = tok_ref[pl.ds(pl.multiple_of(tile * LANES, LANES), LANES), :]
    qry = _dot(toks, wq_ref[...])
    st = [_dot_nt(keys_ref[0, z], qry[:, z * PEER_HALF:(z + 1) * PEER_HALF].astype(BF16)) for z in range(2)]
    _topk_store([(st[z], tv_scr.at[z], ti_scr.at[z], None) for z in range(2)], kk)
    s0, s1 = tv_scr[0], tv_scr[1]
    e0, i1 = ti_scr[0] * float(PEER_KEYS), ti_scr[1]
    cs = [s0[0:1] + s1] + [s0[i:i + 1] + s1[0:half] for i in range(1, half)] + [s0[half:] + s1[0:1]]
    ce = [e0[0:1] + i1] + [e0[i:i + 1] + i1[0:half] for i in range(1, half)] + [e0[half:] + i1[0:1]]
    _topk_store([(jnp.concatenate(cs, axis=0), bv_scr, be_scr, jnp.concatenate(ce, axis=0))], kk)
    best = bv_scr[...]
    ex = jnp.exp(best - best[0:1])
    rows = pl.ds(pl.multiple_of(head * kk, kk), kk)
    lanes = pl.ds(pl.multiple_of(tile * LANES, LANES), LANES)
    e_scr[rows, lanes] = be_scr[...]
    gt_scr[rows, lanes] = ex / jnp.sum(ex, axis=0, keepdims=True)


def _peer_kernel(nm_ref, nr_ref, wq_ref, keys_ref, u_ref, v_ref, out_ref,
                 g_scr, e_scr, gt_scr, et_scr, gtt_scr, tv_scr, ti_scr, bv_scr, be_scr):
    tm = nm_ref.shape[0]
    nk = PEER_KEYS
    ntile = tm // LANES
    tile_i = pl.program_id(0)
    step = pl.program_id(1)

    @pl.when((tile_i == 0) & (step == 0))
    def _():
        e_scr[...] = jnp.zeros_like(e_scr)
        gt_scr[...] = jnp.zeros_like(gt_scr)

    @pl.when(step == 0)
    def _():
        out_ref[...] = jnp.zeros_like(out_ref)
        et_scr[...] = e_scr[...].T.astype(I32)
        gtt_scr[...] = gt_scr[...].T
        sub = lax.broadcasted_iota(I32, (nk, LANES), 0)

        zero = jnp.zeros((nk, LANES), BF16)

        def pair_body(tp, carry):
            t0 = 2 * tp
            ats, bts = [], []
            for o in range(2):
                re = et_scr[pl.ds(t0 + o, 1), :]
                rg = 0.5 * gtt_scr[pl.ds(t0 + o, 1), :]
                ats.append(jnp.where(sub == (re >> 7), rg, 0.0).astype(BF16))
                bts.append(jnp.where(sub == (re & (nk - 1)), 1.0, 0.0).astype(BF16))
            bt = jnp.concatenate([jnp.concatenate([bts[0], zero], axis=1),
                                  jnp.concatenate([zero, bts[1]], axis=1)], axis=0)
            g2 = _dot_nt(jnp.concatenate(ats, axis=1), bt)
            lo = pltpu.bitcast(g2[:, :LANES].astype(BF16).astype(F32), jnp.uint32) >> 16
            hi = pltpu.bitcast(g2[:, LANES:].astype(BF16).astype(F32), jnp.uint32) & jnp.uint32(0xFFFF0000)
            g_scr[pl.ds(tp, nk, stride=G_PITCH), :] = lo | hi
            return carry

        lax.fori_loop(0, tm // 2, pair_body, 0, unroll=TOKEN_UNROLL // 2)

    per_step = PEER_HEADS * ntile * EXP_ROWS // nk
    for k in range(per_step):
        unit = step * per_step + k
        _route_unit(nr_ref, wq_ref, keys_ref, e_scr, gt_scr, tv_scr, ti_scr, bv_scr, be_scr,
                    unit // ntile, lax.rem(unit, ntile))

    hid = _dot_nt(nm_ref[...], u_ref[...])
    act = hid * (1.0 + lax.erf(hid * (0.5 ** 0.5)))
    gates = [pltpu.bitcast(g_scr[pl.ds(pl.multiple_of((step * EXP_ROWS + r) * G_PITCH, SUBLANES), tm // 2), :], BF16)
             for r in range(EXP_ROWS)]
    wts = act.astype(BF16) * jnp.concatenate(gates, axis=1)
    out_ref[...] = out_ref[...] + _dot(wts, v_ref[...])


def _peer(n2, wq, keys, eu, ev):
    t, d = n2.shape
    tm = min(TM_EXP, t)
    assert tm == TM_EXP, "gate slab pitch is derived from TM_EXP"
    assert PEER_KEYS == LANES, "expert ids are decoded with a 7-bit shift"
    hk = PEER_HEADS * PEER_TOPK
    er = EXP_ROWS * PEER_KEYS
    nsteps = eu.shape[0] // er
    nt = t // tm
    units = PEER_HEADS * (tm // LANES)
    assert units % nsteps == 0 and (tm // LANES) % (units // nsteps) == 0, "whole routing units of one head per step"
    spr = nsteps // PEER_HEADS
    qw = 2 * PEER_HALF
    prev = lambda i, s: (jnp.maximum(i - 1, 0), 0)
    return pl.pallas_call(
        _peer_kernel,
        grid=(nt + 1, nsteps),
        in_specs=[pl.BlockSpec((tm, d), prev),
                  pl.BlockSpec((tm, d), lambda i, s: (jnp.minimum(i, nt - 1), 0)),
                  pl.BlockSpec((d, qw), lambda i, s: (0, s // spr)),
                  pl.BlockSpec((1, 2, PEER_KEYS, PEER_HALF), lambda i, s: (s // spr, 0, 0, 0)),
                  pl.BlockSpec((er, d), lambda i, s: (s, 0)),
                  pl.BlockSpec((er, d), lambda i, s: (s, 0))],
        out_specs=pl.BlockSpec((tm, d), prev),
        out_shape=jax.ShapeDtypeStruct((t, d), F32),
        scratch_shapes=[pltpu.VMEM((PEER_KEYS * G_PITCH, LANES), jnp.uint32),
                        pltpu.VMEM((hk, tm), F32), pltpu.VMEM((hk, tm), F32),
                        pltpu.VMEM((tm, hk), I32), pltpu.VMEM((tm, hk), F32),
                        pltpu.VMEM((2, PEER_TOPK, LANES), F32), pltpu.VMEM((2, PEER_TOPK, LANES), F32),
                        pltpu.VMEM((PEER_TOPK, LANES), F32), pltpu.VMEM((PEER_TOPK, LANES), F32)],
        compiler_params=_cparams(("arbitrary", "arbitrary")),
        name="peer_ffn",
    )(n2, n2, wq, keys, eu, ev)


def _ple_kernel(h_ref, f_ref, p_ref, gp_ref, wg_ref, wp_ref, gf_ref, y_ref):
    h = h_ref[...] + f_ref[...]
    gate = jax.nn.sigmoid(_dot(_rms(h, gp_ref[...]).astype(BF16), wg_ref[...]))
    h = h + gate * _dot(p_ref[...].astype(BF16), wp_ref[...])
    y_ref[...] = _rms(h, gf_ref[...])


def _ple(h1, ffn, p2d, gp, wg, wp, gfin):
    t, d = h1.shape
    pd = p2d.shape[1]
    tm = min(TM_PLE, t)
    vec = lambda: pl.BlockSpec((1, d), lambda i: (0, 0))
    tok = lambda: pl.BlockSpec((tm, d), lambda i: (i, 0))
    return pl.pallas_call(
        _ple_kernel,
        grid=(t // tm,),
        in_specs=[tok(), tok(), pl.BlockSpec((tm, pd), lambda i: (i, 0)),
                  vec(), pl.BlockSpec((d, d), lambda i: (0, 0)), pl.BlockSpec((pd, d), lambda i: (0, 0)), vec()],
        out_specs=tok(),
        out_shape=jax.ShapeDtypeStruct((t, d), F32),
        compiler_params=_cparams(("parallel",)),
        name="ple_final",
    )(h1, ffn, p2d, gp, wg, wp, gfin)


def _pad_lanes(v):
    return jnp.pad(v, (0, LANES - v.shape[0]))


def _prepare(norm_mix, w_in, ssd_conv_w, ssd_conv_b, ssd_a_log, ssd_dt_bias, ssd_d, ssd_norm, w_ssd_out,
             gdn_conv_w, gdn_a_log, gdn_dt_bias, gdn_norm, w_gdn_out, w_out, norm_ffn, peer_query, peer_keys,
             expert_u, expert_v, norm_ple, w_ple_gate, w_ple_proj, norm_final):
    w = w_in[0]
    xbc = SSD_INNER + 2 * SSD_GROUPS * SSD_STATE
    c_dt = SSD_INNER + xbc
    c_qkv = c_dt + 2 * SSD_HEADS
    c_a = c_qkv + 4 * GDN_INNER
    c_ga = c_a + 4 * GDN_HEADS
    w_main = jnp.concatenate([w[:, :c_dt], w[:, c_qkv:c_a], w[:, c_ga:]], axis=1).astype(BF16)
    w_small = jnp.concatenate([w[:, c_dt:c_qkv], w[:, c_a:c_ga]], axis=1)
    w_small = jnp.pad(w_small, ((0, 0), (0, LANES - w_small.shape[1]))).astype(BF16)
    hp = jnp.stack([
        _pad_lanes(jnp.concatenate([ssd_dt_bias[0].reshape(-1), gdn_dt_bias[0].reshape(-1)])),
        _pad_lanes(jnp.concatenate([ssd_a_log[0].reshape(-1), gdn_a_log[0].reshape(-1)])),
        _pad_lanes(ssd_d[0]),
    ])
    hp = jnp.pad(hp, ((0, SUBLANES - hp.shape[0]), (0, 0)))
    row = lambda v: v.reshape(1, -1)
    return dict(
        g_mix=row(norm_mix[0]), w_main=w_main, w_small=w_small, hp=hp,
        ssd_conv_w=ssd_conv_w[0], ssd_conv_b=row(ssd_conv_b[0]), ssd_norm=row(ssd_norm[0]),
        gdn_conv_w=gdn_conv_w[0], gdn_norm=row(gdn_norm[0]),
        wa=w_ssd_out[0].astype(BF16), wb=w_gdn_out[0].astype(BF16), wo=w_out[0].astype(BF16),
        g_ffn=row(norm_ffn[0]), wq=peer_query[0].astype(BF16),
        keys=peer_keys[0].astype(BF16),
        eu=expert_u[0].astype(BF16), ev=expert_v[0].astype(BF16),
        g_ple=row(norm_ple[0]), wg=w_ple_gate[0].astype(BF16), wp=w_ple_proj[0].astype(BF16),
        g_fin=row(norm_final),
    )


def _trunk(x, ple, p):
    b, length, d = x.shape
    t = b * length
    x2d = x.reshape(t, d)
    proj, small = _in_proj(x2d, p["g_mix"], p["w_main"], p["w_small"])
    proj3 = proj.reshape(b, length, -1)
    small3 = small.reshape(b, length, LANES)
    ya = _ssd_mixer(proj3, small3, p["ssd_conv_w"], p["ssd_conv_b"], p["hp"], p["ssd_norm"])
    yb = _gdn_mixer(proj3, small3, p["gdn_conv_w"], p["hp"], p["gdn_norm"])
    h1, n2 = _merge(x2d, ya.reshape(t, -1), yb.reshape(t, -1), proj, p["wa"], p["wb"], p["wo"], p["g_ffn"])
    ffn = _peer(n2, p["wq"], p["keys"], p["eu"], p["ev"])
    y = _ple(h1, ffn, ple.reshape(t, -1), p["g_ple"], p["wg"], p["wp"], p["g_fin"])
    return y.reshape(b, length, d)


def kernel(x_prompt, x_sample, p_prompt, p_sample, norm_mix, w_in, ssd_conv_w, ssd_conv_b, ssd_a_log, ssd_dt_bias, ssd_d, ssd_norm, w_ssd_out, gdn_conv_w, gdn_a_log, gdn_dt_bias, gdn_norm, w_gdn_out, w_out, norm_ffn, peer_query, peer_keys, expert_u, expert_v, norm_ple, w_ple_gate, w_ple_proj, norm_final):
    assert w_in.shape[0] == 1, "single-layer trunk"
    p = _prepare(norm_mix, w_in, ssd_conv_w, ssd_conv_b, ssd_a_log, ssd_dt_bias, ssd_d, ssd_norm, w_ssd_out,
                 gdn_conv_w, gdn_a_log, gdn_dt_bias, gdn_norm, w_gdn_out, w_out, norm_ffn, peer_query, peer_keys,
                 expert_u, expert_v, norm_ple, w_ple_gate, w_ple_proj, norm_final)
    return (_trunk(x_prompt, p_prompt[0], p), _trunk(x_sample, p_sample[0], p))
```

```python
import jax
import jax.numpy as jnp
from jax import lax
from jax.experimental import pallas as pl
from jax.experimental.pallas import tpu as pltpu

F32 = jnp.float32
BF16 = jnp.bfloat16
I32 = jnp.int32

EPS = 1e-6
CONV_K = 4
SSD_HEADS = 16
SSD_HEAD_DIM = 64
SSD_GROUPS = 4
SSD_STATE = 128
SSD_INNER = SSD_HEADS * SSD_HEAD_DIM
GDN_HEADS = 8
GDN_HEAD_DIM = 128
GDN_INNER = GDN_HEADS * GDN_HEAD_DIM
PEER_HEADS = 8
PEER_KEYS = 128
PEER_TOPK = 16
PEER_HALF = 128

LANES = 128
SUBLANES = 8
SSD_CHUNK = 128
SSD_STEP_GROUPS = 2
GDN_CHUNK = 128
GDN_PREP_CHUNKS = 8
GDN_STEP_HEADS = 2
GDN_INV_BLOCK = 8
VMEM_LIMIT = 56 * 1024 * 1024

TM_PROJ = 2048
TN_PROJ = 1024
TM_MERGE = 512
TM_EXP = 512
EXP_ROWS = 16
G_PITCH = TM_EXP // 2 + 8
TOKEN_UNROLL = 64
TM_PLE = 512

NEG = -1e30


def _dot(a, b):
    return jnp.dot(a, b, preferred_element_type=F32)


def _dot_nt(a, b):
    return lax.dot_general(a, b, (((1,), (1,)), ((), ())), preferred_element_type=F32)


def _dot_tn(a, b):
    return lax.dot_general(a, b, (((0,), (0,)), ((), ())), preferred_element_type=F32)


def _bf16_terms(m, terms):
    out = []
    for _ in range(terms - 1):
        t = m.astype(BF16)
        out.append(t)
        m = m - t.astype(F32)
    out.append(m.astype(BF16))
    return out


def _dot_exact_lhs(lb, m, terms=3):
    n = m.shape[1]
    r = _dot(lb, jnp.concatenate(_bf16_terms(m, terms), axis=1))
    return sum(r[:, i * n:(i + 1) * n] for i in range(1, terms)) + r[:, :n]


def _dot_exact_rhs(m, eb, terms=2):
    return _dot(jnp.concatenate(_bf16_terms(m, terms), axis=1), jnp.concatenate([eb] * terms, axis=0))


def _rows_transposed(sel, m, terms=3):
    n = m.shape[0]
    r = _dot_nt(sel, jnp.concatenate(_bf16_terms(m, terms), axis=0))
    return sum(r[:, i * n:(i + 1) * n] for i in range(1, terms)) + r[:, :n]


def _softplus(x):
    return jnp.maximum(x, 0.0) + jnp.log1p(jnp.exp(-jnp.abs(x)))


def _silu(x):
    return x * jax.nn.sigmoid(x)


def _rms(x, g):
    return x * lax.rsqrt(jnp.mean(x * x, axis=-1, keepdims=True) + EPS) * g


def _cparams(sem):
    return pltpu.CompilerParams(dimension_semantics=sem, vmem_limit_bytes=VMEM_LIMIT)


def _in_proj_kernel(x_ref, g_ref, wm_ref, ws_ref, main_ref, small_ref, n_scr):
    @pl.when(pl.program_id(1) == 0)
    def _():
        n = _rms(x_ref[...], g_ref[...]).astype(BF16)
        n_scr[...] = n
        small_ref[...] = _dot(n, ws_ref[...])

    main_ref[...] = _dot(n_scr[...], wm_ref[...])


def _in_proj(x2d, g, w_main, w_small):
    t, d = x2d.shape
    n_main = w_main.shape[1]
    tm = min(TM_PROJ, t)
    return pl.pallas_call(
        _in_proj_kernel,
        grid=(t // tm, n_main // TN_PROJ),
        in_specs=[
            pl.BlockSpec((tm, d), lambda i, j: (i, 0)),
            pl.BlockSpec((1, d), lambda i, j: (0, 0)),
            pl.BlockSpec((d, TN_PROJ), lambda i, j: (0, j)),
            pl.BlockSpec((d, LANES), lambda i, j: (0, 0)),
        ],
        out_specs=[
            pl.BlockSpec((tm, TN_PROJ), lambda i, j: (i, j)),
            pl.BlockSpec((tm, LANES), lambda i, j: (i, 0)),
        ],
        out_shape=[jax.ShapeDtypeStruct((t, n_main), F32), jax.ShapeDtypeStruct((t, LANES), F32)],
        scratch_shapes=[pltpu.VMEM((tm, d), BF16)],
        compiler_params=_cparams(("parallel", "arbitrary")),
        name="in_proj",
    )(x2d, g, w_main, w_small)


def _conv_silu_chunk(src_ref, w_ref, b_ref, c, nch, rows, lanes=slice(None)):
    length = src_ref.shape[1]
    base = pl.multiple_of(c * rows, rows)
    main = src_ref[0, pl.ds(base, rows), lanes]
    prev = src_ref[0, pl.ds(pl.multiple_of(jnp.maximum(base - SUBLANES, 0), SUBLANES), SUBLANES), lanes]
    nxt = src_ref[0, pl.ds(pl.multiple_of(jnp.minimum(base + rows, length - SUBLANES), SUBLANES), SUBLANES), lanes]
    prev = jnp.where(c > 0, prev, 0.0)
    nxt = jnp.where(c < nch - 1, nxt, 0.0)
    ext = jnp.concatenate([prev, main, nxt], axis=0)
    w = w_ref[:, lanes]
    lo = SUBLANES - CONV_K // 2
    y = ext[lo:lo + rows] * w[0:1]
    for j in range(1, CONV_K):
        y = y + ext[lo + j:lo + j + rows] * w[j:j + 1]
    if b_ref is not None:
        y = y + b_ref[:, lanes]
    return _silu(y)


def _expand_heads(t, lane0, width):
    r = t.shape[0]
    hid = lax.broadcasted_iota(I32, (r, 4 * width), 1) // width
    out = jnp.broadcast_to(t[:, lane0 + 3:lane0 + 4], (r, 4 * width))
    for j in (2, 1, 0):
        out = jnp.where(hid == j, t[:, lane0 + j:lane0 + j + 1], out)
    return out


def _ssd_kernel(z_ref, xs_ref, bm_ref, cm_ref, sm_ref, cwx_ref, cwb_ref, cwc_ref, cbx_ref, cbb_ref, cbc_ref,
                hp_ref, nw_ref, out_ref, xc_scr, bc_scr, cc_scr, y_scr, st_scr):
    ch = SSD_CHUNK
    length = z_ref.shape[1]
    nch = length // ch
    gwid = 4 * SSD_HEAD_DIM
    ng = SSD_STEP_GROUPS
    gsteps = range(ng)
    shifts = [lax.rem(LANES - 4 * (pl.program_id(1) * ng + gg), LANES) for gg in gsteps]
    hps = [pltpu.roll(hp_ref[...], s, 1) for s in shifts]
    bias_rows = [hp[0:1, :] for hp in hps]
    a_rows = [-jnp.exp(hp[1:2, :]) for hp in hps]
    xlanes = [slice(gg * gwid, (gg + 1) * gwid) for gg in gsteps]
    slanes = [slice(gg * SSD_STATE, (gg + 1) * SSD_STATE) for gg in gsteps]

    row = lax.broadcasted_iota(I32, (ch, ch), 0)
    col = lax.broadcasted_iota(I32, (ch, ch), 1)
    hid = lax.broadcasted_iota(I32, (ch, gwid), 1) // SSD_HEAD_DIM

    def conv_body(c, carry):
        base = pl.multiple_of(c * ch, ch)
        for gg in gsteps:
            xc_scr[gg, pl.ds(base, ch), :] = _conv_silu_chunk(xs_ref, cwx_ref, cbx_ref, c, nch, ch, xlanes[gg])
            bc_scr[gg, pl.ds(base, ch), :] = _conv_silu_chunk(bm_ref, cwb_ref, cbb_ref, c, nch, ch, slanes[gg])
            cc_scr[gg, pl.ds(base, ch), :] = _conv_silu_chunk(cm_ref, cwc_ref, cbc_ref, c, nch, ch, slanes[gg])
        return carry

    lax.fori_loop(0, nch, conv_body, 0)

    masks = ((row >= col), (row <= col))
    lmats = tuple(jnp.where(m, 1.0, 0.0).astype(BF16) for m in masks)
    lanes0 = (0, SSD_HEADS)
    tot_rows = (ch - 1, 0)
    erow = lax.broadcasted_iota(I32, (LANES, gwid), 0)
    ecol = lax.broadcasted_iota(I32, (LANES, gwid), 1) // SSD_HEAD_DIM
    emats = tuple(jnp.where(erow == ecol + l0, 1.0, 0.0).astype(BF16) for l0 in lanes0)
    srow = lax.broadcasted_iota(I32, (SUBLANES, LANES), 0)
    scol = lax.broadcasted_iota(I32, (SUBLANES, LANES), 1)
    sels = tuple(jnp.where(scol == srow + l0, 1.0, 0.0).astype(BF16) for l0 in lanes0)
    chains = [(gg, d) for gg in gsteps for d in (0, 1)]

    st_scr[...] = jnp.zeros_like(st_scr)

    def scan_body(ci, carry):
        bases = (pl.multiple_of(ci * ch, ch), pl.multiple_of((nch - 1 - ci) * ch, ch))
        x = [xc_scr[gg, pl.ds(bases[d], ch), :] for gg, d in chains]
        bb = [bc_scr[gg, pl.ds(bases[d], ch), :].astype(BF16) for gg, d in chains]
        cb = [cc_scr[gg, pl.ds(bases[d], ch), :].astype(BF16) for gg, d in chains]
        dt_all = [_softplus(pltpu.roll(sm_ref[0, pl.ds(bases[d], ch), :], shifts[gg], 1) + bias_rows[gg])
                  for gg, d in chains]
        acum = [_dot_exact_lhs(lmats[d], dt * a_rows[gg]) for (gg, d), dt in zip(chains, dt_all)]
        sc = [jnp.where(masks[d], _dot_nt(c_, b_), 0.0) for (gg, d), c_, b_ in zip(chains, cb, bb)]
        st = [st_scr[gg, d] for gg, d in chains]
        y_off = [_dot(c_, s_.astype(BF16)) for c_, s_ in zip(cb, st)]
        dt_x = [_dot_exact_rhs(dt, emats[d]) for (gg, d), dt in zip(chains, dt_all)]
        acum_t = [_rows_transposed(sels[d], a_) for (gg, d), a_ in zip(chains, acum)]
        tot = [a_[tot_rows[d]:tot_rows[d] + 1, :] for (gg, d), a_ in zip(chains, acum)]
        decay_x = [_dot_exact_rhs(jnp.concatenate([jnp.exp(a_), jnp.exp(t_ - a_)], axis=0), emats[d])
                   for (gg, d), a_, t_ in zip(chains, acum, tot)]
        xd = [x_ * d_ for x_, d_ in zip(x, dt_x)]
        lhs, rhs = [], []
        for k, (gg, d) in enumerate(chains):
            mixes, parts = [], []
            for j in range(4):
                ln = lanes0[d] + j
                dd = acum[k][:, ln:ln + 1] - acum_t[k][j:j + 1, :]
                mixes.append((sc[k] * jnp.exp(jnp.where(masks[d], dd, NEG))).astype(BF16))
                parts.append(jnp.where(hid == j, xd[k], 0.0).astype(BF16))
            lhs.append(jnp.concatenate(mixes, axis=1))
            rhs.append(jnp.concatenate(parts, axis=0))
        y = [_dot(l_, r_) for l_, r_ in zip(lhs, rhs)]
        upd = [_dot_tn(b_, (x_ * d_[ch:]).astype(BF16)) for b_, x_, d_ in zip(bb, xd, decay_x)]
        for k, (gg, d) in enumerate(chains):
            ea_x = decay_x[k][:ch]
            etot_x = ea_x[tot_rows[d]:tot_rows[d] + 1, :]
            y_scr[gg, d, pl.ds(bases[d], ch), :] = y[k] + y_off[k] * ea_x
            st_scr[gg, d] = st[k] * etot_x + upd[k]
        return carry

    lax.fori_loop(0, nch, scan_body, 0)

    dsk_x = [_expand_heads(hp[2:3, :], 0, SSD_HEAD_DIM) for hp in hps]

    def fin_body(c, carry):
        rows = pl.ds(pl.multiple_of(c * ch, ch), ch)
        for gg in gsteps:
            y = y_scr[gg, 0, rows, :] + y_scr[gg, 1, rows, :] + xc_scr[gg, rows, :] * dsk_x[gg]
            y = y * _silu(z_ref[0, rows, xlanes[gg]])
            out_ref[0, rows, xlanes[gg]] = _rms(y, nw_ref[:, xlanes[gg]])
        return carry

    lax.fori_loop(0, nch, fin_body, 0)


def _ssd_mixer(proj3, small3, conv_w, conv_b, hp, norm_w):
    b, length, _ = proj3.shape
    ng = SSD_STEP_GROUPS
    steps = SSD_GROUPS // ng
    gw = ng * 4 * SSD_HEAD_DIM
    sw = ng * SSD_STATE
    xs0 = SSD_INNER // gw
    b0 = 2 * SSD_INNER // sw
    c0 = b0 + steps
    nb_x = SSD_INNER // sw
    seq = lambda w, off: pl.BlockSpec((1, length, w), lambda i, g: (i, 0, off + g))
    par = lambda r, w, off: pl.BlockSpec((r, w), lambda i, g: (0, off + g))
    return pl.pallas_call(
        _ssd_kernel,
        grid=(b, steps),
        in_specs=[
            seq(gw, 0), seq(gw, xs0), seq(sw, b0), seq(sw, c0),
            pl.BlockSpec((1, length, LANES), lambda i, g: (i, 0, 0)),
            par(CONV_K, gw, 0), par(CONV_K, sw, nb_x), par(CONV_K, sw, nb_x + steps),
            par(1, gw, 0), par(1, sw, nb_x), par(1, sw, nb_x + steps),
            pl.BlockSpec((SUBLANES, LANES), lambda i, g: (0, 0)),
            par(1, gw, 0),
        ],
        out_specs=pl.BlockSpec((1, length, gw), lambda i, g: (i, 0, g)),
        out_shape=jax.ShapeDtypeStruct((b, length, SSD_INNER), F32),
        scratch_shapes=[
            pltpu.VMEM((ng, length, 4 * SSD_HEAD_DIM), F32),
            pltpu.VMEM((ng, length, SSD_STATE), F32), pltpu.VMEM((ng, length, SSD_STATE), F32),
            pltpu.VMEM((ng, 2, length, 4 * SSD_HEAD_DIM), F32),
            pltpu.VMEM((ng, 2, SSD_STATE, 4 * SSD_HEAD_DIM), F32),
        ],
        compiler_params=_cparams(("parallel", "arbitrary")),
        name="ssd_mixer",
    )(proj3, proj3, proj3, proj3, small3, conv_w, conv_w, conv_w, conv_b, conv_b, conv_b, hp, norm_w)


def _gdn_kernel(q_ref, k_ref, v_ref, z_ref, sm_ref, cwq_ref, cwk_ref, cwv_ref, hp_ref, nw_ref, out_ref,
                qn_scr, kn_scr, vn_scr, u_scr, wq_scr, qk_scr, kd_scr, et_scr, o_scr):
    ch = GDN_CHUNK
    length = z_ref.shape[1]
    nch = length // ch
    dk = GDN_HEAD_DIM
    nh = GDN_STEP_HEADS
    lane_a = 2 * SSD_HEADS
    lane_beta = lane_a + 2 * GDN_HEADS

    def head_shift(hh):
        return lax.rem(LANES - (pl.program_id(1) * nh + hh), LANES)

    def head_lanes(hh):
        return pl.ds(pl.multiple_of(hh * dk, dk), dk)

    row = lax.broadcasted_iota(I32, (ch, ch), 0)
    col = lax.broadcasted_iota(I32, (ch, ch), 1)
    eye = jnp.where(row == col, 1.0, 0.0)
    n_double = (GDN_INV_BLOCK - 1).bit_length() - 1
    levels = []
    size = GDN_INV_BLOCK
    inside = (row // size) == (col // size)
    levels.append(inside)
    while size < ch:
        size *= 2
        merged = (row // size) == (col // size)
        levels.append(merged & jnp.logical_not(inside))
        inside = merged

    def conv_body(idx, carry):
        hh, c = idx // nch, lax.rem(idx, nch)
        base = pl.multiple_of(c * ch, ch)
        lanes = head_lanes(hh)
        q = _conv_silu_chunk(q_ref, cwq_ref, None, c, nch, ch, lanes)
        k = _conv_silu_chunk(k_ref, cwk_ref, None, c, nch, ch, lanes)
        qn_scr[hh, pl.ds(base, ch), :] = q * (lax.rsqrt(jnp.sum(q * q, axis=-1, keepdims=True) + EPS) * dk ** -0.5)
        kn_scr[hh, pl.ds(base, ch), :] = k * lax.rsqrt(jnp.sum(k * k, axis=-1, keepdims=True) + EPS)
        vn_scr[hh, pl.ds(base, ch), :] = _conv_silu_chunk(v_ref, cwv_ref, None, c, nch, ch, lanes)
        return carry

    lax.fori_loop(0, nh * nch, conv_body, 0)

    incl = ((row >= col), (row <= col))
    strict = ((row > col), (row < col))
    lmat = tuple(jnp.where(m, 1.0, 0.0).astype(BF16) for m in incl)
    umat = tuple(jnp.where(m, 1.0, 0.0) for m in strict)
    jref = (0, ch - 1)
    tot_row = (ch - 1, 0)
    group = min(GDN_PREP_CHUNKS, nch)

    def prep_body(idx, carry):
        hh, gi = idx // (nch // group), lax.rem(idx, nch // group)
        shift = head_shift(hh)
        hp = pltpu.roll(hp_ref[...], shift, 1)
        bias_row = hp[0:1, :]
        acoef_row = jnp.exp(hp[1:2, :])
        chains = []
        for ci in range(group):
            c = gi * group + ci
            base = pl.multiple_of(c * ch, ch)
            q = qn_scr[hh, pl.ds(base, ch), :]
            k = kn_scr[hh, pl.ds(base, ch), :]
            v = vn_scr[hh, pl.ds(base, ch), :]
            sm = pltpu.roll(sm_ref[0, pl.ds(base, ch), :], shift, 1)
            g_all = -acoef_row * _softplus(sm + bias_row)
            beta_all = jax.nn.sigmoid(sm)
            kbf = k.astype(BF16)
            kq = _dot_nt(jnp.concatenate([kbf, q.astype(BF16)], axis=0), kbf)
            for di in (0, 1):
                ln = lane_a + di * GDN_HEADS
                lb = lane_beta + di * GDN_HEADS
                chains.append(dict(c=c, di=di, q=q, k=k, v=v, kq=kq, g=g_all[:, ln:ln + 1],
                                   beta=beta_all[:, lb:lb + 1]))
        for t in chains:
            t["dmat"] = _dot_exact_lhs(lmat[t["di"]], t["g"] * umat[t["di"]], terms=2)
        for t in chains:
            di = t["di"]
            t["kb"] = t["k"] * t["beta"]
            dec = t["dmat"][:, jref[di]:jref[di] + 1] + t["g"][jref[di]:jref[di] + 1, :]
            t["dec"] = dec
            t["tot"] = dec[tot_row[di]:tot_row[di] + 1, :]
            gam = jnp.exp(jnp.where(incl[di], t["dmat"], NEG))
            t["qk"] = (t["kq"][ch:] * gam).astype(BF16)
            a = jnp.where(strict[di], t["kq"][:ch] * (gam * t["beta"]), 0.0)
            ad = jnp.where(levels[0], a, 0.0)
            t["p"] = eye - ad
            t["ab"] = ad.astype(BF16)
            t["off"] = [jnp.where(m, a, 0.0).astype(BF16) for m in levels[1:]]
        for t in chains:
            t["x"] = _dot(t["ab"], t["ab"])
        for i in range(n_double):
            last = i == n_double - 1
            for t in chains:
                xb = t["x"].astype(BF16)
                pb = t["p"].astype(BF16)
                t["r"] = _dot(xb, pb) if last else _dot(xb, jnp.concatenate([pb, xb], axis=1))
            for t in chains:
                t["p"] = t["p"] + t["r"][:, :ch]
                if not last:
                    t["x"] = t["r"][:, ch:]
        for lv in range(len(levels) - 1):
            for t in chains:
                t["m"] = _dot(t["off"][lv], t["p"].astype(BF16))
            for t in chains:
                t["p"] = t["p"] - _dot(t["p"].astype(BF16), t["m"].astype(BF16))
        for t in chains:
            t["edec"] = jnp.exp(t["dec"])
            rhs = jnp.concatenate([t["v"] * t["beta"], t["kb"] * t["edec"]], axis=1).astype(BF16)
            t["uw"] = _dot(t["p"].astype(BF16), rhs)
        for t in chains:
            c, di = t["c"], t["di"]
            base = pl.multiple_of(c * ch, ch)
            base2 = pl.multiple_of(c * 2 * ch, 2 * ch)
            u_scr[hh, di, pl.ds(base, ch), :] = t["uw"][:, :dk]
            wq_scr[hh, di, pl.ds(base2, 2 * ch), :] = jnp.concatenate(
                [t["uw"][:, dk:], t["q"] * t["edec"]], axis=0).astype(BF16)
            qk_scr[hh, di, pl.ds(base, ch), :] = t["qk"]
            kd_scr[hh, di, pl.ds(base, ch), :] = (t["k"] * jnp.exp(t["tot"] - t["dec"])).astype(BF16)
            et_scr[hh, di, pl.ds(pl.multiple_of(c * SUBLANES, SUBLANES), SUBLANES), :] = jnp.broadcast_to(
                jnp.exp(t["tot"]), (SUBLANES, LANES))
        return carry

    lax.fori_loop(0, nh * (nch // group), prep_body, 0)

    chains = [(hh, di) for hh in range(nh) for di in (0, 1)]

    def rec_body(ci, carry):
        cs = (ci, nch - 1 - ci)
        bases = [pl.multiple_of(c * ch, ch) for c in cs]
        ws = [_dot(wq_scr[hh, di, pl.ds(pl.multiple_of(cs[di] * 2 * ch, 2 * ch), 2 * ch), :], s.astype(BF16))
              for (hh, di), s in zip(chains, carry)]
        v_new = [(u_scr[hh, di, pl.ds(bases[di], ch), :] - w[:ch]).astype(BF16) for (hh, di), w in zip(chains, ws)]
        upd = [_dot_tn(kd_scr[hh, di, pl.ds(bases[di], ch), :], vn) for (hh, di), vn in zip(chains, v_new)]
        intra = [_dot(qk_scr[hh, di, pl.ds(bases[di], ch), :], vn) for (hh, di), vn in zip(chains, v_new)]
        s_new = []
        for j, (hh, di) in enumerate(chains):
            et = et_scr[hh, di, pl.ds(pl.multiple_of(cs[di] * SUBLANES, SUBLANES), 1), :]
            s_new.append(carry[j] * et + upd[j])
            o_scr[hh, di, pl.ds(bases[di], ch), :] = ws[j][ch:] + intra[j]
        return tuple(s_new)

    zero = jnp.zeros((dk, dk), F32)
    lax.fori_loop(0, nch, rec_body, (zero,) * len(chains))

    def fin_body(idx, carry):
        hh, c = idx // nch, lax.rem(idx, nch)
        rows = pl.ds(pl.multiple_of(c * ch, ch), ch)
        lanes = head_lanes(hh)
        o = o_scr[hh, 0, rows, :] + o_scr[hh, 1, rows, :]
        out_ref[0, rows, lanes] = _rms(o, nw_ref[...]) * _silu(z_ref[0, rows, lanes])
        return carry

    lax.fori_loop(0, nh * nch, fin_body, 0)


def _gdn_mixer(proj3, small3, conv_w, hp, norm_w):
    b, length, _ = proj3.shape
    dk = GDN_HEAD_DIM
    nch = length // GDN_CHUNK
    nh = GDN_STEP_HEADS
    groups = GDN_HEADS // nh
    q0 = (2 * SSD_INNER + 2 * SSD_GROUPS * SSD_STATE) // (nh * dk)
    seq = lambda off: pl.BlockSpec((1, length, nh * dk), lambda i, h: (i, 0, off + h))
    par = lambda off: pl.BlockSpec((CONV_K, nh * dk), lambda i, h: (0, off + h))
    return pl.pallas_call(
        _gdn_kernel,
        grid=(b, groups),
        in_specs=[
            seq(q0), seq(q0 + groups), seq(q0 + 2 * groups), seq(q0 + 3 * groups),
            pl.BlockSpec((1, length, LANES), lambda i, h: (i, 0, 0)),
            par(0), par(groups), par(2 * groups),
            pl.BlockSpec((SUBLANES, LANES), lambda i, h: (0, 0)),
            pl.BlockSpec((1, dk), lambda i, h: (0, 0)),
        ],
        out_specs=pl.BlockSpec((1, length, nh * dk), lambda i, h: (i, 0, h)),
        out_shape=jax.ShapeDtypeStruct((b, length, GDN_INNER), F32),
        scratch_shapes=[
            pltpu.VMEM((nh, length, dk), F32), pltpu.VMEM((nh, length, dk), F32), pltpu.VMEM((nh, length, dk), F32),
            pltpu.VMEM((nh, 2, length, dk), F32),
            pltpu.VMEM((nh, 2, 2 * length, dk), BF16),
            pltpu.VMEM((nh, 2, length, GDN_CHUNK), BF16),
            pltpu.VMEM((nh, 2, length, dk), BF16),
            pltpu.VMEM((nh, 2, nch * SUBLANES, LANES), F32),
            pltpu.VMEM((nh, 2, length, dk), F32),
        ],
        compiler_params=_cparams(("parallel", "arbitrary")),
        name="gdn_mixer",
    )(proj3, proj3, proj3, proj3, small3, conv_w, conv_w, conv_w, hp, norm_w)


def _merge_kernel(x_ref, ya_ref, yb_ref, ga_ref, gb_ref, wa_ref, wb_ref, wo_ref, gf_ref, h_ref, n_ref):
    ya = _dot(ya_ref[...].astype(BF16), wa_ref[...])
    yb = _dot(yb_ref[...].astype(BF16), wb_ref[...])
    merged = jax.nn.sigmoid(ga_ref[...]) * ya + jax.nn.sigmoid(gb_ref[...]) * yb
    h = x_ref[...] + _dot(merged.astype(BF16), wo_ref[...])
    h_ref[...] = h
    n_ref[...] = _rms(h, gf_ref[...]).astype(BF16)


def _merge(x2d, ya, yb, proj2d, wa, wb, wo, gf):
    t, d = x2d.shape
    tm = min(TM_MERGE, t)
    ga0 = (proj2d.shape[1] - 2 * d) // d
    tok = lambda: pl.BlockSpec((tm, d), lambda i: (i, 0))
    wsp = lambda: pl.BlockSpec((d, d), lambda i: (0, 0))
    return pl.pallas_call(
        _merge_kernel,
        grid=(t // tm,),
        in_specs=[tok(), tok(), tok(),
                  pl.BlockSpec((tm, d), lambda i: (i, ga0)), pl.BlockSpec((tm, d), lambda i: (i, ga0 + 1)),
                  wsp(), wsp(), wsp(), pl.BlockSpec((1, d), lambda i: (0, 0))],
        out_specs=[tok(), tok()],
        out_shape=[jax.ShapeDtypeStruct((t, d), F32), jax.ShapeDtypeStruct((t, d), BF16)],
        compiler_params=_cparams(("parallel",)),
        name="merge",
    )(x2d, ya, yb, proj2d, proj2d, wa, wb, wo, gf)


def _topk_store(problems, k):
    state = [p[0] for p in problems]
    iotas = [lax.broadcasted_iota(I32, s.shape, 0).astype(F32) for s in state]
    for i in range(k):
        ms = [jnp.max(s, axis=0, keepdims=True) for s in state]
        cands = [jnp.where(s == m, io, float(s.shape[0])) for s, m, io in zip(state, ms, iotas)]
        ixs = [jnp.min(c, axis=0, keepdims=True) for c in cands]
        hits = [c == ix for c, ix in zip(cands, ixs)]
        for (_, vals_ref, outs_ref, payload), m, ix, hit in zip(problems, ms, ixs, hits):
            vals_ref[i:i + 1, :] = m
            outs_ref[i:i + 1, :] = (ix if payload is None
                                    else jnp.max(jnp.where(hit, payload, -1.0), axis=0, keepdims=True))
        state = [jnp.where(hit, -jnp.inf, s) for s, hit in zip(state, hits)]


def _route_unit(tok_ref, wq_ref, keys_ref, e_scr, gt_scr, tv_scr, ti_scr, bv_scr, be_scr, head, tile):
    kk = PEER_TOPK
    half = kk // 2
    toks = tok_ref[pl.ds(pl.multiple_of(tile * LANES, LANES), LANES), :]
    qry = _dot(toks, wq_ref[...])
    st = [_dot_nt(keys_ref[0, z], qry[:, z * PEER_HALF:(z + 1) * PEER_HALF].astype(BF16)) for z in range(2)]
    _topk_store([(st[z], tv_scr.at[z], ti_scr.at[z], None) for z in range(2)], kk)
    s0, s1 = tv_scr[0], tv_scr[1]
    e0, i1 = ti_scr[0] * float(PEER_KEYS), ti_scr[1]
    cs = [s0[0:1] + s1] + [s0[i:i + 1] + s1[0:half] for i in range(1, half)] + [s0[half:] + s1[0:1]]
    ce = [e0[0:1] + i1] + [e0[i:i + 1] + i1[0:half] for i in range(1, half)] + [e0[half:] + i1[0:1]]
    _topk_store([(jnp.concatenate(cs, axis=0), bv_scr, be_scr, jnp.concatenate(ce, axis=0))], kk)
    best = bv_scr[...]
    ex = jnp.exp(best - best[0:1])
    rows = pl.ds(pl.multiple_of(head * kk, kk), kk)
    lanes = pl.ds(pl.multiple_of(tile * LANES, LANES), LANES)
    e_scr[rows, lanes] = be_scr[...]
    gt_scr[rows, lanes] = ex / jnp.sum(ex, axis=0, keepdims=True)


def _peer_kernel(nm_ref, nr_ref, wq_ref, keys_ref, u_ref, v_ref, out_ref,
                 g_scr, e_scr, gt_scr, et_scr, gtt_scr, tv_scr, ti_scr, bv_scr, be_scr):
    tm = nm_ref.shape[0]
    nk = PEER_KEYS
    ntile = tm // LANES
    tile_i = pl.program_id(0)
    step = pl.program_id(1)

    @pl.when((tile_i == 0) & (step == 0))
    def _():
        e_scr[...] = jnp.zeros_like(e_scr)
        gt_scr[...] = jnp.zeros_like(gt_scr)

    @pl.when(step == 0)
    def _():
        out_ref[...] = jnp.zeros_like(out_ref)
        et_scr[...] = e_scr[...].T.astype(I32)
        gtt_scr[...] = gt_scr[...].T
        sub = lax.broadcasted_iota(I32, (nk, LANES), 0)

        zero = jnp.zeros((nk, LANES), BF16)

        def pair_body(tp, carry):
            t0 = 2 * tp
            ats, bts = [], []
            for o in range(2):
                re = et_scr[pl.ds(t0 + o, 1), :]
                rg = 0.5 * gtt_scr[pl.ds(t0 + o, 1), :]
                ats.append(jnp.where(sub == (re >> 7), rg, 0.0).astype(BF16))
                bts.append(jnp.where(sub == (re & (nk - 1)), 1.0, 0.0).astype(BF16))
            bt = jnp.concatenate([jnp.concatenate([bts[0], zero], axis=1),
                                  jnp.concatenate([zero, bts[1]], axis=1)], axis=0)
            g2 = _dot_nt(jnp.concatenate(ats, axis=1), bt)
            lo = pltpu.bitcast(g2[:, :LANES].astype(BF16).astype(F32), jnp.uint32) >> 16
            hi = pltpu.bitcast(g2[:, LANES:].astype(BF16).astype(F32), jnp.uint32) & jnp.uint32(0xFFFF0000)
            g_scr[pl.ds(tp, nk, stride=G_PITCH), :] = lo | hi
            return carry

        lax.fori_loop(0, tm // 2, pair_body, 0, unroll=TOKEN_UNROLL // 2)

    per_step = PEER_HEADS * ntile * EXP_ROWS // nk
    for k in range(per_step):
        unit = step * per_step + k
        _route_unit(nr_ref, wq_ref, keys_ref, e_scr, gt_scr, tv_scr, ti_scr, bv_scr, be_scr,
                    unit // ntile, lax.rem(unit, ntile))

    hid = _dot_nt(nm_ref[...], u_ref[...])
    act = hid * (1.0 + lax.erf(hid * (0.5 ** 0.5)))
    gates = [pltpu.bitcast(g_scr[pl.ds(pl.multiple_of((step * EXP_ROWS + r) * G_PITCH, SUBLANES), tm // 2), :], BF16)
             for r in range(EXP_ROWS)]
    wts = act.astype(BF16) * jnp.concatenate(gates, axis=1)
    out_ref[...] = out_ref[...] + _dot(wts, v_ref[...])


def _peer(n2, wq, keys, eu, ev):
    t, d = n2.shape
    tm = min(TM_EXP, t)
    assert tm == TM_EXP, "gate slab pitch is derived from TM_EXP"
    assert PEER_KEYS == LANES, "expert ids are decoded with a 7-bit shift"
    hk = PEER_HEADS * PEER_TOPK
    er = EXP_ROWS * PEER_KEYS
    nsteps = eu.shape[0] // er
    nt = t // tm
    units = PEER_HEADS * (tm // LANES)
    assert units % nsteps == 0 and (tm // LANES) % (units // nsteps) == 0, "whole routing units of one head per step"
    spr = nsteps // PEER_HEADS
    qw = 2 * PEER_HALF
    prev = lambda i, s: (jnp.maximum(i - 1, 0), 0)
    return pl.pallas_call(
        _peer_kernel,
        grid=(nt + 1, nsteps),
        in_specs=[pl.BlockSpec((tm, d), prev),
                  pl.BlockSpec((tm, d), lambda i, s: (jnp.minimum(i, nt - 1), 0)),
                  pl.BlockSpec((d, qw), lambda i, s: (0, s // spr)),
                  pl.BlockSpec((1, 2, PEER_KEYS, PEER_HALF), lambda i, s: (s // spr, 0, 0, 0)),
                  pl.BlockSpec((er, d), lambda i, s: (s, 0)),
                  pl.BlockSpec((er, d), lambda i, s: (s, 0))],
        out_specs=pl.BlockSpec((tm, d), prev),
        out_shape=jax.ShapeDtypeStruct((t, d), F32),
        scratch_shapes=[pltpu.VMEM((PEER_KEYS * G_PITCH, LANES), jnp.uint32),
                        pltpu.VMEM((hk, tm), F32), pltpu.VMEM((hk, tm), F32),
                        pltpu.VMEM((tm, hk), I32), pltpu.VMEM((tm, hk), F32),
                        pltpu.VMEM((2, PEER_TOPK, LANES), F32), pltpu.VMEM((2, PEER_TOPK, LANES), F32),
                        pltpu.VMEM((PEER_TOPK, LANES), F32), pltpu.VMEM((PEER_TOPK, LANES), F32)],
        compiler_params=_cparams(("arbitrary", "arbitrary")),
        name="peer_ffn",
    )(n2, n2, wq, keys, eu, ev)


def _ple_kernel(h_ref, f_ref, p_ref, gp_ref, wg_ref, wp_ref, gf_ref, y_ref):
    h = h_ref[...] + f_ref[...]
    gate = jax.nn.sigmoid(_dot(_rms(h, gp_ref[...]).astype(BF16), wg_ref[...]))
    h = h + gate * _dot(p_ref[...].astype(BF16), wp_ref[...])
    y_ref[...] = _rms(h, gf_ref[...])


def _ple(h1, ffn, p2d, gp, wg, wp, gfin):
    t, d = h1.shape
    pd = p2d.shape[1]
    tm = min(TM_PLE, t)
    vec = lambda: pl.BlockSpec((1, d), lambda i: (0, 0))
    tok = lambda: pl.BlockSpec((tm, d), lambda i: (i, 0))
    return pl.pallas_call(
        _ple_kernel,
        grid=(t // tm,),
        in_specs=[tok(), tok(), pl.BlockSpec((tm, pd), lambda i: (i, 0)),
                  vec(), pl.BlockSpec((d, d), lambda i: (0, 0)), pl.BlockSpec((pd, d), lambda i: (0, 0)), vec()],
        out_specs=tok(),
        out_shape=jax.ShapeDtypeStruct((t, d), F32),
        compiler_params=_cparams(("parallel",)),
        name="ple_final",
    )(h1, ffn, p2d, gp, wg, wp, gfin)


def _pad_lanes(v):
    return jnp.pad(v, (0, LANES - v.shape[0]))


def _prepare(norm_mix, w_in, ssd_conv_w, ssd_conv_b, ssd_a_log, ssd_dt_bias, ssd_d, ssd_norm, w_ssd_out,
             gdn_conv_w, gdn_a_log, gdn_dt_bias, gdn_norm, w_gdn_out, w_out, norm_ffn, peer_query, peer_keys,
             expert_u, expert_v, norm_ple, w_ple_gate, w_ple_proj, norm_final):
    w = w_in[0]
    xbc = SSD_INNER + 2 * SSD_GROUPS * SSD_STATE
    c_dt = SSD_INNER + xbc
    c_qkv = c_dt + 2 * SSD_HEADS
    c_a = c_qkv + 4 * GDN_INNER
    c_ga = c_a + 4 * GDN_HEADS
    w_main = jnp.concatenate([w[:, :c_dt], w[:, c_qkv:c_a], w[:, c_ga:]], axis=1).astype(BF16)
    w_small = jnp.concatenate([w[:, c_dt:c_qkv], w[:, c_a:c_ga]], axis=1)
    w_small = jnp.pad(w_small, ((0, 0), (0, LANES - w_small.shape[1]))).astype(BF16)
    hp = jnp.stack([
        _pad_lanes(jnp.concatenate([ssd_dt_bias[0].reshape(-1), gdn_dt_bias[0].reshape(-1)])),
        _pad_lanes(jnp.concatenate([ssd_a_log[0].reshape(-1), gdn_a_log[0].reshape(-1)])),
        _pad_lanes(ssd_d[0]),
    ])
    hp = jnp.pad(hp, ((0, SUBLANES - hp.shape[0]), (0, 0)))
    row = lambda v: v.reshape(1, -1)
    return dict(
        g_mix=row(norm_mix[0]), w_main=w_main, w_small=w_small, hp=hp,
        ssd_conv_w=ssd_conv_w[0], ssd_conv_b=row(ssd_conv_b[0]), ssd_norm=row(ssd_norm[0]),
        gdn_conv_w=gdn_conv_w[0], gdn_norm=row(gdn_norm[0]),
        wa=w_ssd_out[0].astype(BF16), wb=w_gdn_out[0].astype(BF16), wo=w_out[0].astype(BF16),
        g_ffn=row(norm_ffn[0]), wq=peer_query[0].astype(BF16),
        keys=peer_keys[0].astype(BF16),
        eu=expert_u[0].astype(BF16), ev=expert_v[0].astype(BF16),
        g_ple=row(norm_ple[0]), wg=w_ple_gate[0].astype(BF16), wp=w_ple_proj[0].astype(BF16),
        g_fin=row(norm_final),
    )


def _trunk(x, ple, p):
    b, length, d = x.shape
    t = b * length
    x2d = x.reshape(t, d)
    proj, small = _in_proj(x2d, p["g_mix"], p["w_main"], p["w_small"])
    proj3 = proj.reshape(b, length, -1)
    small3 = small.reshape(b, length, LANES)
    ya = _ssd_mixer(proj3, small3, p["ssd_conv_w"], p["ssd_conv_b"], p["hp"], p["ssd_norm"])
    yb = _gdn_mixer(proj3, small3, p["gdn_conv_w"], p["hp"], p["gdn_norm"])
    h1, n2 = _merge(x2d, ya.reshape(t, -1), yb.reshape(t, -1), proj, p["wa"], p["wb"], p["wo"], p["g_ffn"])
    ffn = _peer(n2, p["wq"], p["keys"], p["eu"], p["ev"])
    y = _ple(h1, ffn, ple.reshape(t, -1), p["g_ple"], p["wg"], p["wp"], p["g_fin"])
    return y.reshape(b, length, d)


def kernel(x_prompt, x_sample, p_prompt, p_sample, norm_mix, w_in, ssd_conv_w, ssd_conv_b, ssd_a_log, ssd_dt_bias, ssd_d, ssd_norm, w_ssd_out, gdn_conv_w, gdn_a_log, gdn_dt_bias, gdn_norm, w_gdn_out, w_out, norm_ffn, peer_query, peer_keys, expert_u, expert_v, norm_ple, w_ple_gate, w_ple_proj, norm_final):
    assert w_in.shape[0] == 1, "single-layer trunk"
    p = _prepare(norm_mix, w_in, ssd_conv_w, ssd_conv_b, ssd_a_log, ssd_dt_bias, ssd_d, ssd_norm, w_ssd_out,
                 gdn_conv_w, gdn_a_log, gdn_dt_bias, gdn_norm, w_gdn_out, w_out, norm_ffn, peer_query, peer_keys,
                 expert_u, expert_v, norm_ple, w_ple_gate, w_ple_proj, norm_final)
    return (_trunk(x_prompt, p_prompt[0], p), _trunk(x_sample, p_sample[0], p))
```

```python
import jax
import jax.numpy as jnp
from jax import lax
from jax.experimental import pallas as pl
from jax.experimental.pallas import tpu as pltpu

F32 = jnp.float32
BF16 = jnp.bfloat16
I32 = jnp.int32

EPS = 1e-6
CONV_K = 4
SSD_HEADS = 16
SSD_HEAD_DIM = 64
SSD_GROUPS = 4
SSD_STATE = 128
SSD_INNER = SSD_HEADS * SSD_HEAD_DIM
GDN_HEADS = 8
GDN_HEAD_DIM = 128
GDN_INNER = GDN_HEADS * GDN_HEAD_DIM
PEER_HEADS = 8
PEER_KEYS = 128
PEER_TOPK = 16
PEER_HALF = 128
PEER_KEY_BITS = PEER_KEYS.bit_length() - 1

LANES = 128
SUBLANES = 8
SSD_CHUNK = 128
SSD_STEP_GROUPS = 2
GDN_CHUNK = 128
GDN_PREP_CHUNKS = 8
GDN_STEP_HEADS = 2
GDN_INV_BLOCK = 8
VMEM_LIMIT = 56 * 1024 * 1024

TM_PROJ = 2048
TN_PROJ = 1024
TM_MERGE = 512
TM_EXP = 512
EXP_ROWS = 16
G_PITCH = TM_EXP // 2 + 8
TOKEN_UNROLL = 64
TM_PLE = 1024

NEG = -1e30


def _dot(a, b):
    return jnp.dot(a, b, preferred_element_type=F32)


def _dot_nt(a, b):
    return lax.dot_general(a, b, (((1,), (1,)), ((), ())), preferred_element_type=F32)


def _dot_tn(a, b):
    return lax.dot_general(a, b, (((0,), (0,)), ((), ())), preferred_element_type=F32)


def _bf16_terms(m, terms):
    out = []
    for _ in range(terms - 1):
        t = m.astype(BF16)
        out.append(t)
        m = m - t.astype(F32)
    out.append(m.astype(BF16))
    return out


def _dot_exact_lhs(lb, m, terms=3):
    n = m.shape[1]
    r = _dot(lb, jnp.concatenate(_bf16_terms(m, terms), axis=1))
    return sum(r[:, i * n:(i + 1) * n] for i in range(1, terms)) + r[:, :n]


def _dot_exact_rhs(m, eb, terms=2):
    return _dot(jnp.concatenate(_bf16_terms(m, terms), axis=1), jnp.concatenate([eb] * terms, axis=0))


def _rows_transposed(sel, m, terms=3):
    n = m.shape[0]
    r = _dot_nt(sel, jnp.concatenate(_bf16_terms(m, terms), axis=0))
    return sum(r[:, i * n:(i + 1) * n] for i in range(1, terms)) + r[:, :n]


def _softplus(x):
    return jnp.maximum(x, 0.0) + jnp.log1p(jnp.exp(-jnp.abs(x)))


def _silu(x):
    return x * jax.nn.sigmoid(x)


def _rms(x, g):
    return x * lax.rsqrt(jnp.mean(x * x, axis=-1, keepdims=True) + EPS) * g


def _cparams(sem):
    return pltpu.CompilerParams(dimension_semantics=sem, vmem_limit_bytes=VMEM_LIMIT)


def _in_proj_kernel(x_ref, g_ref, wm_ref, ws_ref, main_ref, small_ref, n_scr):
    @pl.when(pl.program_id(1) == 0)
    def _():
        n = _rms(x_ref[...], g_ref[...]).astype(BF16)
        n_scr[...] = n
        small_ref[...] = _dot(n, ws_ref[...])

    main_ref[...] = _dot(n_scr[...], wm_ref[...])


def _in_proj(x2d, g, w_main, w_small):
    t, d = x2d.shape
    n_main = w_main.shape[1]
    tm = min(TM_PROJ, t)
    return pl.pallas_call(
        _in_proj_kernel,
        grid=(t // tm, n_main // TN_PROJ),
        in_specs=[
            pl.BlockSpec((tm, d), lambda i, j: (i, 0)),
            pl.BlockSpec((1, d), lambda i, j: (0, 0)),
            pl.BlockSpec((d, TN_PROJ), lambda i, j: (0, j)),
            pl.BlockSpec((d, LANES), lambda i, j: (0, 0)),
        ],
        out_specs=[
            pl.BlockSpec((tm, TN_PROJ), lambda i, j: (i, j)),
            pl.BlockSpec((tm, LANES), lambda i, j: (i, 0)),
        ],
        out_shape=[jax.ShapeDtypeStruct((t, n_main), F32), jax.ShapeDtypeStruct((t, LANES), F32)],
        scratch_shapes=[pltpu.VMEM((tm, d), BF16)],
        compiler_params=_cparams(("parallel", "arbitrary")),
        name="in_proj",
    )(x2d, g, w_main, w_small)


def _conv_silu_chunk(src_ref, w_ref, b_ref, c, nch, rows, lanes=slice(None)):
    length = src_ref.shape[1]
    base = pl.multiple_of(c * rows, rows)
    main = src_ref[0, pl.ds(base, rows), lanes]
    prev = src_ref[0, pl.ds(pl.multiple_of(jnp.maximum(base - SUBLANES, 0), SUBLANES), SUBLANES), lanes]
    nxt = src_ref[0, pl.ds(pl.multiple_of(jnp.minimum(base + rows, length - SUBLANES), SUBLANES), SUBLANES), lanes]
    prev = jnp.where(c > 0, prev, 0.0)
    nxt = jnp.where(c < nch - 1, nxt, 0.0)
    ext = jnp.concatenate([prev, main, nxt], axis=0)
    w = w_ref[:, lanes]
    lo = SUBLANES - CONV_K // 2
    y = ext[lo:lo + rows] * w[0:1]
    for j in range(1, CONV_K):
        y = y + ext[lo + j:lo + j + rows] * w[j:j + 1]
    if b_ref is not None:
        y = y + b_ref[:, lanes]
    return _silu(y)


def _expand_heads(t, lane0, width):
    r = t.shape[0]
    hid = lax.broadcasted_iota(I32, (r, 4 * width), 1) // width
    out = jnp.broadcast_to(t[:, lane0 + 3:lane0 + 4], (r, 4 * width))
    for j in (2, 1, 0):
        out = jnp.where(hid == j, t[:, lane0 + j:lane0 + j + 1], out)
    return out


def _ssd_kernel(z_ref, xs_ref, bm_ref, cm_ref, sm_ref, cwx_ref, cwb_ref, cwc_ref, cbx_ref, cbb_ref, cbc_ref,
                hp_ref, nw_ref, out_ref, xc_scr, bc_scr, cc_scr, y_scr, st_scr):
    ch = SSD_CHUNK
    length = z_ref.shape[1]
    nch = length // ch
    gwid = 4 * SSD_HEAD_DIM
    ng = SSD_STEP_GROUPS
    gsteps = range(ng)
    shifts = [lax.rem(LANES - 4 * (pl.program_id(1) * ng + gg), LANES) for gg in gsteps]
    hps = [pltpu.roll(hp_ref[...], s, 1) for s in shifts]
    bias_rows = [hp[0:1, :] for hp in hps]
    a_rows = [-jnp.exp(hp[1:2, :]) for hp in hps]
    xlanes = [slice(gg * gwid, (gg + 1) * gwid) for gg in gsteps]
    slanes = [slice(gg * SSD_STATE, (gg + 1) * SSD_STATE) for gg in gsteps]

    row = lax.broadcasted_iota(I32, (ch, ch), 0)
    col = lax.broadcasted_iota(I32, (ch, ch), 1)
    hid = lax.broadcasted_iota(I32, (ch, gwid), 1) // SSD_HEAD_DIM

    def conv_body(c, carry):
        base = pl.multiple_of(c * ch, ch)
        for gg in gsteps:
            xc_scr[gg, pl.ds(base, ch), :] = _conv_silu_chunk(xs_ref, cwx_ref, cbx_ref, c, nch, ch, xlanes[gg])
            bc_scr[gg, pl.ds(base, ch), :] = _conv_silu_chunk(bm_ref, cwb_ref, cbb_ref, c, nch, ch, slanes[gg])
            cc_scr[gg, pl.ds(base, ch), :] = _conv_silu_chunk(cm_ref, cwc_ref, cbc_ref, c, nch, ch, slanes[gg])
        return carry

    lax.fori_loop(0, nch, conv_body, 0)

    masks = ((row >= col), (row <= col))
    lmats = tuple(jnp.where(m, 1.0, 0.0).astype(BF16) for m in masks)
    lanes0 = (0, SSD_HEADS)
    tot_rows = (ch - 1, 0)
    erow = lax.broadcasted_iota(I32, (LANES, gwid), 0)
    ecol = lax.broadcasted_iota(I32, (LANES, gwid), 1) // SSD_HEAD_DIM
    emats = tuple(jnp.where(erow == ecol + l0, 1.0, 0.0).astype(BF16) for l0 in lanes0)
    srow = lax.broadcasted_iota(I32, (SUBLANES, LANES), 0)
    scol = lax.broadcasted_iota(I32, (SUBLANES, LANES), 1)
    sels = tuple(jnp.where(scol == srow + l0, 1.0, 0.0).astype(BF16) for l0 in lanes0)
    chains = [(gg, d) for gg in gsteps for d in (0, 1)]

    st_scr[...] = jnp.zeros_like(st_scr)

    def scan_body(ci, carry):
        bases = (pl.multiple_of(ci * ch, ch), pl.multiple_of((nch - 1 - ci) * ch, ch))
        x = [xc_scr[gg, pl.ds(bases[d], ch), :] for gg, d in chains]
        bb = [bc_scr[gg, pl.ds(bases[d], ch), :].astype(BF16) for gg, d in chains]
        cb = [cc_scr[gg, pl.ds(bases[d], ch), :].astype(BF16) for gg, d in chains]
        dt_all = [_softplus(pltpu.roll(sm_ref[0, pl.ds(bases[d], ch), :], shifts[gg], 1) + bias_rows[gg])
                  for gg, d in chains]
        acum = [_dot_exact_lhs(lmats[d], dt * a_rows[gg]) for (gg, d), dt in zip(chains, dt_all)]
        sc = [jnp.where(masks[d], _dot_nt(c_, b_), 0.0) for (gg, d), c_, b_ in zip(chains, cb, bb)]
        st = [st_scr[gg, d] for gg, d in chains]
        y_off = [_dot(c_, s_.astype(BF16)) for c_, s_ in zip(cb, st)]
        dt_x = [_dot_exact_rhs(dt, emats[d]) for (gg, d), dt in zip(chains, dt_all)]
        acum_t = [_rows_transposed(sels[d], a_) for (gg, d), a_ in zip(chains, acum)]
        tot = [a_[tot_rows[d]:tot_rows[d] + 1, :] for (gg, d), a_ in zip(chains, acum)]
        decay_x = [_dot_exact_rhs(jnp.concatenate([jnp.exp(a_), jnp.exp(t_ - a_)], axis=0), emats[d])
                   for (gg, d), a_, t_ in zip(chains, acum, tot)]
        xd = [x_ * d_ for x_, d_ in zip(x, dt_x)]
        lhs, rhs = [], []
        for k, (gg, d) in enumerate(chains):
            mixes, parts = [], []
            for j in range(4):
                ln = lanes0[d] + j
                dd = acum[k][:, ln:ln + 1] - acum_t[k][j:j + 1, :]
                mixes.append((sc[k] * jnp.exp(jnp.where(masks[d], dd, NEG))).astype(BF16))
                parts.append(jnp.where(hid == j, xd[k], 0.0).astype(BF16))
            lhs.append(jnp.concatenate(mixes, axis=1))
            rhs.append(jnp.concatenate(parts, axis=0))
        y = [_dot(l_, r_) for l_, r_ in zip(lhs, rhs)]
        upd = [_dot_tn(b_, (x_ * d_[ch:]).astype(BF16)) for b_, x_, d_ in zip(bb, xd, decay_x)]
        for k, (gg, d) in enumerate(chains):
            ea_x = decay_x[k][:ch]
            etot_x = ea_x[tot_rows[d]:tot_rows[d] + 1, :]
            y_scr[gg, d, pl.ds(bases[d], ch), :] = y[k] + y_off[k] * ea_x
            st_scr[gg, d] = st[k] * etot_x + upd[k]
        return carry

    lax.fori_loop(0, nch, scan_body, 0)

    dsk_x = [_expand_heads(hp[2:3, :], 0, SSD_HEAD_DIM) for hp in hps]

    def fin_body(c, carry):
        rows = pl.ds(pl.multiple_of(c * ch, ch), ch)
        for gg in gsteps:
            y = y_scr[gg, 0, rows, :] + y_scr[gg, 1, rows, :] + xc_scr[gg, rows, :] * dsk_x[gg]
            y = y * _silu(z_ref[0, rows, xlanes[gg]])
            out_ref[0, rows, xlanes[gg]] = _rms(y, nw_ref[:, xlanes[gg]])
        return carry

    lax.fori_loop(0, nch, fin_body, 0)


def _ssd_mixer(proj3, small3, conv_w, conv_b, hp, norm_w):
    b, length, _ = proj3.shape
    ng = SSD_STEP_GROUPS
    steps = SSD_GROUPS // ng
    gw = ng * 4 * SSD_HEAD_DIM
    sw = ng * SSD_STATE
    xs0 = SSD_INNER // gw
    b0 = 2 * SSD_INNER // sw
    c0 = b0 + steps
    nb_x = SSD_INNER // sw
    seq = lambda w, off: pl.BlockSpec((1, length, w), lambda i, g: (i, 0, off + g))
    par = lambda r, w, off: pl.BlockSpec((r, w), lambda i, g: (0, off + g))
    return pl.pallas_call(
        _ssd_kernel,
        grid=(b, steps),
        in_specs=[
            seq(gw, 0), seq(gw, xs0), seq(sw, b0), seq(sw, c0),
            pl.BlockSpec((1, length, LANES), lambda i, g: (i, 0, 0)),
            par(CONV_K, gw, 0), par(CONV_K, sw, nb_x), par(CONV_K, sw, nb_x + steps),
            par(1, gw, 0), par(1, sw, nb_x), par(1, sw, nb_x + steps),
            pl.BlockSpec((SUBLANES, LANES), lambda i, g: (0, 0)),
            par(1, gw, 0),
        ],
        out_specs=pl.BlockSpec((1, length, gw), lambda i, g: (i, 0, g)),
        out_shape=jax.ShapeDtypeStruct((b, length, SSD_INNER), F32),
        scratch_shapes=[
            pltpu.VMEM((ng, length, 4 * SSD_HEAD_DIM), F32),
            pltpu.VMEM((ng, length, SSD_STATE), F32), pltpu.VMEM((ng, length, SSD_STATE), F32),
            pltpu.VMEM((ng, 2, length, 4 * SSD_HEAD_DIM), F32),
            pltpu.VMEM((ng, 2, SSD_STATE, 4 * SSD_HEAD_DIM), F32),
        ],
        compiler_params=_cparams(("parallel", "arbitrary")),
        name="ssd_mixer",
    )(proj3, proj3, proj3, proj3, small3, conv_w, conv_w, conv_w, conv_b, conv_b, conv_b, hp, norm_w)


def _gdn_kernel(q_ref, k_ref, v_ref, z_ref, sm_ref, cwq_ref, cwk_ref, cwv_ref, hp_ref, nw_ref, out_ref,
                qn_scr, kn_scr, vn_scr, u_scr, wq_scr, qk_scr, kd_scr, et_scr, o_scr):
    ch = GDN_CHUNK
    length = z_ref.shape[1]
    nch = length // ch
    dk = GDN_HEAD_DIM
    nh = GDN_STEP_HEADS
    lane_a = 2 * SSD_HEADS
    lane_beta = lane_a + 2 * GDN_HEADS

    def head_shift(hh):
        return lax.rem(LANES - (pl.program_id(1) * nh + hh), LANES)

    def head_lanes(hh):
        return pl.ds(pl.multiple_of(hh * dk, dk), dk)

    row = lax.broadcasted_iota(I32, (ch, ch), 0)
    col = lax.broadcasted_iota(I32, (ch, ch), 1)
    eye = jnp.where(row == col, 1.0, 0.0)
    n_double = (GDN_INV_BLOCK - 1).bit_length() - 1
    levels = []
    size = GDN_INV_BLOCK
    inside = (row // size) == (col // size)
    levels.append(inside)
    while size < ch:
        size *= 2
        merged = (row // size) == (col // size)
        levels.append(merged & jnp.logical_not(inside))
        inside = merged

    def conv_body(idx, carry):
        hh, c = idx // nch, lax.rem(idx, nch)
        base = pl.multiple_of(c * ch, ch)
        lanes = head_lanes(hh)
        q = _conv_silu_chunk(q_ref, cwq_ref, None, c, nch, ch, lanes)
        k = _conv_silu_chunk(k_ref, cwk_ref, None, c, nch, ch, lanes)
        qn_scr[hh, pl.ds(base, ch), :] = q * (lax.rsqrt(jnp.sum(q * q, axis=-1, keepdims=True) + EPS) * dk ** -0.5)
        kn_scr[hh, pl.ds(base, ch), :] = k * lax.rsqrt(jnp.sum(k * k, axis=-1, keepdims=True) + EPS)
        vn_scr[hh, pl.ds(base, ch), :] = _conv_silu_chunk(v_ref, cwv_ref, None, c, nch, ch, lanes)
        return carry

    lax.fori_loop(0, nh * nch, conv_body, 0)

    incl = ((row >= col), (row <= col))
    strict = ((row > col), (row < col))
    lmat = tuple(jnp.where(m, 1.0, 0.0).astype(BF16) for m in incl)
    umat = tuple(jnp.where(m, 1.0, 0.0) for m in strict)
    jref = (0, ch - 1)
    tot_row = (ch - 1, 0)
    group = min(GDN_PREP_CHUNKS, nch)

    def prep_body(idx, carry):
        hh, gi = idx // (nch // group), lax.rem(idx, nch // group)
        shift = head_shift(hh)
        hp = pltpu.roll(hp_ref[...], shift, 1)
        bias_row = hp[0:1, :]
        acoef_row = jnp.exp(hp[1:2, :])
        chains = []
        for ci in range(group):
            c = gi * group + ci
            base = pl.multiple_of(c * ch, ch)
            q = qn_scr[hh, pl.ds(base, ch), :]
            k = kn_scr[hh, pl.ds(base, ch), :]
            v = vn_scr[hh, pl.ds(base, ch), :]
            sm = pltpu.roll(sm_ref[0, pl.ds(base, ch), :], shift, 1)
            g_all = -acoef_row * _softplus(sm + bias_row)
            beta_all = jax.nn.sigmoid(sm)
            kbf = k.astype(BF16)
            kq = _dot_nt(jnp.concatenate([kbf, q.astype(BF16)], axis=0), kbf)
            for di in (0, 1):
                ln = lane_a + di * GDN_HEADS
                lb = lane_beta + di * GDN_HEADS
                chains.append(dict(c=c, di=di, q=q, k=k, v=v, kq=kq, g=g_all[:, ln:ln + 1],
                                   beta=beta_all[:, lb:lb + 1]))
        for t in chains:
            t["dmat"] = _dot_exact_lhs(lmat[t["di"]], t["g"] * umat[t["di"]], terms=2)
        for t in chains:
            di = t["di"]
            t["kb"] = t["k"] * t["beta"]
            dec = t["dmat"][:, jref[di]:jref[di] + 1] + t["g"][jref[di]:jref[di] + 1, :]
            t["dec"] = dec
            t["tot"] = dec[tot_row[di]:tot_row[di] + 1, :]
            gam = jnp.exp(jnp.where(incl[di], t["dmat"], NEG))
            t["qk"] = (t["kq"][ch:] * gam).astype(BF16)
            a = jnp.where(strict[di], t["kq"][:ch] * (gam * t["beta"]), 0.0)
            ad = jnp.where(levels[0], a, 0.0)
            t["p"] = eye - ad
            t["ab"] = ad.astype(BF16)
            t["off"] = [jnp.where(m, a, 0.0).astype(BF16) for m in levels[1:]]
        for t in chains:
            t["x"] = _dot(t["ab"], t["ab"])
        for i in range(n_double):
            last = i == n_double - 1
            for t in chains:
                xb = t["x"].astype(BF16)
                pb = t["p"].astype(BF16)
                t["r"] = _dot(xb, pb) if last else _dot(xb, jnp.concatenate([pb, xb], axis=1))
            for t in chains:
                t["p"] = t["p"] + t["r"][:, :ch]
                if not last:
                    t["x"] = t["r"][:, ch:]
        for lv in range(len(levels) - 1):
            for t in chains:
                t["m"] = _dot(t["off"][lv], t["p"].astype(BF16))
            for t in chains:
                t["p"] = t["p"] - _dot(t["p"].astype(BF16), t["m"].astype(BF16))
        for t in chains:
            t["edec"] = jnp.exp(t["dec"])
            rhs = jnp.concatenate([t["v"] * t["beta"], t["kb"] * t["edec"]], axis=1).astype(BF16)
            t["uw"] = _dot(t["p"].astype(BF16), rhs)
        for t in chains:
            c, di = t["c"], t["di"]
            base = pl.multiple_of(c * ch, ch)
            base2 = pl.multiple_of(c * 2 * ch, 2 * ch)
            u_scr[hh, di, pl.ds(base, ch), :] = t["uw"][:, :dk]
            wq_scr[hh, di, pl.ds(base2, 2 * ch), :] = jnp.concatenate(
                [t["uw"][:, dk:], t["q"] * t["edec"]], axis=0).astype(BF16)
            qk_scr[hh, di, pl.ds(base, ch), :] = t["qk"]
            kd_scr[hh, di, pl.ds(base, ch), :] = (t["k"] * jnp.exp(t["tot"] - t["dec"])).astype(BF16)
            et_scr[hh, di, pl.ds(pl.multiple_of(c * SUBLANES, SUBLANES), SUBLANES), :] = jnp.broadcast_to(
                jnp.exp(t["tot"]), (SUBLANES, LANES))
        return carry

    lax.fori_loop(0, nh * (nch // group), prep_body, 0)

    chains = [(hh, di) for hh in range(nh) for di in (0, 1)]

    def rec_body(ci, carry):
        cs = (ci, nch - 1 - ci)
        bases = [pl.multiple_of(c * ch, ch) for c in cs]
        ws = [_dot(wq_scr[hh, di, pl.ds(pl.multiple_of(cs[di] * 2 * ch, 2 * ch), 2 * ch), :], s.astype(BF16))
              for (hh, di), s in zip(chains, carry)]
        v_new = [(u_scr[hh, di, pl.ds(bases[di], ch), :] - w[:ch]).astype(BF16) for (hh, di), w in zip(chains, ws)]
        upd = [_dot_tn(kd_scr[hh, di, pl.ds(bases[di], ch), :], vn) for (hh, di), vn in zip(chains, v_new)]
        intra = [_dot(qk_scr[hh, di, pl.ds(bases[di], ch), :], vn) for (hh, di), vn in zip(chains, v_new)]
        s_new = []
        for j, (hh, di) in enumerate(chains):
            et = et_scr[hh, di, pl.ds(pl.multiple_of(cs[di] * SUBLANES, SUBLANES), 1), :]
            s_new.append(carry[j] * et + upd[j])
            o_scr[hh, di, pl.ds(bases[di], ch), :] = ws[j][ch:] + intra[j]
        return tuple(s_new)

    zero = jnp.zeros((dk, dk), F32)
    lax.fori_loop(0, nch, rec_body, (zero,) * len(chains))

    def fin_body(idx, carry):
        hh, c = idx // nch, lax.rem(idx, nch)
        rows = pl.ds(pl.multiple_of(c * ch, ch), ch)
        lanes = head_lanes(hh)
        o = o_scr[hh, 0, rows, :] + o_scr[hh, 1, rows, :]
        out_ref[0, rows, lanes] = _rms(o, nw_ref[...]) * _silu(z_ref[0, rows, lanes])
        return carry

    lax.fori_loop(0, nh * nch, fin_body, 0)


def _gdn_mixer(proj3, small3, conv_w, hp, norm_w):
    b, length, _ = proj3.shape
    dk = GDN_HEAD_DIM
    nch = length // GDN_CHUNK
    nh = GDN_STEP_HEADS
    groups = GDN_HEADS // nh
    q0 = (2 * SSD_INNER + 2 * SSD_GROUPS * SSD_STATE) // (nh * dk)
    seq = lambda off: pl.BlockSpec((1, length, nh * dk), lambda i, h: (i, 0, off + h))
    par = lambda off: pl.BlockSpec((CONV_K, nh * dk), lambda i, h: (0, off + h))
    return pl.pallas_call(
        _gdn_kernel,
        grid=(b, groups),
        in_specs=[
            seq(q0), seq(q0 + groups), seq(q0 + 2 * groups), seq(q0 + 3 * groups),
            pl.BlockSpec((1, length, LANES), lambda i, h: (i, 0, 0)),
            par(0), par(groups), par(2 * groups),
            pl.BlockSpec((SUBLANES, LANES), lambda i, h: (0, 0)),
            pl.BlockSpec((1, dk), lambda i, h: (0, 0)),
        ],
        out_specs=pl.BlockSpec((1, length, nh * dk), lambda i, h: (i, 0, h)),
        out_shape=jax.ShapeDtypeStruct((b, length, GDN_INNER), F32),
        scratch_shapes=[
            pltpu.VMEM((nh, length, dk), F32), pltpu.VMEM((nh, length, dk), F32), pltpu.VMEM((nh, length, dk), F32),
            pltpu.VMEM((nh, 2, length, dk), F32),
            pltpu.VMEM((nh, 2, 2 * length, dk), BF16),
            pltpu.VMEM((nh, 2, length, GDN_CHUNK), BF16),
            pltpu.VMEM((nh, 2, length, dk), BF16),
            pltpu.VMEM((nh, 2, nch * SUBLANES, LANES), F32),
            pltpu.VMEM((nh, 2, length, dk), F32),
        ],
        compiler_params=_cparams(("parallel", "arbitrary")),
        name="gdn_mixer",
    )(proj3, proj3, proj3, proj3, small3, conv_w, conv_w, conv_w, hp, norm_w)


def _merge_kernel(x_ref, ya_ref, yb_ref, ga_ref, gb_ref, wa_ref, wb_ref, wo_ref, gf_ref, h_ref, n_ref):
    ya = _dot(ya_ref[...].astype(BF16), wa_ref[...])
    yb = _dot(yb_ref[...].astype(BF16), wb_ref[...])
    merged = jax.nn.sigmoid(ga_ref[...]) * ya + jax.nn.sigmoid(gb_ref[...]) * yb
    h = x_ref[...] + _dot(merged.astype(BF16), wo_ref[...])
    h_ref[...] = h
    n_ref[...] = _rms(h, gf_ref[...]).astype(BF16)


def _merge(x2d, ya, yb, proj2d, wa, wb, wo, gf):
    t, d = x2d.shape
    tm = min(TM_MERGE, t)
    ga0 = (proj2d.shape[1] - 2 * d) // d
    tok = lambda: pl.BlockSpec((tm, d), lambda i: (i, 0))
    wsp = lambda: pl.BlockSpec((d, d), lambda i: (0, 0))
    return pl.pallas_call(
        _merge_kernel,
        grid=(t // tm,),
        in_specs=[tok(), tok(), tok(),
                  pl.BlockSpec((tm, d), lambda i: (i, ga0)), pl.BlockSpec((tm, d), lambda i: (i, ga0 + 1)),
                  wsp(), wsp(), wsp(), pl.BlockSpec((1, d), lambda i: (0, 0))],
        out_specs=[tok(), tok()],
        out_shape=[jax.ShapeDtypeStruct((t, d), F32), jax.ShapeDtypeStruct((t, d), BF16)],
        compiler_params=_cparams(("parallel",)),
        name="merge",
    )(x2d, ya, yb, proj2d, proj2d, wa, wb, wo, gf)


def _topk_store(problems, k):
    state = [p[0] for p in problems]
    iotas = [lax.broadcasted_iota(I32, s.shape, 0).astype(F32) for s in state]
    for i in range(k):
        ms = [jnp.max(s, axis=0, keepdims=True) for s in state]
        cands = [jnp.where(s == m, io, float(s.shape[0])) for s, m, io in zip(state, ms, iotas)]
        ixs = [jnp.min(c, axis=0, keepdims=True) for c in cands]
        hits = [c == ix for c, ix in zip(cands, ixs)]
        for (_, vals_ref, outs_ref, payload), m, ix, hit in zip(problems, ms, ixs, hits):
            vals_ref[i:i + 1, :] = m
            outs_ref[i:i + 1, :] = (ix if payload is None
                                    else jnp.max(jnp.where(hit, payload, -1.0), axis=0, keepdims=True))
        state = [jnp.where(hit, -jnp.inf, s) for s, hit in zip(state, hits)]


def _route_unit(tok_ref, wq_ref, keys_ref, e_scr, gt_scr, tv_scr, ti_scr, bv_scr, be_scr, head, tile):
    kk = PEER_TOPK
    half = kk // 2
    toks = tok_ref[pl.ds(pl.multiple_of(tile * LANES, LANES), LANES), :]
    qry = _dot(toks, wq_ref[...])
    st = [_dot_nt(keys_ref[0, z], qry[:, z * PEER_HALF:(z + 1) * PEER_HALF].astype(BF16)) for z in range(2)]
    _topk_store([(st[z], tv_scr.at[z], ti_scr.at[z], None) for z in range(2)], kk)
    s0, s1 = tv_scr[0], tv_scr[1]
    e0, i1 = ti_scr[0] * float(PEER_KEYS), ti_scr[1]
    quarter = half // 2
    upper = lax.broadcasted_iota(I32, (half, LANES), 0) >= quarter
    s1q, i1q = (jnp.concatenate([t[0:quarter], t[0:quarter]], axis=0) for t in (s1, i1))
    duo = lambda t, i: jnp.where(upper, t[i + 1:i + 2], t[i:i + 1])
    cs = ([s0[0:1] + s1] + [s0[i:i + 1] + s1[0:half] for i in range(1, quarter)]
          + [duo(s0, i) + s1q for i in range(quarter, half, 2)] + [s0[half:] + s1[0:1]])
    ce = ([e0[0:1] + i1] + [e0[i:i + 1] + i1[0:half] for i in range(1, quarter)]
          + [duo(e0, i) + i1q for i in range(quarter, half, 2)] + [e0[half:] + i1[0:1]])
    _topk_store([(jnp.concatenate(cs, axis=0), bv_scr, be_scr, jnp.concatenate(ce, axis=0))], kk)
    best = bv_scr[...]
    ex = jnp.exp(best - best[0:1])
    rows = pl.ds(pl.multiple_of(head * kk, kk), kk)
    lanes = pl.ds(pl.multiple_of(tile * LANES, LANES), LANES)
    e_scr[rows, lanes] = be_scr[...]
    gt_scr[rows, lanes] = ex / jnp.sum(ex, axis=0, keepdims=True)


def _peer_kernel(nm_ref, nr_ref, wq_ref, keys_ref, u_ref, v_ref, out_ref,
                 g_scr, e_scr, gt_scr, et_scr, gtt_scr, tv_scr, ti_scr, bv_scr, be_scr):
    tm = nm_ref.shape[0]
    nk = PEER_KEYS
    ntile = tm // LANES
    tile_i = pl.program_id(0)
    step = pl.program_id(1)

    @pl.when((tile_i == 0) & (step == 0))
    def _():
        e_scr[...] = jnp.zeros_like(e_scr)
        gt_scr[...] = jnp.zeros_like(gt_scr)

    @pl.when(step == 0)
    def _():
        out_ref[...] = jnp.zeros_like(out_ref)
        et_scr[...] = e_scr[...].T.astype(I32)
        gtt_scr[...] = gt_scr[...].T
        sub = lax.broadcasted_iota(I32, (nk, LANES), 0)

        zero = jnp.zeros((nk, LANES), BF16)

        def pair_body(tp, carry):
            t0 = 2 * tp
            ats, bts = [], []
            for o in range(2):
                re = et_scr[pl.ds(t0 + o, 1), :]
                rg = 0.5 * gtt_scr[pl.ds(t0 + o, 1), :]
                ats.append(jnp.where(sub == (re >> PEER_KEY_BITS), rg, 0.0).astype(BF16))
                bts.append(jnp.where(sub == (re & (nk - 1)), 1.0, 0.0).astype(BF16))
            bt = jnp.concatenate([jnp.concatenate([bts[0], zero], axis=1),
                                  jnp.concatenate([zero, bts[1]], axis=1)], axis=0)
            g2 = _dot_nt(jnp.concatenate(ats, axis=1), bt)
            lo = pltpu.bitcast(g2[:, :LANES].astype(BF16).astype(F32), jnp.uint32) >> 16
            hi = pltpu.bitcast(g2[:, LANES:].astype(BF16).astype(F32), jnp.uint32) & jnp.uint32(0xFFFF0000)
            g_scr[pl.ds(tp, nk, stride=G_PITCH), :] = lo | hi
            return carry

        lax.fori_loop(0, tm // 2, pair_body, 0, unroll=TOKEN_UNROLL // 2)

    per_step = PEER_HEADS * ntile * EXP_ROWS // nk
    for k in range(per_step):
        unit = step * per_step + k
        _route_unit(nr_ref, wq_ref, keys_ref, e_scr, gt_scr, tv_scr, ti_scr, bv_scr, be_scr,
                    unit // ntile, lax.rem(unit, ntile))

    hid = _dot_nt(nm_ref[...], u_ref[...])
    act = hid * (1.0 + lax.erf(hid * (0.5 ** 0.5)))
    gates = [pltpu.bitcast(g_scr[pl.ds(pl.multiple_of((step * EXP_ROWS + r) * G_PITCH, SUBLANES), tm // 2), :], BF16)
             for r in range(EXP_ROWS)]
    wts = act.astype(BF16) * jnp.concatenate(gates, axis=1)
    out_ref[...] = out_ref[...] + _dot(wts, v_ref[...])


def _peer(n2, wq, keys, eu, ev):
    t, d = n2.shape
    tm = min(TM_EXP, t)
    assert tm == TM_EXP, "gate slab pitch is derived from TM_EXP"
    assert PEER_KEYS == LANES == 1 << PEER_KEY_BITS, "one lane per sub-key; expert ids decoded by shift and mask"
    hk = PEER_HEADS * PEER_TOPK
    er = EXP_ROWS * PEER_KEYS
    nsteps = eu.shape[0] // er
    nt = t // tm
    units = PEER_HEADS * (tm // LANES)
    assert units % nsteps == 0 and (tm // LANES) % (units // nsteps) == 0, "whole routing units of one head per step"
    spr = nsteps // PEER_HEADS
    qw = 2 * PEER_HALF
    prev = lambda i, s: (jnp.maximum(i - 1, 0), 0)
    return pl.pallas_call(
        _peer_kernel,
        grid=(nt + 1, nsteps),
        in_specs=[pl.BlockSpec((tm, d), prev),
                  pl.BlockSpec((tm, d), lambda i, s: (jnp.minimum(i, nt - 1), 0)),
                  pl.BlockSpec((d, qw), lambda i, s: (0, s // spr)),
                  pl.BlockSpec((1, 2, PEER_KEYS, PEER_HALF), lambda i, s: (s // spr, 0, 0, 0)),
                  pl.BlockSpec((er, d), lambda i, s: (s, 0)),
                  pl.BlockSpec((er, d), lambda i, s: (s, 0))],
        out_specs=pl.BlockSpec((tm, d), prev),
        out_shape=jax.ShapeDtypeStruct((t, d), F32),
        scratch_shapes=[pltpu.VMEM((PEER_KEYS * G_PITCH, LANES), jnp.uint32),
                        pltpu.VMEM((hk, tm), F32), pltpu.VMEM((hk, tm), F32),
                        pltpu.VMEM((tm, hk), I32), pltpu.VMEM((tm, hk), F32),
                        pltpu.VMEM((2, PEER_TOPK, LANES), F32), pltpu.VMEM((2, PEER_TOPK, LANES), F32),
                        pltpu.VMEM((PEER_TOPK, LANES), F32), pltpu.VMEM((PEER_TOPK, LANES), F32)],
        compiler_params=_cparams(("arbitrary", "arbitrary")),
        name="peer_ffn",
    )(n2, n2, wq, keys, eu, ev)


def _ple_kernel(h_ref, f_ref, p_ref, gp_ref, wg_ref, wp_ref, gf_ref, y_ref):
    h = h_ref[...] + f_ref[...]
    gate = jax.nn.sigmoid(_dot(_rms(h, gp_ref[...]).astype(BF16), wg_ref[...]))
    h = h + gate * _dot(p_ref[...].astype(BF16), wp_ref[...])
    y_ref[...] = _rms(h, gf_ref[...])


def _ple(h1, ffn, p2d, gp, wg, wp, gfin):
    t, d = h1.shape
    pd = p2d.shape[1]
    tm = min(TM_PLE, t)
    vec = lambda: pl.BlockSpec((1, d), lambda i: (0, 0))
    tok = lambda: pl.BlockSpec((tm, d), lambda i: (i, 0))
    return pl.pallas_call(
        _ple_kernel,
        grid=(t // tm,),
        in_specs=[tok(), tok(), pl.BlockSpec((tm, pd), lambda i: (i, 0)),
                  vec(), pl.BlockSpec((d, d), lambda i: (0, 0)), pl.BlockSpec((pd, d), lambda i: (0, 0)), vec()],
        out_specs=tok(),
        out_shape=jax.ShapeDtypeStruct((t, d), F32),
        compiler_params=_cparams(("parallel",)),
        name="ple_final",
    )(h1, ffn, p2d, gp, wg, wp, gfin)


def _pad_lanes(v):
    return jnp.pad(v, (0, LANES - v.shape[0]))


def _prepare(norm_mix, w_in, ssd_conv_w, ssd_conv_b, ssd_a_log, ssd_dt_bias, ssd_d, ssd_norm, w_ssd_out,
             gdn_conv_w, gdn_a_log, gdn_dt_bias, gdn_norm, w_gdn_out, w_out, norm_ffn, peer_query, peer_keys,
             expert_u, expert_v, norm_ple, w_ple_gate, w_ple_proj, norm_final):
    w = w_in[0]
    xbc = SSD_INNER + 2 * SSD_GROUPS * SSD_STATE
    c_dt = SSD_INNER + xbc
    c_qkv = c_dt + 2 * SSD_HEADS
    c_a = c_qkv + 4 * GDN_INNER
    c_ga = c_a + 4 * GDN_HEADS
    w_main = jnp.concatenate([w[:, :c_dt], w[:, c_qkv:c_a], w[:, c_ga:]], axis=1).astype(BF16)
    w_small = jnp.concatenate([w[:, c_dt:c_qkv], w[:, c_a:c_ga]], axis=1)
    w_small = jnp.pad(w_small, ((0, 0), (0, LANES - w_small.shape[1]))).astype(BF16)
    hp = jnp.stack([
        _pad_lanes(jnp.concatenate([ssd_dt_bias[0].reshape(-1), gdn_dt_bias[0].reshape(-1)])),
        _pad_lanes(jnp.concatenate([ssd_a_log[0].reshape(-1), gdn_a_log[0].reshape(-1)])),
        _pad_lanes(ssd_d[0]),
    ])
    hp = jnp.pad(hp, ((0, SUBLANES - hp.shape[0]), (0, 0)))
    row = lambda v: v.reshape(1, -1)
    return dict(
        g_mix=row(norm_mix[0]), w_main=w_main, w_small=w_small, hp=hp,
        ssd_conv_w=ssd_conv_w[0], ssd_conv_b=row(ssd_conv_b[0]), ssd_norm=row(ssd_norm[0]),
        gdn_conv_w=gdn_conv_w[0], gdn_norm=row(gdn_norm[0]),
        wa=w_ssd_out[0].astype(BF16), wb=w_gdn_out[0].astype(BF16), wo=w_out[0].astype(BF16),
        g_ffn=row(norm_ffn[0]), wq=peer_query[0].astype(BF16),
        keys=peer_keys[0].astype(BF16),
        eu=expert_u[0].astype(BF16), ev=expert_v[0].astype(BF16),
        g_ple=row(norm_ple[0]), wg=w_ple_gate[0].astype(BF16), wp=w_ple_proj[0].astype(BF16),
        g_fin=row(norm_final),
    )


def _trunk(x, ple, p):
    b, length, d = x.shape
    t = b * length
    x2d = x.reshape(t, d)
    proj, small = _in_proj(x2d, p["g_mix"], p["w_main"], p["w_small"])
    proj3 = proj.reshape(b, length, -1)
    small3 = small.reshape(b, length, LANES)
    ya = _ssd_mixer(proj3, small3, p["ssd_conv_w"], p["ssd_conv_b"], p["hp"], p["ssd_norm"])
    yb = _gdn_mixer(proj3, small3, p["gdn_conv_w"], p["hp"], p["gdn_norm"])
    h1, n2 = _merge(x2d, ya.reshape(t, -1), yb.reshape(t, -1), proj, p["wa"], p["wb"], p["wo"], p["g_ffn"])
    ffn = _peer(n2, p["wq"], p["keys"], p["eu"], p["ev"])
    y = _ple(h1, ffn, ple.reshape(t, -1), p["g_ple"], p["wg"], p["wp"], p["g_fin"])
    return y.reshape(b, length, d)


def kernel(x_prompt, x_sample, p_prompt, p_sample, norm_mix, w_in, ssd_conv_w, ssd_conv_b, ssd_a_log, ssd_dt_bias, ssd_d, ssd_norm, w_ssd_out, gdn_conv_w, gdn_a_log, gdn_dt_bias, gdn_norm, w_gdn_out, w_out, norm_ffn, peer_query, peer_keys, expert_u, expert_v, norm_ple, w_ple_gate, w_ple_proj, norm_final):
    assert w_in.shape[0] == 1, "single-layer trunk"
    p = _prepare(norm_mix, w_in, ssd_conv_w, ssd_conv_b, ssd_a_log, ssd_dt_bias, ssd_d, ssd_norm, w_ssd_out,
                 gdn_conv_w, gdn_a_log, gdn_dt_bias, gdn_norm, w_gdn_out, w_out, norm_ffn, peer_query, peer_keys,
                 expert_u, expert_v, norm_ple, w_ple_gate, w_ple_proj, norm_final)
    return (_trunk(x_prompt, p_prompt[0], p), _trunk(x_sample, p_sample[0], p))
```

```python
import jax
import jax.numpy as jnp
from jax import lax
from jax.experimental import pallas as pl
from jax.experimental.pallas import tpu as pltpu

F32 = jnp.float32
BF16 = jnp.bfloat16
I32 = jnp.int32

EPS = 1e-6
CONV_K = 4
SSD_HEADS = 16
SSD_HEAD_DIM = 64
SSD_GROUPS = 4
SSD_STATE = 128
SSD_INNER = SSD_HEADS * SSD_HEAD_DIM
GDN_HEADS = 8
GDN_HEAD_DIM = 128
GDN_INNER = GDN_HEADS * GDN_HEAD_DIM
PEER_HEADS = 8
PEER_KEYS = 128
PEER_TOPK = 16
PEER_HALF = 128
PEER_KEY_BITS = PEER_KEYS.bit_length() - 1

LANES = 128
SUBLANES = 8
SSD_CHUNK = 128
SSD_STEP_GROUPS = 2
GDN_CHUNK = 128
GDN_PREP_CHUNKS = 8
GDN_STEP_HEADS = 2
GDN_INV_BLOCK = 8
VMEM_LIMIT = 56 * 1024 * 1024

TM_PROJ = 2048
TN_PROJ = 1024
TM_MERGE = 512
TM_EXP = 512
EXP_ROWS = 16
G_PITCH = TM_EXP // 2 + 8
TOKEN_UNROLL = 64
TM_PLE = 1024

NEG = -1e30


def _dot(a, b):
    return jnp.dot(a, b, preferred_element_type=F32)


def _dot_nt(a, b):
    return lax.dot_general(a, b, (((1,), (1,)), ((), ())), preferred_element_type=F32)


def _dot_tn(a, b):
    return lax.dot_general(a, b, (((0,), (0,)), ((), ())), preferred_element_type=F32)


def _bf16_terms(m, terms):
    out = []
    for _ in range(terms - 1):
        t = m.astype(BF16)
        out.append(t)
        m = m - t.astype(F32)
    out.append(m.astype(BF16))
    return out


def _dot_exact_lhs(lb, m, terms=3):
    n = m.shape[1]
    r = _dot(lb, jnp.concatenate(_bf16_terms(m, terms), axis=1))
    return sum(r[:, i * n:(i + 1) * n] for i in range(1, terms)) + r[:, :n]


def _dot_exact_rhs(m, eb, terms=2):
    return _dot(jnp.concatenate(_bf16_terms(m, terms), axis=1), jnp.concatenate([eb] * terms, axis=0))


def _rows_transposed(sel, m, terms=3):
    n = m.shape[0]
    r = _dot_nt(sel, jnp.concatenate(_bf16_terms(m, terms), axis=0))
    return sum(r[:, i * n:(i + 1) * n] for i in range(1, terms)) + r[:, :n]


def _softplus(x):
    return jnp.maximum(x, 0.0) + jnp.log1p(jnp.exp(-jnp.abs(x)))


def _silu(x):
    return x * jax.nn.sigmoid(x)


def _rms(x, g):
    return x * lax.rsqrt(jnp.mean(x * x, axis=-1, keepdims=True) + EPS) * g


def _cparams(sem):
    return pltpu.CompilerParams(dimension_semantics=sem, vmem_limit_bytes=VMEM_LIMIT)


def _in_proj_kernel(x_ref, g_ref, wm_ref, ws_ref, main_ref, small_ref, n_scr):
    @pl.when(pl.program_id(1) == 0)
    def _():
        n = _rms(x_ref[...], g_ref[...]).astype(BF16)
        n_scr[...] = n
        small_ref[...] = _dot(n, ws_ref[...])

    main_ref[...] = _dot(n_scr[...], wm_ref[...])


def _in_proj(x2d, g, w_main, w_small):
    t, d = x2d.shape
    n_main = w_main.shape[1]
    tm = min(TM_PROJ, t)
    return pl.pallas_call(
        _in_proj_kernel,
        grid=(t // tm, n_main // TN_PROJ),
        in_specs=[
            pl.BlockSpec((tm, d), lambda i, j: (i, 0)),
            pl.BlockSpec((1, d), lambda i, j: (0, 0)),
            pl.BlockSpec((d, TN_PROJ), lambda i, j: (0, j)),
            pl.BlockSpec((d, LANES), lambda i, j: (0, 0)),
        ],
        out_specs=[
            pl.BlockSpec((tm, TN_PROJ), lambda i, j: (i, j)),
            pl.BlockSpec((tm, LANES), lambda i, j: (i, 0)),
        ],
        out_shape=[jax.ShapeDtypeStruct((t, n_main), F32), jax.ShapeDtypeStruct((t, LANES), F32)],
        scratch_shapes=[pltpu.VMEM((tm, d), BF16)],
        compiler_params=_cparams(("parallel", "arbitrary")),
        name="in_proj",
    )(x2d, g, w_main, w_small)


def _conv_silu_chunk(src_ref, w_ref, b_ref, c, nch, rows, lanes=slice(None)):
    length = src_ref.shape[1]
    base = pl.multiple_of(c * rows, rows)
    main = src_ref[0, pl.ds(base, rows), lanes]
    prev = src_ref[0, pl.ds(pl.multiple_of(jnp.maximum(base - SUBLANES, 0), SUBLANES), SUBLANES), lanes]
    nxt = src_ref[0, pl.ds(pl.multiple_of(jnp.minimum(base + rows, length - SUBLANES), SUBLANES), SUBLANES), lanes]
    prev = jnp.where(c > 0, prev, 0.0)
    nxt = jnp.where(c < nch - 1, nxt, 0.0)
    ext = jnp.concatenate([prev, main, nxt], axis=0)
    w = w_ref[:, lanes]
    lo = SUBLANES - CONV_K // 2
    y = ext[lo:lo + rows] * w[0:1]
    for j in range(1, CONV_K):
        y = y + ext[lo + j:lo + j + rows] * w[j:j + 1]
    if b_ref is not None:
        y = y + b_ref[:, lanes]
    return _silu(y)


def _expand_heads(t, lane0, width):
    r = t.shape[0]
    hid = lax.broadcasted_iota(I32, (r, 4 * width), 1) // width
    out = jnp.broadcast_to(t[:, lane0 + 3:lane0 + 4], (r, 4 * width))
    for j in (2, 1, 0):
        out = jnp.where(hid == j, t[:, lane0 + j:lane0 + j + 1], out)
    return out


def _ssd_kernel(z_ref, xs_ref, bm_ref, cm_ref, sm_ref, cwx_ref, cwb_ref, cwc_ref, cbx_ref, cbb_ref, cbc_ref,
                hp_ref, nw_ref, out_ref, xc_scr, bc_scr, cc_scr, y_scr, st_scr):
    ch = SSD_CHUNK
    length = z_ref.shape[1]
    nch = length // ch
    gwid = 4 * SSD_HEAD_DIM
    ng = SSD_STEP_GROUPS
    gsteps = range(ng)
    shifts = [lax.rem(LANES - 4 * (pl.program_id(1) * ng + gg), LANES) for gg in gsteps]
    hps = [pltpu.roll(hp_ref[...], s, 1) for s in shifts]
    bias_rows = [hp[0:1, :] for hp in hps]
    a_rows = [-jnp.exp(hp[1:2, :]) for hp in hps]
    xlanes = [slice(gg * gwid, (gg + 1) * gwid) for gg in gsteps]
    slanes = [slice(gg * SSD_STATE, (gg + 1) * SSD_STATE) for gg in gsteps]

    row = lax.broadcasted_iota(I32, (ch, ch), 0)
    col = lax.broadcasted_iota(I32, (ch, ch), 1)
    hid = lax.broadcasted_iota(I32, (ch, gwid), 1) // SSD_HEAD_DIM

    def conv_body(c, carry):
        base = pl.multiple_of(c * ch, ch)
        for gg in gsteps:
            xc_scr[gg, pl.ds(base, ch), :] = _conv_silu_chunk(xs_ref, cwx_ref, cbx_ref, c, nch, ch, xlanes[gg])
            bc_scr[gg, pl.ds(base, ch), :] = _conv_silu_chunk(bm_ref, cwb_ref, cbb_ref, c, nch, ch, slanes[gg])
            cc_scr[gg, pl.ds(base, ch), :] = _conv_silu_chunk(cm_ref, cwc_ref, cbc_ref, c, nch, ch, slanes[gg])
        return carry

    lax.fori_loop(0, nch, conv_body, 0)

    masks = ((row >= col), (row <= col))
    lmats = tuple(jnp.where(m, 1.0, 0.0).astype(BF16) for m in masks)
    lanes0 = (0, SSD_HEADS)
    tot_rows = (ch - 1, 0)
    erow = lax.broadcasted_iota(I32, (LANES, gwid), 0)
    ecol = lax.broadcasted_iota(I32, (LANES, gwid), 1) // SSD_HEAD_DIM
    emats = tuple(jnp.where(erow == ecol + l0, 1.0, 0.0).astype(BF16) for l0 in lanes0)
    srow = lax.broadcasted_iota(I32, (SUBLANES, LANES), 0)
    scol = lax.broadcasted_iota(I32, (SUBLANES, LANES), 1)
    sels = tuple(jnp.where(scol == srow + l0, 1.0, 0.0).astype(BF16) for l0 in lanes0)
    chains = [(gg, d) for gg in gsteps for d in (0, 1)]

    st_scr[...] = jnp.zeros_like(st_scr)

    def scan_body(ci, carry):
        bases = (pl.multiple_of(ci * ch, ch), pl.multiple_of((nch - 1 - ci) * ch, ch))
        x = [xc_scr[gg, pl.ds(bases[d], ch), :] for gg, d in chains]
        bb = [bc_scr[gg, pl.ds(bases[d], ch), :].astype(BF16) for gg, d in chains]
        cb = [cc_scr[gg, pl.ds(bases[d], ch), :].astype(BF16) for gg, d in chains]
        dt_all = [_softplus(pltpu.roll(sm_ref[0, pl.ds(bases[d], ch), :], shifts[gg], 1) + bias_rows[gg])
                  for gg, d in chains]
        acum = [_dot_exact_lhs(lmats[d], dt * a_rows[gg]) for (gg, d), dt in zip(chains, dt_all)]
        sc = [jnp.where(masks[d], _dot_nt(c_, b_), 0.0) for (gg, d), c_, b_ in zip(chains, cb, bb)]
        st = [st_scr[gg, d] for gg, d in chains]
        y_off = [_dot(c_, s_.astype(BF16)) for c_, s_ in zip(cb, st)]
        dt_x = [_dot_exact_rhs(dt, emats[d]) for (gg, d), dt in zip(chains, dt_all)]
        acum_t = [_rows_transposed(sels[d], a_) for (gg, d), a_ in zip(chains, acum)]
        tot = [a_[tot_rows[d]:tot_rows[d] + 1, :] for (gg, d), a_ in zip(chains, acum)]
        decay_x = [_dot_exact_rhs(jnp.concatenate([jnp.exp(a_), jnp.exp(t_ - a_)], axis=0), emats[d])
                   for (gg, d), a_, t_ in zip(chains, acum, tot)]
        xd = [x_ * d_ for x_, d_ in zip(x, dt_x)]
        lhs, rhs = [], []
        for k, (gg, d) in enumerate(chains):
            mixes, parts = [], []
            for j in range(4):
                ln = lanes0[d] + j
                dd = acum[k][:, ln:ln + 1] - acum_t[k][j:j + 1, :]
                mixes.append((sc[k] * jnp.exp(jnp.where(masks[d], dd, NEG))).astype(BF16))
                parts.append(jnp.where(hid == j, xd[k], 0.0).astype(BF16))
            lhs.append(jnp.concatenate(mixes, axis=1))
            rhs.append(jnp.concatenate(parts, axis=0))
        y = [_dot(l_, r_) for l_, r_ in zip(lhs, rhs)]
        upd = [_dot_tn(b_, (x_ * d_[ch:]).astype(BF16)) for b_, x_, d_ in zip(bb, xd, decay_x)]
        for k, (gg, d) in enumerate(chains):
            ea_x = decay_x[k][:ch]
            etot_x = ea_x[tot_rows[d]:tot_rows[d] + 1, :]
            y_scr[gg, d, pl.ds(bases[d], ch), :] = y[k] + y_off[k] * ea_x
            st_scr[gg, d] = st[k] * etot_x + upd[k]
        return carry

    lax.fori_loop(0, nch, scan_body, 0)

    dsk_x = [_expand_heads(hp[2:3, :], 0, SSD_HEAD_DIM) for hp in hps]

    def fin_body(c, carry):
        rows = pl.ds(pl.multiple_of(c * ch, ch), ch)
        for gg in gsteps:
            y = y_scr[gg, 0, rows, :] + y_scr[gg, 1, rows, :] + xc_scr[gg, rows, :] * dsk_x[gg]
            y = y * _silu(z_ref[0, rows, xlanes[gg]])
            out_ref[0, rows, xlanes[gg]] = _rms(y, nw_ref[:, xlanes[gg]])
        return carry

    lax.fori_loop(0, nch, fin_body, 0)


def _ssd_mixer(proj3, small3, conv_w, conv_b, hp, norm_w):
    b, length, _ = proj3.shape
    ng = SSD_STEP_GROUPS
    steps = SSD_GROUPS // ng
    gw = ng * 4 * SSD_HEAD_DIM
    sw = ng * SSD_STATE
    xs0 = SSD_INNER // gw
    b0 = 2 * SSD_INNER // sw
    c0 = b0 + steps
    nb_x = SSD_INNER // sw
    seq = lambda w, off: pl.BlockSpec((1, length, w), lambda i, g: (i, 0, off + g))
    par = lambda r, w, off: pl.BlockSpec((r, w), lambda i, g: (0, off + g))
    return pl.pallas_call(
        _ssd_kernel,
        grid=(b, steps),
        in_specs=[
            seq(gw, 0), seq(gw, xs0), seq(sw, b0), seq(sw, c0),
            pl.BlockSpec((1, length, LANES), lambda i, g: (i, 0, 0)),
            par(CONV_K, gw, 0), par(CONV_K, sw, nb_x), par(CONV_K, sw, nb_x + steps),
            par(1, gw, 0), par(1, sw, nb_x), par(1, sw, nb_x + steps),
            pl.BlockSpec((SUBLANES, LANES), lambda i, g: (0, 0)),
            par(1, gw, 0),
        ],
        out_specs=pl.BlockSpec((1, length, gw), lambda i, g: (i, 0, g)),
        out_shape=jax.ShapeDtypeStruct((b, length, SSD_INNER), F32),
        scratch_shapes=[
            pltpu.VMEM((ng, length, 4 * SSD_HEAD_DIM), F32),
            pltpu.VMEM((ng, length, SSD_STATE), F32), pltpu.VMEM((ng, length, SSD_STATE), F32),
            pltpu.VMEM((ng, 2, length, 4 * SSD_HEAD_DIM), F32),
            pltpu.VMEM((ng, 2, SSD_STATE, 4 * SSD_HEAD_DIM), F32),
        ],
        compiler_params=_cparams(("parallel", "arbitrary")),
        name="ssd_mixer",
    )(proj3, proj3, proj3, proj3, small3, conv_w, conv_w, conv_w, conv_b, conv_b, conv_b, hp, norm_w)


def _gdn_kernel(q_ref, k_ref, v_ref, z_ref, sm_ref, cwq_ref, cwk_ref, cwv_ref, hp_ref, nw_ref, out_ref,
                qn_scr, kn_scr, vn_scr, nc_scr, wq_scr, rc_scr, et_scr, o_scr):
    ch = GDN_CHUNK
    length = z_ref.shape[1]
    nch = length // ch
    dk = GDN_HEAD_DIM
    nh = GDN_STEP_HEADS
    lane_a = 2 * SSD_HEADS
    lane_beta = lane_a + 2 * GDN_HEADS

    def head_shift(hh):
        return lax.rem(LANES - (pl.program_id(1) * nh + hh), LANES)

    def head_lanes(hh):
        return pl.ds(pl.multiple_of(hh * dk, dk), dk)

    row = lax.broadcasted_iota(I32, (ch, ch), 0)
    col = lax.broadcasted_iota(I32, (ch, ch), 1)
    eye = jnp.where(row == col, 1.0, 0.0)
    n_double = (GDN_INV_BLOCK - 1).bit_length() - 1
    levels = []
    size = GDN_INV_BLOCK
    inside = (row // size) == (col // size)
    levels.append(inside)
    while size < ch:
        size *= 2
        merged = (row // size) == (col // size)
        levels.append(merged & jnp.logical_not(inside))
        inside = merged

    def conv_body(idx, carry):
        hh, c = idx // nch, lax.rem(idx, nch)
        base = pl.multiple_of(c * ch, ch)
        lanes = head_lanes(hh)
        q = _conv_silu_chunk(q_ref, cwq_ref, None, c, nch, ch, lanes)
        k = _conv_silu_chunk(k_ref, cwk_ref, None, c, nch, ch, lanes)
        qn_scr[hh, pl.ds(base, ch), :] = q * (lax.rsqrt(jnp.sum(q * q, axis=-1, keepdims=True) + EPS) * dk ** -0.5)
        kn_scr[hh, pl.ds(base, ch), :] = k * lax.rsqrt(jnp.sum(k * k, axis=-1, keepdims=True) + EPS)
        vn_scr[hh, pl.ds(base, ch), :] = _conv_silu_chunk(v_ref, cwv_ref, None, c, nch, ch, lanes)
        return carry

    lax.fori_loop(0, nh * nch, conv_body, 0)

    incl = ((row >= col), (row <= col))
    strict = ((row > col), (row < col))
    lmat = tuple(jnp.where(m, 1.0, 0.0).astype(BF16) for m in incl)
    umat = tuple(jnp.where(m, 1.0, 0.0) for m in strict)
    jref = (0, ch - 1)
    tot_row = (ch - 1, 0)
    group = min(GDN_PREP_CHUNKS, nch)

    def prep_body(idx, carry):
        hh, gi = idx // (nch // group), lax.rem(idx, nch // group)
        shift = head_shift(hh)
        hp = pltpu.roll(hp_ref[...], shift, 1)
        bias_row = hp[0:1, :]
        acoef_row = jnp.exp(hp[1:2, :])
        chains = []
        for ci in range(group):
            c = gi * group + ci
            base = pl.multiple_of(c * ch, ch)
            q = qn_scr[hh, pl.ds(base, ch), :]
            k = kn_scr[hh, pl.ds(base, ch), :]
            v = vn_scr[hh, pl.ds(base, ch), :]
            sm = pltpu.roll(sm_ref[0, pl.ds(base, ch), :], shift, 1)
            g_all = -acoef_row * _softplus(sm + bias_row)
            beta_all = jax.nn.sigmoid(sm)
            kbf = k.astype(BF16)
            kq = _dot_nt(jnp.concatenate([kbf, q.astype(BF16)], axis=0), kbf)
            for di in (0, 1):
                ln = lane_a + di * GDN_HEADS
                lb = lane_beta + di * GDN_HEADS
                chains.append(dict(c=c, di=di, q=q, k=k, v=v, kq=kq, g=g_all[:, ln:ln + 1],
                                   beta=beta_all[:, lb:lb + 1]))
        for t in chains:
            t["dmat"] = _dot_exact_lhs(lmat[t["di"]], t["g"] * umat[t["di"]], terms=2)
        for t in chains:
            di = t["di"]
            t["kb"] = t["k"] * t["beta"]
            dec = t["dmat"][:, jref[di]:jref[di] + 1] + t["g"][jref[di]:jref[di] + 1, :]
            t["dec"] = dec
            t["tot"] = dec[tot_row[di]:tot_row[di] + 1, :]
            gam = jnp.exp(jnp.where(incl[di], t["dmat"], NEG))
            t["qk"] = (t["kq"][ch:] * gam).astype(BF16)
            a = jnp.where(strict[di], t["kq"][:ch] * (gam * t["beta"]), 0.0)
            ad = jnp.where(levels[0], a, 0.0)
            t["p"] = eye - ad
            t["ab"] = ad.astype(BF16)
            t["off"] = [jnp.where(m, a, 0.0).astype(BF16) for m in levels[1:]]
        for t in chains:
            t["x"] = _dot(t["ab"], t["ab"])
        for i in range(n_double):
            last = i == n_double - 1
            for t in chains:
                xb = t["x"].astype(BF16)
                pb = t["p"].astype(BF16)
                t["r"] = _dot(xb, pb) if last else _dot(xb, jnp.concatenate([pb, xb], axis=1))
            for t in chains:
                t["p"] = t["p"] + t["r"][:, :ch]
                if not last:
                    t["x"] = t["r"][:, ch:]
        for lv in range(len(levels) - 1):
            for t in chains:
                t["m"] = _dot(t["off"][lv], t["p"].astype(BF16))
            for t in chains:
                t["p"] = t["p"] - _dot(t["p"].astype(BF16), t["m"].astype(BF16))
        for t in chains:
            t["edec"] = jnp.exp(t["dec"])
            rhs = jnp.concatenate([t["v"] * t["beta"], t["kb"] * t["edec"]], axis=1).astype(BF16)
            t["uw"] = _dot(t["p"].astype(BF16), rhs)
        for t in chains:
            uwb = t["uw"].astype(BF16)
            kd = (t["k"] * jnp.exp(t["tot"] - t["dec"])).astype(BF16)
            t["kuw"] = _dot_tn(kd, uwb)
            t["quw"] = _dot(t["qk"], uwb)
        for t in chains:
            c, di = t["c"], t["di"]
            base = pl.multiple_of(c * ch, ch)
            base2 = pl.multiple_of(c * 2 * ch, 2 * ch)
            nc_scr[hh, di, pl.ds(base, dk), :] = t["kuw"][:, :dk]
            rc_scr[hh, di, pl.ds(base, ch), :] = t["quw"][:, :dk]
            wq_scr[hh, di, pl.ds(base2, 2 * ch), :] = jnp.concatenate(
                [t["kuw"][:, dk:], t["q"] * t["edec"] - t["quw"][:, dk:]], axis=0).astype(BF16)
            et_scr[hh, di, pl.ds(pl.multiple_of(c * SUBLANES, SUBLANES), SUBLANES), :] = jnp.broadcast_to(
                jnp.exp(t["tot"]), (SUBLANES, LANES))
        return carry

    lax.fori_loop(0, nh * (nch // group), prep_body, 0)

    chains = [(hh, di) for hh in range(nh) for di in (0, 1)]

    def rec_body(ci, carry):
        cs = (ci, nch - 1 - ci)
        bases = [pl.multiple_of(c * ch, ch) for c in cs]
        ws = [_dot(wq_scr[hh, di, pl.ds(pl.multiple_of(cs[di] * 2 * ch, 2 * ch), 2 * ch), :], s.astype(BF16))
              for (hh, di), s in zip(chains, carry)]
        s_new = []
        for j, (hh, di) in enumerate(chains):
            et = et_scr[hh, di, pl.ds(pl.multiple_of(cs[di] * SUBLANES, SUBLANES), 1), :]
            s_new.append(carry[j] * et - ws[j][:dk] + nc_scr[hh, di, pl.ds(bases[di], dk), :])
            o_scr[hh, di, pl.ds(bases[di], ch), :] = ws[j][ch:] + rc_scr[hh, di, pl.ds(bases[di], ch), :]
        return tuple(s_new)

    zero = jnp.zeros((dk, dk), F32)
    lax.fori_loop(0, nch, rec_body, (zero,) * len(chains))

    def fin_body(idx, carry):
        hh, c = idx // nch, lax.rem(idx, nch)
        rows = pl.ds(pl.multiple_of(c * ch, ch), ch)
        lanes = head_lanes(hh)
        o = o_scr[hh, 0, rows, :] + o_scr[hh, 1, rows, :]
        out_ref[0, rows, lanes] = _rms(o, nw_ref[...]) * _silu(z_ref[0, rows, lanes])
        return carry

    lax.fori_loop(0, nh * nch, fin_body, 0)


def _gdn_mixer(proj3, small3, conv_w, hp, norm_w):
    b, length, _ = proj3.shape
    dk = GDN_HEAD_DIM
    nch = length // GDN_CHUNK
    nh = GDN_STEP_HEADS
    groups = GDN_HEADS // nh
    q0 = (2 * SSD_INNER + 2 * SSD_GROUPS * SSD_STATE) // (nh * dk)
    seq = lambda off: pl.BlockSpec((1, length, nh * dk), lambda i, h: (i, 0, off + h))
    par = lambda off: pl.BlockSpec((CONV_K, nh * dk), lambda i, h: (0, off + h))
    return pl.pallas_call(
        _gdn_kernel,
        grid=(b, groups),
        in_specs=[
            seq(q0), seq(q0 + groups), seq(q0 + 2 * groups), seq(q0 + 3 * groups),
            pl.BlockSpec((1, length, LANES), lambda i, h: (i, 0, 0)),
            par(0), par(groups), par(2 * groups),
            pl.BlockSpec((SUBLANES, LANES), lambda i, h: (0, 0)),
            pl.BlockSpec((1, dk), lambda i, h: (0, 0)),
        ],
        out_specs=pl.BlockSpec((1, length, nh * dk), lambda i, h: (i, 0, h)),
        out_shape=jax.ShapeDtypeStruct((b, length, GDN_INNER), F32),
        scratch_shapes=[
            pltpu.VMEM((nh, length, dk), F32), pltpu.VMEM((nh, length, dk), F32), pltpu.VMEM((nh, length, dk), F32),
            pltpu.VMEM((nh, 2, nch * dk, dk), F32),
            pltpu.VMEM((nh, 2, 2 * length, dk), BF16),
            pltpu.VMEM((nh, 2, length, dk), F32),
            pltpu.VMEM((nh, 2, nch * SUBLANES, LANES), F32),
            pltpu.VMEM((nh, 2, length, dk), F32),
        ],
        compiler_params=_cparams(("parallel", "arbitrary")),
        name="gdn_mixer",
    )(proj3, proj3, proj3, proj3, small3, conv_w, conv_w, conv_w, hp, norm_w)


def _merge_kernel(x_ref, ya_ref, yb_ref, ga_ref, gb_ref, wa_ref, wb_ref, wo_ref, gf_ref, h_ref, n_ref):
    ya = _dot(ya_ref[...].astype(BF16), wa_ref[...])
    yb = _dot(yb_ref[...].astype(BF16), wb_ref[...])
    merged = jax.nn.sigmoid(ga_ref[...]) * ya + jax.nn.sigmoid(gb_ref[...]) * yb
    h = x_ref[...] + _dot(merged.astype(BF16), wo_ref[...])
    h_ref[...] = h
    n_ref[...] = _rms(h, gf_ref[...]).astype(BF16)


def _merge(x2d, ya, yb, proj2d, wa, wb, wo, gf):
    t, d = x2d.shape
    tm = min(TM_MERGE, t)
    ga0 = (proj2d.shape[1] - 2 * d) // d
    tok = lambda: pl.BlockSpec((tm, d), lambda i: (i, 0))
    wsp = lambda: pl.BlockSpec((d, d), lambda i: (0, 0))
    return pl.pallas_call(
        _merge_kernel,
        grid=(t // tm,),
        in_specs=[tok(), tok(), tok(),
                  pl.BlockSpec((tm, d), lambda i: (i, ga0)), pl.BlockSpec((tm, d), lambda i: (i, ga0 + 1)),
                  wsp(), wsp(), wsp(), pl.BlockSpec((1, d), lambda i: (0, 0))],
        out_specs=[tok(), tok()],
        out_shape=[jax.ShapeDtypeStruct((t, d), F32), jax.ShapeDtypeStruct((t, d), BF16)],
        compiler_params=_cparams(("parallel",)),
        name="merge",
    )(x2d, ya, yb, proj2d, proj2d, wa, wb, wo, gf)


def _topk_store(problems, k):
    state = [p[0] for p in problems]
    iotas = [lax.broadcasted_iota(I32, s.shape, 0).astype(F32) for s in state]
    for i in range(k):
        ms = [jnp.max(s, axis=0, keepdims=True) for s in state]
        cands = [jnp.where(s == m, io, float(s.shape[0])) for s, m, io in zip(state, ms, iotas)]
        ixs = [jnp.min(c, axis=0, keepdims=True) for c in cands]
        hits = [c == ix for c, ix in zip(cands, ixs)]
        for (_, vals_ref, outs_ref, payload), m, ix, hit in zip(problems, ms, ixs, hits):
            vals_ref[i:i + 1, :] = m
            outs_ref[i:i + 1, :] = (ix if payload is None
                                    else jnp.max(jnp.where(hit, payload, -1.0), axis=0, keepdims=True))
        state = [jnp.where(hit, -jnp.inf, s) for s, hit in zip(state, hits)]


def _route_unit(tok_ref, wq_ref, keys_ref, e_scr, gt_scr, tv_scr, ti_scr, bv_scr, be_scr, head, tile):
    kk = PEER_TOPK
    half = kk // 2
    toks = tok_ref[pl.ds(pl.multiple_of(tile * LANES, LANES), LANES), :]
    qry = _dot(toks, wq_ref[...])
    st = [_dot_nt(keys_ref[0, z], qry[:, z * PEER_HALF:(z + 1) * PEER_HALF].astype(BF16)) for z in range(2)]
    _topk_store([(st[z], tv_scr.at[z], ti_scr.at[z], None) for z in range(2)], kk)
    s0, s1 = tv_scr[0], tv_scr[1]
    e0, i1 = ti_scr[0] * float(PEER_KEYS), ti_scr[1]
    quarter = half // 2
    upper = lax.broadcasted_iota(I32, (half, LANES), 0) >= quarter
    s1q, i1q = (jnp.concatenate([t[0:quarter], t[0:quarter]], axis=0) for t in (s1, i1))
    duo = lambda t, i: jnp.where(upper, t[i + 1:i + 2], t[i:i + 1])
    cs = ([s0[0:1] + s1] + [s0[i:i + 1] + s1[0:half] for i in range(1, quarter)]
          + [duo(s0, i) + s1q for i in range(quarter, half, 2)] + [s0[half:] + s1[0:1]])
    ce = ([e0[0:1] + i1] + [e0[i:i + 1] + i1[0:half] for i in range(1, quarter)]
          + [duo(e0, i) + i1q for i in range(quarter, half, 2)] + [e0[half:] + i1[0:1]])
    _topk_store([(jnp.concatenate(cs, axis=0), bv_scr, be_scr, jnp.concatenate(ce, axis=0))], kk)
    best = bv_scr[...]
    ex = jnp.exp(best - best[0:1])
    rows = pl.ds(pl.multiple_of(head * kk, kk), kk)
    lanes = pl.ds(pl.multiple_of(tile * LANES, LANES), LANES)
    e_scr[rows, lanes] = be_scr[...]
    gt_scr[rows, lanes] = ex / jnp.sum(ex, axis=0, keepdims=True)


def _peer_kernel(nm_ref, nr_ref, wq_ref, keys_ref, u_ref, v_ref, out_ref,
                 g_scr, e_scr, gt_scr, et_scr, gtt_scr, tv_scr, ti_scr, bv_scr, be_scr):
    tm = nm_ref.shape[0]
    nk = PEER_KEYS
    ntile = tm // LANES
    tile_i = pl.program_id(0)
    step = pl.program_id(1)

    @pl.when((tile_i == 0) & (step == 0))
    def _():
        e_scr[...] = jnp.zeros_like(e_scr)
        gt_scr[...] = jnp.zeros_like(gt_scr)

    @pl.when(step == 0)
    def _():
        out_ref[...] = jnp.zeros_like(out_ref)
        et_scr[...] = e_scr[...].T.astype(I32)
        gtt_scr[...] = gt_scr[...].T
        sub = lax.broadcasted_iota(I32, (nk, LANES), 0)

        zero = jnp.zeros((nk, LANES), BF16)

        def pair_body(tp, carry):
            t0 = 2 * tp
            ats, bts = [], []
            for o in range(2):
                re = et_scr[pl.ds(t0 + o, 1), :]
                rg = 0.5 * gtt_scr[pl.ds(t0 + o, 1), :]
                ats.append(jnp.where(sub == (re >> PEER_KEY_BITS), rg, 0.0).astype(BF16))
                bts.append(jnp.where(sub == (re & (nk - 1)), 1.0, 0.0).astype(BF16))
            bt = jnp.concatenate([jnp.concatenate([bts[0], zero], axis=1),
                                  jnp.concatenate([zero, bts[1]], axis=1)], axis=0)
            g2 = _dot_nt(jnp.concatenate(ats, axis=1), bt)
            lo = pltpu.bitcast(g2[:, :LANES].astype(BF16).astype(F32), jnp.uint32) >> 16
            hi = pltpu.bitcast(g2[:, LANES:].astype(BF16).astype(F32), jnp.uint32) & jnp.uint32(0xFFFF0000)
            g_scr[pl.ds(tp, nk, stride=G_PITCH), :] = lo | hi
            return carry

        lax.fori_loop(0, tm // 2, pair_body, 0, unroll=TOKEN_UNROLL // 2)

    per_step = PEER_HEADS * ntile * EXP_ROWS // nk
    for k in range(per_step):
        unit = step * per_step + k
        _route_unit(nr_ref, wq_ref, keys_ref, e_scr, gt_scr, tv_scr, ti_scr, bv_scr, be_scr,
                    unit // ntile, lax.rem(unit, ntile))

    hid = _dot_nt(nm_ref[...], u_ref[...])
    act = hid * (1.0 + lax.erf(hid * (0.5 ** 0.5)))
    gates = [pltpu.bitcast(g_scr[pl.ds(pl.multiple_of((step * EXP_ROWS + r) * G_PITCH, SUBLANES), tm // 2), :], BF16)
             for r in range(EXP_ROWS)]
    wts = act.astype(BF16) * jnp.concatenate(gates, axis=1)
    out_ref[...] = out_ref[...] + _dot(wts, v_ref[...])


def _peer(n2, wq, keys, eu, ev):
    t, d = n2.shape
    tm = min(TM_EXP, t)
    assert tm == TM_EXP, "gate slab pitch is derived from TM_EXP"
    assert PEER_KEYS == LANES == 1 << PEER_KEY_BITS, "one lane per sub-key; expert ids decoded by shift and mask"
    hk = PEER_HEADS * PEER_TOPK
    er = EXP_ROWS * PEER_KEYS
    nsteps = eu.shape[0] // er
    nt = t // tm
    units = PEER_HEADS * (tm // LANES)
    assert units % nsteps == 0 and (tm // LANES) % (units // nsteps) == 0, "whole routing units of one head per step"
    spr = nsteps // PEER_HEADS
    qw = 2 * PEER_HALF
    prev = lambda i, s: (jnp.maximum(i - 1, 0), 0)
    return pl.pallas_call(
        _peer_kernel,
        grid=(nt + 1, nsteps),
        in_specs=[pl.BlockSpec((tm, d), prev),
                  pl.BlockSpec((tm, d), lambda i, s: (jnp.minimum(i, nt - 1), 0)),
                  pl.BlockSpec((d, qw), lambda i, s: (0, s // spr)),
                  pl.BlockSpec((1, 2, PEER_KEYS, PEER_HALF), lambda i, s: (s // spr, 0, 0, 0)),
                  pl.BlockSpec((er, d), lambda i, s: (s, 0)),
                  pl.BlockSpec((er, d), lambda i, s: (s, 0))],
        out_specs=pl.BlockSpec((tm, d), prev),
        out_shape=jax.ShapeDtypeStruct((t, d), F32),
        scratch_shapes=[pltpu.VMEM((PEER_KEYS * G_PITCH, LANES), jnp.uint32),
                        pltpu.VMEM((hk, tm), F32), pltpu.VMEM((hk, tm), F32),
                        pltpu.VMEM((tm, hk), I32), pltpu.VMEM((tm, hk), F32),
                        pltpu.VMEM((2, PEER_TOPK, LANES), F32), pltpu.VMEM((2, PEER_TOPK, LANES), F32),
                        pltpu.VMEM((PEER_TOPK, LANES), F32), pltpu.VMEM((PEER_TOPK, LANES), F32)],
        compiler_params=_cparams(("arbitrary", "arbitrary")),
        name="peer_ffn",
    )(n2, n2, wq, keys, eu, ev)


def _ple_kernel(h_ref, f_ref, p_ref, gp_ref, wg_ref, wp_ref, gf_ref, y_ref):
    h = h_ref[...] + f_ref[...]
    gate = jax.nn.sigmoid(_dot(_rms(h, gp_ref[...]).astype(BF16), wg_ref[...]))
    h = h + gate * _dot(p_ref[...].astype(BF16), wp_ref[...])
    y_ref[...] = _rms(h, gf_ref[...])


def _ple(h1, ffn, p2d, gp, wg, wp, gfin):
    t, d = h1.shape
    pd = p2d.shape[1]
    tm = min(TM_PLE, t)
    vec = lambda: pl.BlockSpec((1, d), lambda i: (0, 0))
    tok = lambda: pl.BlockSpec((tm, d), lambda i: (i, 0))
    return pl.pallas_call(
        _ple_kernel,
        grid=(t // tm,),
        in_specs=[tok(), tok(), pl.BlockSpec((tm, pd), lambda i: (i, 0)),
                  vec(), pl.BlockSpec((d, d), lambda i: (0, 0)), pl.BlockSpec((pd, d), lambda i: (0, 0)), vec()],
        out_specs=tok(),
        out_shape=jax.ShapeDtypeStruct((t, d), F32),
        compiler_params=_cparams(("parallel",)),
        name="ple_final",
    )(h1, ffn, p2d, gp, wg, wp, gfin)


def _pad_lanes(v):
    return jnp.pad(v, (0, LANES - v.shape[0]))


def _prepare(norm_mix, w_in, ssd_conv_w, ssd_conv_b, ssd_a_log, ssd_dt_bias, ssd_d, ssd_norm, w_ssd_out,
             gdn_conv_w, gdn_a_log, gdn_dt_bias, gdn_norm, w_gdn_out, w_out, norm_ffn, peer_query, peer_keys,
             expert_u, expert_v, norm_ple, w_ple_gate, w_ple_proj, norm_final):
    w = w_in[0]
    xbc = SSD_INNER + 2 * SSD_GROUPS * SSD_STATE
    c_dt = SSD_INNER + xbc
    c_qkv = c_dt + 2 * SSD_HEADS
    c_a = c_qkv + 4 * GDN_INNER
    c_ga = c_a + 4 * GDN_HEADS
    w_main = jnp.concatenate([w[:, :c_dt], w[:, c_qkv:c_a], w[:, c_ga:]], axis=1).astype(BF16)
    w_small = jnp.concatenate([w[:, c_dt:c_qkv], w[:, c_a:c_ga]], axis=1)
    w_small = jnp.pad(w_small, ((0, 0), (0, LANES - w_small.shape[1]))).astype(BF16)
    hp = jnp.stack([
        _pad_lanes(jnp.concatenate([ssd_dt_bias[0].reshape(-1), gdn_dt_bias[0].reshape(-1)])),
        _pad_lanes(jnp.concatenate([ssd_a_log[0].reshape(-1), gdn_a_log[0].reshape(-1)])),
        _pad_lanes(ssd_d[0]),
    ])
    hp = jnp.pad(hp, ((0, SUBLANES - hp.shape[0]), (0, 0)))
    row = lambda v: v.reshape(1, -1)
    return dict(
        g_mix=row(norm_mix[0]), w_main=w_main, w_small=w_small, hp=hp,
        ssd_conv_w=ssd_conv_w[0], ssd_conv_b=row(ssd_conv_b[0]), ssd_norm=row(ssd_norm[0]),
        gdn_conv_w=gdn_conv_w[0], gdn_norm=row(gdn_norm[0]),
        wa=w_ssd_out[0].astype(BF16), wb=w_gdn_out[0].astype(BF16), wo=w_out[0].astype(BF16),
        g_ffn=row(norm_ffn[0]), wq=peer_query[0].astype(BF16),
        keys=peer_keys[0].astype(BF16),
        eu=expert_u[0].astype(BF16), ev=expert_v[0].astype(BF16),
        g_ple=row(norm_ple[0]), wg=w_ple_gate[0].astype(BF16), wp=w_ple_proj[0].astype(BF16),
        g_fin=row(norm_final),
    )


def _trunk(x, ple, p):
    b, length, d = x.shape
    t = b * length
    x2d = x.reshape(t, d)
    proj, small = _in_proj(x2d, p["g_mix"], p["w_main"], p["w_small"])
    proj3 = proj.reshape(b, length, -1)
    small3 = small.reshape(b, length, LANES)
    ya = _ssd_mixer(proj3, small3, p["ssd_conv_w"], p["ssd_conv_b"], p["hp"], p["ssd_norm"])
    yb = _gdn_mixer(proj3, small3, p["gdn_conv_w"], p["hp"], p["gdn_norm"])
    h1, n2 = _merge(x2d, ya.reshape(t, -1), yb.reshape(t, -1), proj, p["wa"], p["wb"], p["wo"], p["g_ffn"])
    ffn = _peer(n2, p["wq"], p["keys"], p["eu"], p["ev"])
    y = _ple(h1, ffn, ple.reshape(t, -1), p["g_ple"], p["wg"], p["wp"], p["g_fin"])
    return y.reshape(b, length, d)


def kernel(x_prompt, x_sample, p_prompt, p_sample, norm_mix, w_in, ssd_conv_w, ssd_conv_b, ssd_a_log, ssd_dt_bias, ssd_d, ssd_norm, w_ssd_out, gdn_conv_w, gdn_a_log, gdn_dt_bias, gdn_norm, w_gdn_out, w_out, norm_ffn, peer_query, peer_keys, expert_u, expert_v, norm_ple, w_ple_gate, w_ple_proj, norm_final):
    assert w_in.shape[0] == 1, "single-layer trunk"
    p = _prepare(norm_mix, w_in, ssd_conv_w, ssd_conv_b, ssd_a_log, ssd_dt_bias, ssd_d, ssd_norm, w_ssd_out,
                 gdn_conv_w, gdn_a_log, gdn_dt_bias, gdn_norm, w_gdn_out, w_out, norm_ffn, peer_query, peer_keys,
                 expert_u, expert_v, norm_ple, w_ple_gate, w_ple_proj, norm_final)
    return (_trunk(x_prompt, p_prompt[0], p), _trunk(x_sample, p_sample[0], p))
```

```python
import jax
import jax.numpy as jnp
from jax import lax
from jax.experimental import pallas as pl
from jax.experimental.pallas import tpu as pltpu

F32 = jnp.float32
BF16 = jnp.bfloat16
I32 = jnp.int32

EPS = 1e-6
CONV_K = 4
SSD_HEADS = 16
SSD_HEAD_DIM = 64
SSD_GROUPS = 4
SSD_STATE = 128
SSD_INNER = SSD_HEADS * SSD_HEAD_DIM
GDN_HEADS = 8
GDN_HEAD_DIM = 128
GDN_INNER = GDN_HEADS * GDN_HEAD_DIM
PEER_HEADS = 8
PEER_KEYS = 128
PEER_TOPK = 16
PEER_HALF = 128
PEER_KEY_BITS = PEER_KEYS.bit_length() - 1

LANES = 128
SUBLANES = 8
SSD_CHUNK = 128
SSD_STEP_GROUPS = 2
GDN_CHUNK = 128
GDN_PREP_CHUNKS = 8
GDN_STEP_HEADS = 2
GDN_INV_BLOCK = 8
VMEM_LIMIT = 56 * 1024 * 1024

TM_PROJ = 2048
TN_PROJ = 1024
TM_MERGE = 512
TM_EXP = 512
EXP_ROWS = 16
G_PITCH = TM_EXP // 2 + 8
TOKEN_UNROLL = 64
TM_PLE = 1024

NEG = -1e30


def _dot(a, b):
    return jnp.dot(a, b, preferred_element_type=F32)


def _dot_nt(a, b):
    return lax.dot_general(a, b, (((1,), (1,)), ((), ())), preferred_element_type=F32)


def _dot_tn(a, b):
    return lax.dot_general(a, b, (((0,), (0,)), ((), ())), preferred_element_type=F32)


def _bf16_terms(m, terms):
    out = []
    for _ in range(terms - 1):
        t = m.astype(BF16)
        out.append(t)
        m = m - t.astype(F32)
    out.append(m.astype(BF16))
    return out


def _dot_exact_lhs(lb, m, terms=3):
    n = m.shape[1]
    r = _dot(lb, jnp.concatenate(_bf16_terms(m, terms), axis=1))
    return sum(r[:, i * n:(i + 1) * n] for i in range(1, terms)) + r[:, :n]


def _dot_exact_rhs(m, eb, terms=2):
    return _dot(jnp.concatenate(_bf16_terms(m, terms), axis=1), jnp.concatenate([eb] * terms, axis=0))


def _rows_transposed(sel, m, terms=3):
    n = m.shape[0]
    r = _dot_nt(sel, jnp.concatenate(_bf16_terms(m, terms), axis=0))
    return sum(r[:, i * n:(i + 1) * n] for i in range(1, terms)) + r[:, :n]


def _softplus(x):
    return jnp.maximum(x, 0.0) + jnp.log1p(jnp.exp(-jnp.abs(x)))


def _silu(x):
    return x * jax.nn.sigmoid(x)


def _rms(x, g):
    return x * lax.rsqrt(jnp.mean(x * x, axis=-1, keepdims=True) + EPS) * g


def _cparams(sem):
    return pltpu.CompilerParams(dimension_semantics=sem, vmem_limit_bytes=VMEM_LIMIT)


def _in_proj_kernel(x_ref, g_ref, wm_ref, ws_ref, main_ref, small_ref, n_scr):
    @pl.when(pl.program_id(1) == 0)
    def _():
        n = _rms(x_ref[...], g_ref[...]).astype(BF16)
        n_scr[...] = n
        small_ref[...] = _dot(n, ws_ref[...])

    main_ref[...] = _dot(n_scr[...], wm_ref[...])


def _in_proj(x2d, g, w_main, w_small):
    t, d = x2d.shape
    n_main = w_main.shape[1]
    tm = min(TM_PROJ, t)
    return pl.pallas_call(
        _in_proj_kernel,
        grid=(t // tm, n_main // TN_PROJ),
        in_specs=[
            pl.BlockSpec((tm, d), lambda i, j: (i, 0)),
            pl.BlockSpec((1, d), lambda i, j: (0, 0)),
            pl.BlockSpec((d, TN_PROJ), lambda i, j: (0, j)),
            pl.BlockSpec((d, LANES), lambda i, j: (0, 0)),
        ],
        out_specs=[
            pl.BlockSpec((tm, TN_PROJ), lambda i, j: (i, j)),
            pl.BlockSpec((tm, LANES), lambda i, j: (i, 0)),
        ],
        out_shape=[jax.ShapeDtypeStruct((t, n_main), F32), jax.ShapeDtypeStruct((t, LANES), F32)],
        scratch_shapes=[pltpu.VMEM((tm, d), BF16)],
        compiler_params=_cparams(("parallel", "arbitrary")),
        name="in_proj",
    )(x2d, g, w_main, w_small)


def _conv_silu_chunk(src_ref, w_ref, b_ref, c, nch, rows, lanes=slice(None)):
    length = src_ref.shape[1]
    base = pl.multiple_of(c * rows, rows)
    main = src_ref[0, pl.ds(base, rows), lanes]
    prev = src_ref[0, pl.ds(pl.multiple_of(jnp.maximum(base - SUBLANES, 0), SUBLANES), SUBLANES), lanes]
    nxt = src_ref[0, pl.ds(pl.multiple_of(jnp.minimum(base + rows, length - SUBLANES), SUBLANES), SUBLANES), lanes]
    prev = jnp.where(c > 0, prev, 0.0)
    nxt = jnp.where(c < nch - 1, nxt, 0.0)
    ext = jnp.concatenate([prev, main, nxt], axis=0)
    w = w_ref[:, lanes]
    lo = SUBLANES - CONV_K // 2
    y = ext[lo:lo + rows] * w[0:1]
    for j in range(1, CONV_K):
        y = y + ext[lo + j:lo + j + rows] * w[j:j + 1]
    if b_ref is not None:
        y = y + b_ref[:, lanes]
    return _silu(y)


def _expand_heads(t, lane0, width):
    r = t.shape[0]
    hid = lax.broadcasted_iota(I32, (r, 4 * width), 1) // width
    out = jnp.broadcast_to(t[:, lane0 + 3:lane0 + 4], (r, 4 * width))
    for j in (2, 1, 0):
        out = jnp.where(hid == j, t[:, lane0 + j:lane0 + j + 1], out)
    return out


def _ssd_kernel(z_ref, xs_ref, bm_ref, cm_ref, sm_ref, cwx_ref, cwb_ref, cwc_ref, cbx_ref, cbb_ref, cbc_ref,
                hp_ref, nw_ref, out_ref, xc_scr, bc_scr, cc_scr, y_scr, st_scr):
    ch = SSD_CHUNK
    length = z_ref.shape[1]
    nch = length // ch
    gwid = 4 * SSD_HEAD_DIM
    ng = SSD_STEP_GROUPS
    gsteps = range(ng)
    shifts = [lax.rem(LANES - 4 * (pl.program_id(1) * ng + gg), LANES) for gg in gsteps]
    hps = [pltpu.roll(hp_ref[...], s, 1) for s in shifts]
    bias_rows = [hp[0:1, :] for hp in hps]
    a_rows = [-jnp.exp(hp[1:2, :]) for hp in hps]
    xlanes = [slice(gg * gwid, (gg + 1) * gwid) for gg in gsteps]
    slanes = [slice(gg * SSD_STATE, (gg + 1) * SSD_STATE) for gg in gsteps]

    row = lax.broadcasted_iota(I32, (ch, ch), 0)
    col = lax.broadcasted_iota(I32, (ch, ch), 1)
    hid = lax.broadcasted_iota(I32, (ch, gwid), 1) // SSD_HEAD_DIM

    def conv_body(c, carry):
        base = pl.multiple_of(c * ch, ch)
        for gg in gsteps:
            xc_scr[gg, pl.ds(base, ch), :] = _conv_silu_chunk(xs_ref, cwx_ref, cbx_ref, c, nch, ch, xlanes[gg])
            bc_scr[gg, pl.ds(base, ch), :] = _conv_silu_chunk(bm_ref, cwb_ref, cbb_ref, c, nch, ch, slanes[gg])
            cc_scr[gg, pl.ds(base, ch), :] = _conv_silu_chunk(cm_ref, cwc_ref, cbc_ref, c, nch, ch, slanes[gg])
        return carry

    lax.fori_loop(0, nch, conv_body, 0)

    masks = ((row >= col), (row <= col))
    lmats = tuple(jnp.where(m, 1.0, 0.0).astype(BF16) for m in masks)
    lanes0 = (0, SSD_HEADS)
    tot_rows = (ch - 1, 0)
    erow = lax.broadcasted_iota(I32, (LANES, gwid), 0)
    ecol = lax.broadcasted_iota(I32, (LANES, gwid), 1) // SSD_HEAD_DIM
    emats = tuple(jnp.where(erow == ecol + l0, 1.0, 0.0).astype(BF16) for l0 in lanes0)
    srow = lax.broadcasted_iota(I32, (SUBLANES, LANES), 0)
    scol = lax.broadcasted_iota(I32, (SUBLANES, LANES), 1)
    sels = tuple(jnp.where(scol == srow + l0, 1.0, 0.0).astype(BF16) for l0 in lanes0)
    chains = [(gg, d) for gg in gsteps for d in (0, 1)]

    st_scr[...] = jnp.zeros_like(st_scr)

    def scan_body(ci, carry):
        bases = (pl.multiple_of(ci * ch, ch), pl.multiple_of((nch - 1 - ci) * ch, ch))
        x = [xc_scr[gg, pl.ds(bases[d], ch), :] for gg, d in chains]
        bb = [bc_scr[gg, pl.ds(bases[d], ch), :].astype(BF16) for gg, d in chains]
        cb = [cc_scr[gg, pl.ds(bases[d], ch), :].astype(BF16) for gg, d in chains]
        dt_all = [_softplus(pltpu.roll(sm_ref[0, pl.ds(bases[d], ch), :], shifts[gg], 1) + bias_rows[gg])
                  for gg, d in chains]
        acum = [_dot_exact_lhs(lmats[d], dt * a_rows[gg]) for (gg, d), dt in zip(chains, dt_all)]
        sc = [jnp.where(masks[d], _dot_nt(c_, b_), 0.0) for (gg, d), c_, b_ in zip(chains, cb, bb)]
        st = [st_scr[gg, d] for gg, d in chains]
        y_off = [_dot(c_, s_.astype(BF16)) for c_, s_ in zip(cb, st)]
        dt_x = [_dot_exact_rhs(dt, emats[d]) for (gg, d), dt in zip(chains, dt_all)]
        acum_t = [_rows_transposed(sels[d], a_) for (gg, d), a_ in zip(chains, acum)]
        tot = [a_[tot_rows[d]:tot_rows[d] + 1, :] for (gg, d), a_ in zip(chains, acum)]
        decay_x = [_dot_exact_rhs(jnp.concatenate([jnp.exp(a_), jnp.exp(t_ - a_)], axis=0), emats[d])
                   for (gg, d), a_, t_ in zip(chains, acum, tot)]
        xd = [x_ * d_ for x_, d_ in zip(x, dt_x)]
        lhs, rhs = [], []
        for k, (gg, d) in enumerate(chains):
            mixes, parts = [], []
            for j in range(4):
                ln = lanes0[d] + j
                dd = acum[k][:, ln:ln + 1] - acum_t[k][j:j + 1, :]
                mixes.append((sc[k] * jnp.exp(jnp.where(masks[d], dd, NEG))).astype(BF16))
                parts.append(jnp.where(hid == j, xd[k], 0.0).astype(BF16))
            lhs.append(jnp.concatenate(mixes, axis=1))
            rhs.append(jnp.concatenate(parts, axis=0))
        y = [_dot(l_, r_) for l_, r_ in zip(lhs, rhs)]
        upd = [_dot_tn(b_, (x_ * d_[ch:]).astype(BF16)) for b_, x_, d_ in zip(bb, xd, decay_x)]
        for k, (gg, d) in enumerate(chains):
            ea_x = decay_x[k][:ch]
            etot_x = ea_x[tot_rows[d]:tot_rows[d] + 1, :]
            y_scr[gg, d, pl.ds(bases[d], ch), :] = y[k] + y_off[k] * ea_x
            st_scr[gg, d] = st[k] * etot_x + upd[k]
        return carry

    lax.fori_loop(0, nch, scan_body, 0)

    dsk_x = [_expand_heads(hp[2:3, :], 0, SSD_HEAD_DIM) for hp in hps]

    def fin_body(c, carry):
        rows = pl.ds(pl.multiple_of(c * ch, ch), ch)
        for gg in gsteps:
            y = y_scr[gg, 0, rows, :] + y_scr[gg, 1, rows, :] + xc_scr[gg, rows, :] * dsk_x[gg]
            y = y * _silu(z_ref[0, rows, xlanes[gg]])
            out_ref[0, rows, xlanes[gg]] = _rms(y, nw_ref[:, xlanes[gg]])
        return carry

    lax.fori_loop(0, nch, fin_body, 0)


def _ssd_mixer(proj3, small3, conv_w, conv_b, hp, norm_w):
    b, length, _ = proj3.shape
    ng = SSD_STEP_GROUPS
    steps = SSD_GROUPS // ng
    gw = ng * 4 * SSD_HEAD_DIM
    sw = ng * SSD_STATE
    xs0 = SSD_INNER // gw
    b0 = 2 * SSD_INNER // sw
    c0 = b0 + steps
    nb_x = SSD_INNER // sw
    seq = lambda w, off: pl.BlockSpec((1, length, w), lambda i, g: (i, 0, off + g))
    par = lambda r, w, off: pl.BlockSpec((r, w), lambda i, g: (0, off + g))
    return pl.pallas_call(
        _ssd_kernel,
        grid=(b, steps),
        in_specs=[
            seq(gw, 0), seq(gw, xs0), seq(sw, b0), seq(sw, c0),
            pl.BlockSpec((1, length, LANES), lambda i, g: (i, 0, 0)),
            par(CONV_K, gw, 0), par(CONV_K, sw, nb_x), par(CONV_K, sw, nb_x + steps),
            par(1, gw, 0), par(1, sw, nb_x), par(1, sw, nb_x + steps),
            pl.BlockSpec((SUBLANES, LANES), lambda i, g: (0, 0)),
            par(1, gw, 0),
        ],
        out_specs=pl.BlockSpec((1, length, gw), lambda i, g: (i, 0, g)),
        out_shape=jax.ShapeDtypeStruct((b, length, SSD_INNER), F32),
        scratch_shapes=[
            pltpu.VMEM((ng, length, 4 * SSD_HEAD_DIM), F32),
            pltpu.VMEM((ng, length, SSD_STATE), F32), pltpu.VMEM((ng, length, SSD_STATE), F32),
            pltpu.VMEM((ng, 2, length, 4 * SSD_HEAD_DIM), F32),
            pltpu.VMEM((ng, 2, SSD_STATE, 4 * SSD_HEAD_DIM), F32),
        ],
        compiler_params=_cparams(("parallel", "arbitrary")),
        name="ssd_mixer",
    )(proj3, proj3, proj3, proj3, small3, conv_w, conv_w, conv_w, conv_b, conv_b, conv_b, hp, norm_w)


def _gdn_kernel(q_ref, k_ref, v_ref, z_ref, sm_ref, cwq_ref, cwk_ref, cwv_ref, hp_ref, nw_ref, out_ref,
                qn_scr, kn_scr, vn_scr, nc_scr, wq_scr, rc_scr, et_scr, o_scr):
    ch = GDN_CHUNK
    length = z_ref.shape[1]
    nch = length // ch
    dk = GDN_HEAD_DIM
    nh = GDN_STEP_HEADS
    lane_a = 2 * SSD_HEADS
    lane_beta = lane_a + 2 * GDN_HEADS

    def head_shift(hh):
        return lax.rem(LANES - (pl.program_id(1) * nh + hh), LANES)

    def head_lanes(hh):
        return pl.ds(pl.multiple_of(hh * dk, dk), dk)

    row = lax.broadcasted_iota(I32, (ch, ch), 0)
    col = lax.broadcasted_iota(I32, (ch, ch), 1)
    eye = jnp.where(row == col, 1.0, 0.0)
    n_double = (GDN_INV_BLOCK - 1).bit_length() - 1
    levels = []
    size = GDN_INV_BLOCK
    inside = (row // size) == (col // size)
    levels.append(inside)
    while size < ch:
        size *= 2
        merged = (row // size) == (col // size)
        levels.append(merged & jnp.logical_not(inside))
        inside = merged

    def conv_body(idx, carry):
        hh, c = idx // nch, lax.rem(idx, nch)
        base = pl.multiple_of(c * ch, ch)
        lanes = head_lanes(hh)
        q = _conv_silu_chunk(q_ref, cwq_ref, None, c, nch, ch, lanes)
        k = _conv_silu_chunk(k_ref, cwk_ref, None, c, nch, ch, lanes)
        qn_scr[hh, pl.ds(base, ch), :] = q * (lax.rsqrt(jnp.sum(q * q, axis=-1, keepdims=True) + EPS) * dk ** -0.5)
        kn_scr[hh, pl.ds(base, ch), :] = k * lax.rsqrt(jnp.sum(k * k, axis=-1, keepdims=True) + EPS)
        vn_scr[hh, pl.ds(base, ch), :] = _conv_silu_chunk(v_ref, cwv_ref, None, c, nch, ch, lanes)
        return carry

    lax.fori_loop(0, nh * nch, conv_body, 0)

    incl = ((row >= col), (row <= col))
    strict = ((row > col), (row < col))
    lmat = tuple(jnp.where(m, 1.0, 0.0).astype(BF16) for m in incl)
    umat = tuple(jnp.where(m, 1.0, 0.0) for m in strict)
    jref = (0, ch - 1)
    tot_row = (ch - 1, 0)
    group = min(GDN_PREP_CHUNKS, nch)

    def prep_body(idx, carry):
        hh, gi = idx // (nch // group), lax.rem(idx, nch // group)
        shift = head_shift(hh)
        hp = pltpu.roll(hp_ref[...], shift, 1)
        bias_row = hp[0:1, :]
        acoef_row = jnp.exp(hp[1:2, :])
        chains = []
        for ci in range(group):
            c = gi * group + ci
            base = pl.multiple_of(c * ch, ch)
            q = qn_scr[hh, pl.ds(base, ch), :]
            k = kn_scr[hh, pl.ds(base, ch), :]
            v = vn_scr[hh, pl.ds(base, ch), :]
            sm = pltpu.roll(sm_ref[0, pl.ds(base, ch), :], shift, 1)
            g_all = -acoef_row * _softplus(sm + bias_row)
            beta_all = jax.nn.sigmoid(sm)
            kbf = k.astype(BF16)
            kq = _dot_nt(jnp.concatenate([kbf, q.astype(BF16)], axis=0), kbf)
            for di in (0, 1):
                ln = lane_a + di * GDN_HEADS
                lb = lane_beta + di * GDN_HEADS
                chains.append(dict(c=c, di=di, q=q, k=k, v=v, kq=kq, g=g_all[:, ln:ln + 1],
                                   beta=beta_all[:, lb:lb + 1]))
        for t in chains:
            t["dmat"] = _dot_exact_lhs(lmat[t["di"]], t["g"] * umat[t["di"]], terms=2)
        for t in chains:
            di = t["di"]
            t["kb"] = t["k"] * t["beta"]
            dec = t["dmat"][:, jref[di]:jref[di] + 1] + t["g"][jref[di]:jref[di] + 1, :]
            t["dec"] = dec
            t["tot"] = dec[tot_row[di]:tot_row[di] + 1, :]
            gam = jnp.exp(jnp.where(incl[di], t["dmat"], NEG))
            t["qk"] = (t["kq"][ch:] * gam).astype(BF16)
            a = jnp.where(strict[di], t["kq"][:ch] * (gam * t["beta"]), 0.0)
            ad = jnp.where(levels[0], a, 0.0)
            t["p"] = eye - ad
            t["ab"] = ad.astype(BF16)
            t["off"] = [jnp.where(m, a, 0.0).astype(BF16) for m in levels[1:]]
        for t in chains:
            t["x"] = _dot(t["ab"], t["ab"])
        for i in range(n_double):
            last = i == n_double - 1
            for t in chains:
                xb = t["x"].astype(BF16)
                pb = t["p"].astype(BF16)
                t["r"] = _dot(xb, pb) if last else _dot(xb, jnp.concatenate([pb, xb], axis=1))
            for t in chains:
                t["p"] = t["p"] + t["r"][:, :ch]
                if not last:
                    t["x"] = t["r"][:, ch:]
        for lv in range(len(levels) - 1):
            for t in chains:
                t["m"] = _dot(t["off"][lv], t["p"].astype(BF16))
            for t in chains:
                t["p"] = t["p"] - _dot(t["p"].astype(BF16), t["m"].astype(BF16))
        for t in chains:
            t["edec"] = jnp.exp(t["dec"])
            rhs = jnp.concatenate([t["v"] * t["beta"], t["kb"] * t["edec"]], axis=1).astype(BF16)
            t["uw"] = _dot(t["p"].astype(BF16), rhs)
        for t in chains:
            uwb = t["uw"].astype(BF16)
            kd = (t["k"] * jnp.exp(t["tot"] - t["dec"])).astype(BF16)
            t["kuw"] = _dot_tn(kd, uwb)
            t["quw"] = _dot(t["qk"], uwb)
        for t in chains:
            c, di = t["c"], t["di"]
            base = pl.multiple_of(c * ch, ch)
            base2 = pl.multiple_of(c * 2 * ch, 2 * ch)
            nc_scr[hh, di, pl.ds(base, dk), :] = t["kuw"][:, :dk]
            rc_scr[hh, di, pl.ds(base, ch), :] = t["quw"][:, :dk]
            wq_scr[hh, di, pl.ds(base2, 2 * ch), :] = jnp.concatenate(
                [t["kuw"][:, dk:], t["q"] * t["edec"] - t["quw"][:, dk:]], axis=0).astype(BF16)
            et_scr[hh, di, pl.ds(pl.multiple_of(c * SUBLANES, SUBLANES), SUBLANES), :] = jnp.broadcast_to(
                jnp.exp(t["tot"]), (SUBLANES, LANES))
        return carry

    lax.fori_loop(0, nh * (nch // group), prep_body, 0)

    chains = [(hh, di) for hh in range(nh) for di in (0, 1)]

    def rec_body(ci, carry):
        cs = (ci, nch - 1 - ci)
        bases = [pl.multiple_of(c * ch, ch) for c in cs]
        ws = [_dot(wq_scr[hh, di, pl.ds(pl.multiple_of(cs[di] * 2 * ch, 2 * ch), 2 * ch), :], s.astype(BF16))
              for (hh, di), s in zip(chains, carry)]
        s_new = []
        for j, (hh, di) in enumerate(chains):
            et = et_scr[hh, di, pl.ds(pl.multiple_of(cs[di] * SUBLANES, SUBLANES), 1), :]
            s_new.append(carry[j] * et - ws[j][:dk] + nc_scr[hh, di, pl.ds(bases[di], dk), :])
            o_scr[hh, di, pl.ds(bases[di], ch), :] = ws[j][ch:] + rc_scr[hh, di, pl.ds(bases[di], ch), :]
        return tuple(s_new)

    zero = jnp.zeros((dk, dk), F32)
    lax.fori_loop(0, nch, rec_body, (zero,) * len(chains))

    def fin_body(idx, carry):
        hh, c = idx // nch, lax.rem(idx, nch)
        rows = pl.ds(pl.multiple_of(c * ch, ch), ch)
        lanes = head_lanes(hh)
        o = o_scr[hh, 0, rows, :] + o_scr[hh, 1, rows, :]
        out_ref[0, rows, lanes] = _rms(o, nw_ref[...]) * _silu(z_ref[0, rows, lanes])
        return carry

    lax.fori_loop(0, nh * nch, fin_body, 0)


def _gdn_mixer(proj3, small3, conv_w, hp, norm_w):
    b, length, _ = proj3.shape
    dk = GDN_HEAD_DIM
    nch = length // GDN_CHUNK
    nh = GDN_STEP_HEADS
    groups = GDN_HEADS // nh
    q0 = (2 * SSD_INNER + 2 * SSD_GROUPS * SSD_STATE) // (nh * dk)
    seq = lambda off: pl.BlockSpec((1, length, nh * dk), lambda i, h: (i, 0, off + h))
    par = lambda off: pl.BlockSpec((CONV_K, nh * dk), lambda i, h: (0, off + h))
    return pl.pallas_call(
        _gdn_kernel,
        grid=(b, groups),
        in_specs=[
            seq(q0), seq(q0 + groups), seq(q0 + 2 * groups), seq(q0 + 3 * groups),
            pl.BlockSpec((1, length, LANES), lambda i, h: (i, 0, 0)),
            par(0), par(groups), par(2 * groups),
            pl.BlockSpec((SUBLANES, LANES), lambda i, h: (0, 0)),
            pl.BlockSpec((1, dk), lambda i, h: (0, 0)),
        ],
        out_specs=pl.BlockSpec((1, length, nh * dk), lambda i, h: (i, 0, h)),
        out_shape=jax.ShapeDtypeStruct((b, length, GDN_INNER), F32),
        scratch_shapes=[
            pltpu.VMEM((nh, length, dk), F32), pltpu.VMEM((nh, length, dk), F32), pltpu.VMEM((nh, length, dk), F32),
            pltpu.VMEM((nh, 2, nch * dk, dk), F32),
            pltpu.VMEM((nh, 2, 2 * length, dk), BF16),
            pltpu.VMEM((nh, 2, length, dk), F32),
            pltpu.VMEM((nh, 2, nch * SUBLANES, LANES), F32),
            pltpu.VMEM((nh, 2, length, dk), F32),
        ],
        compiler_params=_cparams(("parallel", "arbitrary")),
        name="gdn_mixer",
    )(proj3, proj3, proj3, proj3, small3, conv_w, conv_w, conv_w, hp, norm_w)


def _merge_kernel(x_ref, ya_ref, yb_ref, ga_ref, gb_ref, wa_ref, wb_ref, wo_ref, gf_ref, h_ref, n_ref):
    ya = _dot(ya_ref[...].astype(BF16), wa_ref[...])
    yb = _dot(yb_ref[...].astype(BF16), wb_ref[...])
    merged = jax.nn.sigmoid(ga_ref[...]) * ya + jax.nn.sigmoid(gb_ref[...]) * yb
    h = x_ref[...] + _dot(merged.astype(BF16), wo_ref[...])
    h_ref[...] = h
    n_ref[...] = _rms(h, gf_ref[...]).astype(BF16)


def _merge(x2d, ya, yb, proj2d, wa, wb, wo, gf):
    t, d = x2d.shape
    tm = min(TM_MERGE, t)
    ga0 = (proj2d.shape[1] - 2 * d) // d
    tok = lambda: pl.BlockSpec((tm, d), lambda i: (i, 0))
    wsp = lambda: pl.BlockSpec((d, d), lambda i: (0, 0))
    return pl.pallas_call(
        _merge_kernel,
        grid=(t // tm,),
        in_specs=[tok(), tok(), tok(),
                  pl.BlockSpec((tm, d), lambda i: (i, ga0)), pl.BlockSpec((tm, d), lambda i: (i, ga0 + 1)),
                  wsp(), wsp(), wsp(), pl.BlockSpec((1, d), lambda i: (0, 0))],
        out_specs=[tok(), tok()],
        out_shape=[jax.ShapeDtypeStruct((t, d), F32), jax.ShapeDtypeStruct((t, d), BF16)],
        compiler_params=_cparams(("parallel",)),
        name="merge",
    )(x2d, ya, yb, proj2d, proj2d, wa, wb, wo, gf)


def _topk_store(problems, k):
    state = [p[0] for p in problems]
    iotas = [lax.broadcasted_iota(I32, s.shape, 0).astype(F32) for s in state]
    for i in range(k):
        ms = [jnp.max(s, axis=0, keepdims=True) for s in state]
        cands = [jnp.where(s == m, io, float(s.shape[0])) for s, m, io in zip(state, ms, iotas)]
        ixs = [jnp.min(c, axis=0, keepdims=True) for c in cands]
        hits = [c == ix for c, ix in zip(cands, ixs)]
        for (_, vals_ref, outs_ref, payload), m, ix, hit in zip(problems, ms, ixs, hits):
            vals_ref[i:i + 1, :] = m
            outs_ref[i:i + 1, :] = (ix if payload is None
                                    else jnp.max(jnp.where(hit, payload, -1.0), axis=0, keepdims=True))
        state = [jnp.where(hit, -jnp.inf, s) for s, hit in zip(state, hits)]


def _route_unit(tok_ref, wq_ref, keys_ref, e_scr, gt_scr, tv_scr, ti_scr, bv_scr, be_scr, head, tile):
    kk = PEER_TOPK
    half = kk // 2
    toks = tok_ref[pl.ds(pl.multiple_of(tile * LANES, LANES), LANES), :]
    qry = _dot(toks, wq_ref[...])
    st = [_dot_nt(keys_ref[0, z], qry[:, z * PEER_HALF:(z + 1) * PEER_HALF].astype(BF16)) for z in range(2)]
    _topk_store([(st[z], tv_scr.at[z], ti_scr.at[z], None) for z in range(2)], kk)
    s0, s1 = tv_scr[0], tv_scr[1]
    e0, i1 = ti_scr[0] * float(PEER_KEYS), ti_scr[1]
    quarter = half // 2
    upper = lax.broadcasted_iota(I32, (half, LANES), 0) >= quarter
    s1q, i1q = (jnp.concatenate([t[0:quarter], t[0:quarter]], axis=0) for t in (s1, i1))
    duo = lambda t, i: jnp.where(upper, t[i + 1:i + 2], t[i:i + 1])
    cs = ([s0[0:1] + s1] + [s0[i:i + 1] + s1[0:half] for i in range(1, quarter)]
          + [duo(s0, i) + s1q for i in range(quarter, half, 2)] + [s0[half:] + s1[0:1]])
    ce = ([e0[0:1] + i1] + [e0[i:i + 1] + i1[0:half] for i in range(1, quarter)]
          + [duo(e0, i) + i1q for i in range(quarter, half, 2)] + [e0[half:] + i1[0:1]])
    _topk_store([(jnp.concatenate(cs, axis=0), bv_scr, be_scr, jnp.concatenate(ce, axis=0))], kk)
    best = bv_scr[...]
    ex = jnp.exp(best - best[0:1])
    rows = pl.ds(pl.multiple_of(head * kk, kk), kk)
    lanes = pl.ds(pl.multiple_of(tile * LANES, LANES), LANES)
    e_scr[rows, lanes] = be_scr[...]
    gt_scr[rows, lanes] = ex / jnp.sum(ex, axis=0, keepdims=True)


def _peer_kernel(h_ref, nm_ref, nr_ref, wq_ref, keys_ref, u_ref, v_ref, out_ref,
                 g_scr, e_scr, gt_scr, et_scr, gtt_scr, tv_scr, ti_scr, bv_scr, be_scr):
    tm = nm_ref.shape[0]
    nk = PEER_KEYS
    ntile = tm // LANES
    tile_i = pl.program_id(0)
    step = pl.program_id(1)

    @pl.when((tile_i == 0) & (step == 0))
    def _():
        e_scr[...] = jnp.zeros_like(e_scr)
        gt_scr[...] = jnp.zeros_like(gt_scr)

    @pl.when(step == 0)
    def _():
        out_ref[...] = h_ref[...]
        et_scr[...] = e_scr[...].T.astype(I32)
        gtt_scr[...] = gt_scr[...].T
        sub = lax.broadcasted_iota(I32, (nk, LANES), 0)

        zero = jnp.zeros((nk, LANES), BF16)

        def pair_body(tp, carry):
            t0 = 2 * tp
            ats, bts = [], []
            for o in range(2):
                re = et_scr[pl.ds(t0 + o, 1), :]
                rg = 0.5 * gtt_scr[pl.ds(t0 + o, 1), :]
                ats.append(jnp.where(sub == (re >> PEER_KEY_BITS), rg, 0.0).astype(BF16))
                bts.append(jnp.where(sub == (re & (nk - 1)), 1.0, 0.0).astype(BF16))
            bt = jnp.concatenate([jnp.concatenate([bts[0], zero], axis=1),
                                  jnp.concatenate([zero, bts[1]], axis=1)], axis=0)
            g2 = _dot_nt(jnp.concatenate(ats, axis=1), bt)
            lo = pltpu.bitcast(g2[:, :LANES].astype(BF16).astype(F32), jnp.uint32) >> 16
            hi = pltpu.bitcast(g2[:, LANES:].astype(BF16).astype(F32), jnp.uint32) & jnp.uint32(0xFFFF0000)
            g_scr[pl.ds(tp, nk, stride=G_PITCH), :] = lo | hi
            return carry

        lax.fori_loop(0, tm // 2, pair_body, 0, unroll=TOKEN_UNROLL // 2)

    per_step = PEER_HEADS * ntile * EXP_ROWS // nk
    for k in range(per_step):
        unit = step * per_step + k
        _route_unit(nr_ref, wq_ref, keys_ref, e_scr, gt_scr, tv_scr, ti_scr, bv_scr, be_scr,
                    unit // ntile, lax.rem(unit, ntile))

    hid = _dot_nt(nm_ref[...], u_ref[...])
    act = hid * (1.0 + lax.erf(hid * (0.5 ** 0.5)))
    gates = [pltpu.bitcast(g_scr[pl.ds(pl.multiple_of((step * EXP_ROWS + r) * G_PITCH, SUBLANES), tm // 2), :], BF16)
             for r in range(EXP_ROWS)]
    wts = act.astype(BF16) * jnp.concatenate(gates, axis=1)
    out_ref[...] = out_ref[...] + _dot(wts, v_ref[...])


def _peer(h1, n2, wq, keys, eu, ev):
    t, d = n2.shape
    tm = min(TM_EXP, t)
    assert tm == TM_EXP, "gate slab pitch is derived from TM_EXP"
    assert PEER_KEYS == LANES == 1 << PEER_KEY_BITS, "one lane per sub-key; expert ids decoded by shift and mask"
    hk = PEER_HEADS * PEER_TOPK
    er = EXP_ROWS * PEER_KEYS
    nsteps = eu.shape[0] // er
    nt = t // tm
    units = PEER_HEADS * (tm // LANES)
    assert units % nsteps == 0 and (tm // LANES) % (units // nsteps) == 0, "whole routing units of one head per step"
    spr = nsteps // PEER_HEADS
    qw = 2 * PEER_HALF
    prev = lambda i, s: (jnp.maximum(i - 1, 0), 0)
    return pl.pallas_call(
        _peer_kernel,
        grid=(nt + 1, nsteps),
        in_specs=[pl.BlockSpec((tm, d), prev),
                  pl.BlockSpec((tm, d), prev),
                  pl.BlockSpec((tm, d), lambda i, s: (jnp.minimum(i, nt - 1), 0)),
                  pl.BlockSpec((d, qw), lambda i, s: (0, s // spr)),
                  pl.BlockSpec((1, 2, PEER_KEYS, PEER_HALF), lambda i, s: (s // spr, 0, 0, 0)),
                  pl.BlockSpec((er, d), lambda i, s: (s, 0)),
                  pl.BlockSpec((er, d), lambda i, s: (s, 0))],
        out_specs=pl.BlockSpec((tm, d), prev),
        out_shape=jax.ShapeDtypeStruct((t, d), F32),
        scratch_shapes=[pltpu.VMEM((PEER_KEYS * G_PITCH, LANES), jnp.uint32),
                        pltpu.VMEM((hk, tm), F32), pltpu.VMEM((hk, tm), F32),
                        pltpu.VMEM((tm, hk), I32), pltpu.VMEM((tm, hk), F32),
                        pltpu.VMEM((2, PEER_TOPK, LANES), F32), pltpu.VMEM((2, PEER_TOPK, LANES), F32),
                        pltpu.VMEM((PEER_TOPK, LANES), F32), pltpu.VMEM((PEER_TOPK, LANES), F32)],
        compiler_params=_cparams(("arbitrary", "arbitrary")),
        name="peer_ffn",
    )(h1, n2, n2, wq, keys, eu, ev)


def _ple_kernel(h_ref, p_ref, gp_ref, wg_ref, wp_ref, gf_ref, y_ref):
    h = h_ref[...]
    gate = jax.nn.sigmoid(_dot(_rms(h, gp_ref[...]).astype(BF16), wg_ref[...]))
    h = h + gate * _dot(p_ref[...].astype(BF16), wp_ref[...])
    y_ref[...] = _rms(h, gf_ref[...])


def _ple(h2, p2d, gp, wg, wp, gfin):
    t, d = h2.shape
    pd = p2d.shape[1]
    tm = min(TM_PLE, t)
    vec = lambda: pl.BlockSpec((1, d), lambda i: (0, 0))
    tok = lambda: pl.BlockSpec((tm, d), lambda i: (i, 0))
    return pl.pallas_call(
        _ple_kernel,
        grid=(t // tm,),
        in_specs=[tok(), pl.BlockSpec((tm, pd), lambda i: (i, 0)),
                  vec(), pl.BlockSpec((d, d), lambda i: (0, 0)), pl.BlockSpec((pd, d), lambda i: (0, 0)), vec()],
        out_specs=tok(),
        out_shape=jax.ShapeDtypeStruct((t, d), F32),
        compiler_params=_cparams(("parallel",)),
        name="ple_final",
    )(h2, p2d, gp, wg, wp, gfin)


def _pad_lanes(v):
    return jnp.pad(v, (0, LANES - v.shape[0]))


def _prepare(norm_mix, w_in, ssd_conv_w, ssd_conv_b, ssd_a_log, ssd_dt_bias, ssd_d, ssd_norm, w_ssd_out,
             gdn_conv_w, gdn_a_log, gdn_dt_bias, gdn_norm, w_gdn_out, w_out, norm_ffn, peer_query, peer_keys,
             expert_u, expert_v, norm_ple, w_ple_gate, w_ple_proj, norm_final):
    w = w_in[0]
    xbc = SSD_INNER + 2 * SSD_GROUPS * SSD_STATE
    c_dt = SSD_INNER + xbc
    c_qkv = c_dt + 2 * SSD_HEADS
    c_a = c_qkv + 4 * GDN_INNER
    c_ga = c_a + 4 * GDN_HEADS
    w_main = jnp.concatenate([w[:, :c_dt], w[:, c_qkv:c_a], w[:, c_ga:]], axis=1).astype(BF16)
    w_small = jnp.concatenate([w[:, c_dt:c_qkv], w[:, c_a:c_ga]], axis=1)
    w_small = jnp.pad(w_small, ((0, 0), (0, LANES - w_small.shape[1]))).astype(BF16)
    hp = jnp.stack([
        _pad_lanes(jnp.concatenate([ssd_dt_bias[0].reshape(-1), gdn_dt_bias[0].reshape(-1)])),
        _pad_lanes(jnp.concatenate([ssd_a_log[0].reshape(-1), gdn_a_log[0].reshape(-1)])),
        _pad_lanes(ssd_d[0]),
    ])
    hp = jnp.pad(hp, ((0, SUBLANES - hp.shape[0]), (0, 0)))
    row = lambda v: v.reshape(1, -1)
    return dict(
        g_mix=row(norm_mix[0]), w_main=w_main, w_small=w_small, hp=hp,
        ssd_conv_w=ssd_conv_w[0], ssd_conv_b=row(ssd_conv_b[0]), ssd_norm=row(ssd_norm[0]),
        gdn_conv_w=gdn_conv_w[0], gdn_norm=row(gdn_norm[0]),
        wa=w_ssd_out[0].astype(BF16), wb=w_gdn_out[0].astype(BF16), wo=w_out[0].astype(BF16),
        g_ffn=row(norm_ffn[0]), wq=peer_query[0].astype(BF16),
        keys=peer_keys[0].astype(BF16),
        eu=expert_u[0].astype(BF16), ev=expert_v[0].astype(BF16),
        g_ple=row(norm_ple[0]), wg=w_ple_gate[0].astype(BF16), wp=w_ple_proj[0].astype(BF16),
        g_fin=row(norm_final),
    )


def _trunk(x, ple, p):
    b, length, d = x.shape
    t = b * length
    x2d = x.reshape(t, d)
    proj, small = _in_proj(x2d, p["g_mix"], p["w_main"], p["w_small"])
    proj3 = proj.reshape(b, length, -1)
    small3 = small.reshape(b, length, LANES)
    ya = _ssd_mixer(proj3, small3, p["ssd_conv_w"], p["ssd_conv_b"], p["hp"], p["ssd_norm"])
    yb = _gdn_mixer(proj3, small3, p["gdn_conv_w"], p["hp"], p["gdn_norm"])
    h1, n2 = _merge(x2d, ya.reshape(t, -1), yb.reshape(t, -1), proj, p["wa"], p["wb"], p["wo"], p["g_ffn"])
    h2 = _peer(h1, n2, p["wq"], p["keys"], p["eu"], p["ev"])
    y = _ple(h2, ple.reshape(t, -1), p["g_ple"], p["wg"], p["wp"], p["g_fin"])
    return y.reshape(b, length, d)


def kernel(x_prompt, x_sample, p_prompt, p_sample, norm_mix, w_in, ssd_conv_w, ssd_conv_b, ssd_a_log, ssd_dt_bias, ssd_d, ssd_norm, w_ssd_out, gdn_conv_w, gdn_a_log, gdn_dt_bias, gdn_norm, w_gdn_out, w_out, norm_ffn, peer_query, peer_keys, expert_u, expert_v, norm_ple, w_ple_gate, w_ple_proj, norm_final):
    assert w_in.shape[0] == 1, "single-layer trunk"
    p = _prepare(norm_mix, w_in, ssd_conv_w, ssd_conv_b, ssd_a_log, ssd_dt_bias, ssd_d, ssd_norm, w_ssd_out,
                 gdn_conv_w, gdn_a_log, gdn_dt_bias, gdn_norm, w_gdn_out, w_out, norm_ffn, peer_query, peer_keys,
                 expert_u, expert_v, norm_ple, w_ple_gate, w_ple_proj, norm_final)
    return (_trunk(x_prompt, p_prompt[0], p), _trunk(x_sample, p_sample[0], p))
```
